```python
import math
import jax, jax.numpy as jnp
from jax import lax
import numpy as np

D_MODEL = 1024
BATCH = 1
SEQ = 16384
DEPTH = 2
DEC_BATCH = 16
DEC_SEQ = 16
PAST_LEN = 1024

CHUNK = 64
QBLOCK = 128
MLA_H = 8
NOPE_DIM = 64
ROPE_DIM = 32
MLA_DV = 64
Q_LORA = 384
KV_LORA = 256
MLA_W = MLA_H * MLA_DV
MLA_SCALE = 1.0 / math.sqrt(NOPE_DIM + ROPE_DIM)
ROPE_THETA = 10000.0
FOX_H = 8
FOX_DH = 64
FOX_W = FOX_H * FOX_DH
FOX_SCALE = 1.0 / math.sqrt(FOX_DH)
IN_SIZES = (Q_LORA, KV_LORA, ROPE_DIM, MLA_W, FOX_W, FOX_W, FOX_W, FOX_H, FOX_W, D_MODEL, D_MODEL)
IN_WIDTH = Q_LORA + KV_LORA + ROPE_DIM + MLA_W + 4 * FOX_W + FOX_H + 2 * D_MODEL
EPS = 1e-6
NEG = -1e30

kernel_name = "chunk_causal_mla_fox_hybrid_step"


def rmsnorm(x, g):
    x32 = x.astype(jnp.float32)
    y = x32 * lax.rsqrt(jnp.mean(x32 * x32, axis=-1, keepdims=True) + EPS)
    return (y * g.astype(jnp.float32)).astype(x.dtype)


def rope(x, pos):
    half = x.shape[-1] // 2
    inv = jnp.exp(-math.log(ROPE_THETA) * jnp.arange(half, dtype=jnp.float32) / half)
    ang = pos.astype(jnp.float32)[:, None] * inv[None, :]
    cos = jnp.cos(ang)[:, None, :].astype(x.dtype)
    sin = jnp.sin(ang)[:, None, :].astype(x.dtype)
    x1, x2 = x[..., :half], x[..., half:]
    return jnp.concatenate([x1 * cos - x2 * sin, x1 * sin + x2 * cos], axis=-1)


def project(h, pos, w_in, g_q, w_uq, g_kv, b_f):
    B, T, _ = h.shape
    z = h @ w_in
    offs = np.cumsum(IN_SIZES)[:-1].tolist()
    z_cq, z_ckv, z_kr, za, fq, fk, fv, zf, zb, ga, gb = jnp.split(z, offs, axis=-1)
    cq = rmsnorm(z_cq, g_q)
    q = (cq @ w_uq).reshape(B, T, MLA_H, NOPE_DIM + ROPE_DIM)
    q = jnp.concatenate([q[..., :NOPE_DIM], rope(q[..., NOPE_DIM:], pos)], axis=-1)
    ckv = rmsnorm(z_ckv, g_kv)
    kr = rope(z_kr[:, :, None, :], pos)[:, :, 0, :]
    fq = fq.reshape(B, T, FOX_H, FOX_DH)
    fk = fk.reshape(B, T, FOX_H, FOX_DH)
    fv = fv.reshape(B, T, FOX_H, FOX_DH)
    logf = jax.nn.log_sigmoid((zf + b_f).astype(jnp.float32))
    return q, ckv, kr, za, fq, fk, fv, logf, zb, ga, gb


def mla_expand(ckv, kr, w_ukv):
    B, L, _ = ckv.shape
    kv = (ckv @ w_ukv).reshape(B, L, MLA_H, NOPE_DIM + MLA_DV)
    k = jnp.concatenate([kv[..., :NOPE_DIM], jnp.broadcast_to(kr[:, :, None, :], (B, L, MLA_H, ROPE_DIM))], axis=-1)
    return k, kv[..., NOPE_DIM:]


def attend(q, k, v, qpos, kpos, scale, chunk_causal, cq=None, ck=None):
    s = jnp.einsum('bqhd,bkhd->bhqk', q, k).astype(jnp.float32) * scale
    if cq is not None:
        s = s + (jnp.transpose(cq, (0, 2, 1))[:, :, :, None] - jnp.transpose(ck, (0, 2, 1))[:, :, None, :])
    if chunk_causal:
        ok = (kpos[None, :] // CHUNK) <= (qpos[:, None] // CHUNK)
    else:
        ok = kpos[None, :] <= qpos[:, None]
    s = jnp.where(ok, s, NEG)
    p = jax.nn.softmax(s, axis=-1).astype(v.dtype)
    return jnp.einsum('bhqk,bkhd->bqhd', p, v)


def blocked_attend(q, k, v, pos, scale, chunk_causal, cum=None):
    B, S, H, D = q.shape
    nb = S // QBLOCK
    qb = q.reshape(B, nb, QBLOCK, H, D).swapaxes(0, 1)
    pb = pos.reshape(nb, QBLOCK)
    if cum is None:
        out = lax.map(lambda a: attend(a[0], k, v, a[1], pos, scale, chunk_causal), (qb, pb))
    else:
        cb = cum.reshape(B, nb, QBLOCK, H).swapaxes(0, 1)
        out = lax.map(lambda a: attend(a[0], k, v, a[1], pos, scale, chunk_causal, a[2], cum), (qb, pb, cb))
    return out.swapaxes(0, 1).reshape(B, S, H, v.shape[-1])


def merge(o_a, za, o_b, zb, ga, gb, w_oa, w_ob, w_out):
    B, T = o_a.shape[:2]
    a = (o_a.reshape(B, T, MLA_W) * jax.nn.silu(za)) @ w_oa
    b = (o_b.reshape(B, T, FOX_W) * jax.nn.silu(zb)) @ w_ob
    return (jax.nn.sigmoid(ga) * a + jax.nn.sigmoid(gb) * b) @ w_out


def setup_inputs(seed: int = 0) -> dict:
    key = jax.random.key(seed)
    ks = jax.random.split(key, 24)
    n = lambda k, s, sc: jax.random.normal(k, s, jnp.float32) * sc
    return {
        "x_prompt": n(ks[0], (BATCH, SEQ, D_MODEL), 1.0),
        "x_sample": n(ks[1], (DEC_BATCH, DEC_SEQ, D_MODEL), 1.0),
        "cache_mla_ckv": n(ks[2], (DEPTH, DEC_BATCH, PAST_LEN, KV_LORA), 1.0),
        "cache_mla_krope": n(ks[3], (DEPTH, DEC_BATCH, PAST_LEN, ROPE_DIM), 1.0),
        "cache_fox_k": n(ks[4], (DEPTH, DEC_BATCH, PAST_LEN, FOX_H, FOX_DH), 1.0),
        "cache_fox_v": n(ks[5], (DEPTH, DEC_BATCH, PAST_LEN, FOX_H, FOX_DH), 1.0),
        "cache_fox_logf": jax.nn.log_sigmoid(2.0 + n(ks[6], (DEPTH, DEC_BATCH, PAST_LEN, FOX_H), 0.5)),
        "norm_g": 1.0 + n(ks[7], (DEPTH, D_MODEL), 0.02),
        "w_in": n(ks[8], (DEPTH, D_MODEL, IN_WIDTH), D_MODEL ** -0.5),
        "g_q": 1.0 + n(ks[9], (DEPTH, Q_LORA), 0.02),
        "w_uq": n(ks[10], (DEPTH, Q_LORA, MLA_H * (NOPE_DIM + ROPE_DIM)), Q_LORA ** -0.5),
        "g_kv": 1.0 + n(ks[11], (DEPTH, KV_LORA), 0.02),
        "w_ukv": n(ks[12], (DEPTH, KV_LORA, MLA_H * (NOPE_DIM + MLA_DV)), KV_LORA ** -0.5),
        "b_f": 2.0 + n(ks[13], (DEPTH, FOX_H), 0.5),
        "w_oa": n(ks[14], (DEPTH, MLA_W, D_MODEL), MLA_W ** -0.5),
        "w_ob": n(ks[15], (DEPTH, FOX_W, D_MODEL), FOX_W ** -0.5),
        "w_out": n(ks[16], (DEPTH, D_MODEL, D_MODEL), D_MODEL ** -0.5),
        "final_g": 1.0 + n(ks[17], (D_MODEL,), 0.02),
    }


def reference(x_prompt, x_sample, cache_mla_ckv, cache_mla_krope, cache_fox_k, cache_fox_v, cache_fox_logf,
              norm_g, w_in, g_q, w_uq, g_kv, w_ukv, b_f, w_oa, w_ob, w_out, final_g):
    S = x_prompt.shape[1]
    T = x_sample.shape[1]
    P = cache_mla_ckv.shape[2]
    pos_p = jnp.arange(S, dtype=jnp.int32)
    pos_s = P + jnp.arange(T, dtype=jnp.int32)
    kpos_s = jnp.arange(P + T, dtype=jnp.int32)
    xp, xs = x_prompt, x_sample
    p_ckv, p_kr, p_fk, p_fv, p_lf = [], [], [], [], []
    s_ckv, s_kr, s_fk, s_fv, s_lf = [], [], [], [], []
    for l in range(DEPTH):
        h = rmsnorm(xp, norm_g[l])
        q, ckv, kr, za, fq, fk, fv, logf, zb, ga, gb = project(h, pos_p, w_in[l], g_q[l], w_uq[l], g_kv[l], b_f[l])
        k_a, v_a = mla_expand(ckv, kr, w_ukv[l])
        o_a = blocked_attend(q, k_a, v_a, pos_p, MLA_SCALE, True)
        cum = jnp.cumsum(logf, axis=1)
        o_b = blocked_attend(fq, fk, fv, pos_p, FOX_SCALE, False, cum)
        xp = xp + merge(o_a, za, o_b, zb, ga, gb, w_oa[l], w_ob[l], w_out[l])
        p_ckv.append(ckv); p_kr.append(kr); p_fk.append(fk); p_fv.append(fv); p_lf.append(logf)

        h = rmsnorm(xs, norm_g[l])
        q, ckv, kr, za, fq, fk, fv, logf, zb, ga, gb = project(h, pos_s, w_in[l], g_q[l], w_uq[l], g_kv[l], b_f[l])
        ckv_all = jnp.concatenate([cache_mla_ckv[l].astype(ckv.dtype), ckv], axis=1)
        kr_all = jnp.concatenate([cache_mla_krope[l].astype(kr.dtype), kr], axis=1)
        k_a, v_a = mla_expand(ckv_all, kr_all, w_ukv[l])
        o_a = attend(q, k_a, v_a, pos_s, kpos_s, MLA_SCALE, True)
        fk_all = jnp.concatenate([cache_fox_k[l].astype(fk.dtype), fk], axis=1)
        fv_all = jnp.concatenate([cache_fox_v[l].astype(fv.dtype), fv], axis=1)
        cum = jnp.cumsum(jnp.concatenate([cache_fox_logf[l].astype(jnp.float32), logf], axis=1), axis=1)
        o_b = attend(fq, fk_all, fv_all, pos_s, kpos_s, FOX_SCALE, False, cum[:, P:], cum)
        xs = xs + merge(o_a, za, o_b, zb, ga, gb, w_oa[l], w_ob[l], w_out[l])
        s_ckv.append(ckv); s_kr.append(kr); s_fk.append(fk); s_fv.append(fv); s_lf.append(logf)

    y_prompt = rmsnorm(xp, final_g)
    y_sample = rmsnorm(xs, final_g)
    return (y_prompt, y_sample,
            jnp.stack(p_ckv), jnp.stack(p_kr), jnp.stack(p_fk), jnp.stack(p_fv), jnp.stack(p_lf),
            jnp.stack(s_ckv), jnp.stack(s_kr), jnp.stack(s_fk), jnp.stack(s_fv), jnp.stack(s_lf))
```

```python
import functools
import math

import numpy as np
import jax
import jax.numpy as jnp
from jax import lax
from jax.experimental import pallas as pl
from jax.experimental.pallas import tpu as pltpu

D_MODEL = 1024
N_HEADS = 8
NOPE_DIM = 64
ROPE_DIM = 32
HALF_ROPE = ROPE_DIM // 2
HEAD_DV = 64
Q_LORA = 384
KV_LORA = 256
HEADS_W = N_HEADS * HEAD_DV
CHUNK = 64
ROPE_THETA = 10000.0
MLA_SCALE = 1.0 / math.sqrt(NOPE_DIM + ROPE_DIM)
FOX_SCALE = 1.0 / math.sqrt(HEAD_DV)
EPS = 1e-6
NEG = -1e30
IN_SIZES = (Q_LORA, KV_LORA, ROPE_DIM, HEADS_W, HEADS_W, HEADS_W, HEADS_W, N_HEADS, HEADS_W, D_MODEL, D_MODEL)

LANES = 128
SEQ_BLOCK = 512
AUG_PER_HEAD = 8
VMEM_LIMIT_BYTES = 56 * 1024 * 1024

F32 = jnp.float32
BF16 = jnp.bfloat16


def _dot(a, b):
    return jnp.dot(a, b, preferred_element_type=F32)


def _dot_nt(a, b):
    return lax.dot_general(a, b, (((1,), (1,)), ((), ())), preferred_element_type=F32)


def _rms(x, g):
    return x * lax.rsqrt(jnp.mean(x * x, axis=-1, keepdims=True) + EPS) * g


def _sigmoid(x):
    return 1.0 / (1.0 + jnp.exp(-x))


def _split3(x):
    hi = x.astype(BF16)
    r1 = x - hi.astype(F32)
    mid = r1.astype(BF16)
    lo = (r1 - mid.astype(F32)).astype(BF16)
    return hi, mid, lo


def _full_spec(shape):
    nd = len(shape)
    return pl.BlockSpec(shape, lambda *_: (0,) * nd)


def _params(n_axes):
    return pltpu.CompilerParams(dimension_semantics=("arbitrary",) * n_axes,
                                vmem_limit_bytes=VMEM_LIMIT_BYTES)


MLA_IN_COLS = Q_LORA + KV_LORA + 2 * LANES + HEADS_W + D_MODEL


def _proj_mla_kernel(x_ref, g_ref, w_ref, gq_ref, gkv_ref, wuq_ref, wukn_ref, wuvt_ref,
                     cosT_ref, sinT_ref, ctok_ref, stok_ref,
                     ckv_ref, kr_ref, za_ref, ga_ref, qnT_ref, qrT_ref, knb_ref, krb_ref, vT_ref):
    hb = _rms(x_ref[...], g_ref[...]).astype(BF16)
    o = 0
    zcq = _dot(hb, w_ref[:, o:o + Q_LORA]); o += Q_LORA
    zckv = _dot(hb, w_ref[:, o:o + KV_LORA]); o += KV_LORA
    zka = _dot(hb, w_ref[:, o:o + LANES]); o += LANES
    zkb = _dot(hb, w_ref[:, o:o + LANES]); o += LANES
    za_ref[...] = _dot(hb, w_ref[:, o:o + HEADS_W]); o += HEADS_W
    ga_ref[...] = _dot(hb, w_ref[:, o:o + D_MODEL])

    cqb = _rms(zcq, gq_ref[...]).astype(BF16)
    qT = _dot_nt(wuq_ref[...], cqb)
    qnT_ref[...] = (qT[0:HEADS_W] * MLA_SCALE).astype(BF16)
    x1 = qT[HEADS_W:HEADS_W + LANES]
    x2 = qT[HEADS_W + LANES:HEADS_W + 2 * LANES]
    c = cosT_ref[...]
    s = sinT_ref[...]
    n1 = ((x1 * c - x2 * s) * MLA_SCALE).astype(BF16)
    n2 = ((x1 * s + x2 * c) * MLA_SCALE).astype(BF16)
    for h in range(N_HEADS):
        qrT_ref[h * ROPE_DIM:h * ROPE_DIM + HALF_ROPE, :] = n1[h * HALF_ROPE:(h + 1) * HALF_ROPE]
        qrT_ref[h * ROPE_DIM + HALF_ROPE:(h + 1) * ROPE_DIM, :] = n2[h * HALF_ROPE:(h + 1) * HALF_ROPE]

    ckv = _rms(zckv, gkv_ref[...])
    ckv_ref[...] = ckv
    cb = ckv.astype(BF16)
    knb_ref[0] = _dot(cb, wukn_ref[...]).astype(BF16)
    vT_ref[0] = _dot_nt(wuvt_ref[...], cb).astype(BF16)

    kr = zka * ctok_ref[...] + zkb * stok_ref[...]
    kr_ref[...] = kr
    krb_ref[0] = kr.astype(BF16)


def _proj_mla(x, g, w, gq, gkv, wuq, wukn, wuvt, cosT, sinT, ctok, stok, tm):
    m = x.shape[0]
    nb = m // tm
    row = lambda c: pl.BlockSpec((tm, c), lambda i: (i, 0))
    colT = lambda r: pl.BlockSpec((r, tm), lambda i: (0, i))
    blk3 = lambda a, b: pl.BlockSpec((1, a, b), lambda i: (i, 0, 0))
    out_shape = (
        jax.ShapeDtypeStruct((m, KV_LORA), F32),
        jax.ShapeDtypeStruct((m, LANES), F32),
        jax.ShapeDtypeStruct((m, HEADS_W), F32),
        jax.ShapeDtypeStruct((m, D_MODEL), F32),
        jax.ShapeDtypeStruct((HEADS_W, m), BF16),
        jax.ShapeDtypeStruct((N_HEADS * ROPE_DIM, m), BF16),
        jax.ShapeDtypeStruct((nb, tm, HEADS_W), BF16),
        jax.ShapeDtypeStruct((nb, tm, LANES), BF16),
        jax.ShapeDtypeStruct((nb, HEADS_W, tm), BF16),
    )
    out_specs = (row(KV_LORA), row(LANES), row(HEADS_W), row(D_MODEL), colT(HEADS_W),
                 colT(N_HEADS * ROPE_DIM), blk3(tm, HEADS_W), blk3(tm, LANES), blk3(HEADS_W, tm))
    in_specs = [row(D_MODEL), _full_spec(g.shape), _full_spec(w.shape), _full_spec(gq.shape),
                _full_spec(gkv.shape), _full_spec(wuq.shape), _full_spec(wukn.shape), _full_spec(wuvt.shape),
                colT(LANES), colT(LANES), row(LANES), row(LANES)]
    return pl.pallas_call(
        _proj_mla_kernel, grid=(nb,), in_specs=in_specs, out_specs=out_specs, out_shape=out_shape,
        compiler_params=_params(1), name="proj_mla",
    )(x, g, w, gq, gkv, wuq, wukn, wuvt, cosT, sinT, ctok, stok)


FOX_IN_COLS = 2 * HEADS_W + LANES + HEADS_W + D_MODEL


def _proj_fox_kernel(x_ref, g_ref, w_ref, wfqt_ref, wfvt_ref, bf_ref, tri_ref, pk_ref, pqt_ref,
                     fk_ref, fv_ref, logf_ref, zb_ref, gb_ref, fqT_ref, fkb_ref, fvT_ref,
                     augk_ref, augqT_ref, base_ref, carry_ref):
    i = pl.program_id(0)
    tm = x_ref.shape[0]
    hb = _rms(x_ref[...], g_ref[...]).astype(BF16)
    o = 0
    fk = _dot(hb, w_ref[:, o:o + HEADS_W]); o += HEADS_W
    fk_ref[...] = fk
    fkb_ref[0] = fk.astype(BF16)
    fv_ref[...] = _dot(hb, w_ref[:, o:o + HEADS_W]); o += HEADS_W
    zf = _dot(hb, w_ref[:, o:o + LANES]) + bf_ref[...]; o += LANES
    zb_ref[...] = _dot(hb, w_ref[:, o:o + HEADS_W]); o += HEADS_W
    gb_ref[...] = _dot(hb, w_ref[:, o:o + D_MODEL])
    fqT_ref[...] = (_dot_nt(wfqt_ref[...], hb) * FOX_SCALE).astype(BF16)
    fvT_ref[0] = _dot_nt(wfvt_ref[...], hb).astype(BF16)

    lane = lax.broadcasted_iota(jnp.int32, (tm, LANES), 1)
    logf = jnp.minimum(zf, 0.0) - jnp.log(1.0 + jnp.exp(-jnp.abs(zf)))
    logf = jnp.where(lane < N_HEADS, logf, 0.0)
    logf_ref[...] = logf[:, 0:N_HEADS]

    hi, mid, lo = _split3(logf)
    tri = tri_ref[...]
    r = _dot(tri, hi) + _dot(tri, mid) + _dot(tri, lo)

    @pl.when(i == 0)
    def _():
        carry_ref[...] = jnp.zeros_like(carry_ref)

    base_ref[0] = carry_ref[...]
    carry_ref[...] = carry_ref[...] + r[tm - 1:tm, :]

    rh, rm, rl = _split3(r)
    rcat = jnp.concatenate([rh, rm, rl], axis=1)
    slot = lane % AUG_PER_HEAD
    ones_k = jnp.where((lane < N_HEADS * AUG_PER_HEAD) & (slot < 3), 1.0, 0.0)
    augk_ref[0] = (_dot(rcat, pk_ref[...]) + ones_k).astype(BF16)
    rowq = lax.broadcasted_iota(jnp.int32, (N_HEADS * AUG_PER_HEAD, tm), 0) % AUG_PER_HEAD
    ones_q = jnp.where((rowq >= 3) & (rowq < 6), 1.0, 0.0)
    augqT_ref[...] = (_dot_nt(pqt_ref[...], rcat) + ones_q).astype(BF16)


def _proj_fox(x, g, w, wfqt, wfvt, bf, tri, pk, pqt, tm):
    m = x.shape[0]
    nb = m // tm
    row = lambda c: pl.BlockSpec((tm, c), lambda i: (i, 0))
    colT = lambda r: pl.BlockSpec((r, tm), lambda i: (0, i))
    blk3 = lambda a, b: pl.BlockSpec((1, a, b), lambda i: (i, 0, 0))
    out_shape = (
        jax.ShapeDtypeStruct((m, HEADS_W), F32),
        jax.ShapeDtypeStruct((m, HEADS_W), F32),
        jax.ShapeDtypeStruct((m, N_HEADS), F32),
        jax.ShapeDtypeStruct((m, HEADS_W), F32),
        jax.ShapeDtypeStruct((m, D_MODEL), F32),
        jax.ShapeDtypeStruct((HEADS_W, m), BF16),
        jax.ShapeDtypeStruct((nb, tm, HEADS_W), BF16),
        jax.ShapeDtypeStruct((nb, HEADS_W, tm), BF16),
        jax.ShapeDtypeStruct((nb, tm, LANES), BF16),
        jax.ShapeDtypeStruct((N_HEADS * AUG_PER_HEAD, m), BF16),
        jax.ShapeDtypeStruct((nb, 1, LANES), F32),
    )
    out_specs = (row(HEADS_W), row(HEADS_W), row(N_HEADS), row(HEADS_W), row(D_MODEL), colT(HEADS_W),
                 blk3(tm, HEADS_W), blk3(HEADS_W, tm), blk3(tm, LANES), colT(N_HEADS * AUG_PER_HEAD),
                 blk3(1, LANES))
    in_specs = [row(D_MODEL)] + [_full_spec(a.shape) for a in (g, w, wfqt, wfvt, bf, tri, pk, pqt)]
    return pl.pallas_call(
        _proj_fox_kernel, grid=(nb,), in_specs=in_specs, out_specs=out_specs, out_shape=out_shape,
        scratch_shapes=[pltpu.VMEM((1, LANES), F32)],
        compiler_params=_params(1), name="proj_fox",
    )(x, g, w, wfqt, wfvt, bf, tri, pk, pqt)


def _attn_kernel(base_ref, qp_ref, qe_ref, ka_ref, ke_ref, vT_ref, o_ref, *, fox):
    p = pl.program_id(0)
    i = pl.program_id(1)
    tq = qp_ref.shape[1]
    tk = ka_ref.shape[1]
    rows_p = lax.broadcasted_iota(jnp.int32, (LANES, tq), 0)
    rows_e = lax.broadcasted_iota(jnp.int32, qe_ref.shape, 0)
    kpos = lax.broadcasted_iota(jnp.int32, (tk, tq), 0)
    qpos = lax.broadcasted_iota(jnp.int32, (tk, tq), 1)
    if fox:
        visible = kpos <= qpos
    else:
        visible = (kpos // CHUNK) <= (qpos // CHUNK)
    qp = qp_ref[...]
    qe = qe_ref[...]

    outs = []
    for hh in range(2):
        h = 2 * p + hh
        keep_p = (rows_p >= hh * HEAD_DV) & (rows_p < (hh + 1) * HEAD_DV)
        if fox:
            keep_e = (rows_e >= h * AUG_PER_HEAD) & (rows_e < (h + 1) * AUG_PER_HEAD)
            extra = jnp.where(keep_e, qe, jnp.zeros_like(qe))
        else:
            extra = qe[hh * ROPE_DIM:(hh + 1) * ROPE_DIM]
        zero_rows = jnp.zeros((LANES - extra.shape[0], tq), BF16)
        w = jnp.concatenate([jnp.where(keep_p, qp, jnp.zeros_like(qp)), extra, zero_rows], axis=0)

        def block(j, carry, masked):
            m_old, l_old, acc = carry
            lhs = jnp.concatenate([ka_ref[j], ke_ref[j]], axis=1)
            sT = _dot(lhs, w)
            if masked:
                sT = jnp.where(visible, sT, NEG)
            if fox:
                d = base_ref[h, i] - base_ref[h, j]
            else:
                d = 0.0
            m_new = jnp.maximum(m_old, jnp.max(sT, axis=0, keepdims=True) + d)
            alpha = jnp.exp(m_old - m_new)
            pT = jnp.exp(sT - (m_new - d))
            l_new = alpha * l_old + jnp.sum(pT, axis=0, keepdims=True)
            v = vT_ref[j, hh * HEAD_DV:(hh + 1) * HEAD_DV, :]
            acc = alpha * acc + _dot(v, pT.astype(BF16))
            return m_new, l_new, acc

        init = (jnp.full((1, tq), NEG, F32), jnp.zeros((1, tq), F32), jnp.zeros((HEAD_DV, tq), F32))
        carry = lax.fori_loop(0, i, lambda j, c: block(j, c, False), init)
        _, l_fin, acc = block(i, carry, True)
        outs.append(acc / l_fin)
    o_ref[...] = jnp.concatenate(outs, axis=0).T


def _attention(base, qpT, qeT, ka, ke, vT, fox):
    nb, tk, _ = ka.shape
    s = qpT.shape[1]
    tq = tk
    e_rows = qeT.shape[0] if fox else 2 * ROPE_DIM
    e_map = (lambda p, i, b: (0, i)) if fox else (lambda p, i, b: (p, i))
    grid_spec = pltpu.PrefetchScalarGridSpec(
        num_scalar_prefetch=1,
        grid=(N_HEADS // 2, nb),
        in_specs=[
            pl.BlockSpec((LANES, tq), lambda p, i, b: (p, i)),
            pl.BlockSpec((e_rows, tq), e_map),
            pl.BlockSpec((nb, tk, LANES), lambda p, i, b: (0, 0, p)),
            pl.BlockSpec((nb, tk, LANES), lambda p, i, b: (0, 0, 0)),
            pl.BlockSpec((nb, LANES, tk), lambda p, i, b: (0, p, 0)),
        ],
        out_specs=pl.BlockSpec((tq, LANES), lambda p, i, b: (i, p)),
    )
    return pl.pallas_call(
        functools.partial(_attn_kernel, fox=fox), grid_spec=grid_spec,
        out_shape=jax.ShapeDtypeStruct((s, HEADS_W), F32),
        compiler_params=_params(2), name="attn_fox" if fox else "attn_mla",
    )(base, qpT, qeT, ka, ke, vT)


def _softmax_pv(s, v):
    m = jnp.max(s, axis=-1, keepdims=True)
    p = jnp.exp(s - m)
    l = jnp.sum(p, axis=-1, keepdims=True)
    return _dot(p.astype(BF16), v) / l


def _diag_heads(o_big, t_new):
    rows = lax.broadcasted_iota(jnp.int32, o_big.shape, 0) // t_new
    cols = lax.broadcasted_iota(jnp.int32, o_big.shape, 1) // HEAD_DV
    kept = jnp.where(rows == cols, o_big, 0.0)
    return jnp.sum(kept.reshape(N_HEADS, t_new, o_big.shape[1]), axis=0)


def _sample_attn_kernel(qn_ref, qr_ref, qf_ref, lrep_ref, cckv_ref, nckv_ref, ckr_ref, nkr_ref,
                        cfk_ref, nfk_ref, cfv_ref, nfv_ref, wukn_ref, wuv_ref, oa_ref, ob_ref):
    past = cckv_ref.shape[1]
    t_new = nckv_ref.shape[0]
    rows = N_HEADS * t_new
    pad = lrep_ref.shape[2] - past - t_new
    keys = past + t_new + pad

    def with_new(cache, new):
        return jnp.concatenate([cache, new, jnp.zeros((pad, new.shape[1]), new.dtype)], axis=0)

    kidx = lax.broadcasted_iota(jnp.int32, (rows, keys), 1)
    qpos = past + lax.broadcasted_iota(jnp.int32, (rows, keys), 0) % t_new
    real = kidx < past + t_new

    ckv_all = with_new(cckv_ref[0], nckv_ref[...]).astype(BF16)
    kn = _dot(ckv_all, wukn_ref[...]).astype(BF16)
    va = _dot(ckv_all, wuv_ref[...]).astype(BF16)
    kr = with_new(ckr_ref[0], nkr_ref[:, 0:ROPE_DIM]).astype(BF16)
    s = _dot_nt(qn_ref[0], kn) + _dot_nt(qr_ref[0], kr)
    s = jnp.where(real & ((kidx // CHUNK) <= (qpos // CHUNK)), s, NEG)
    oa_ref[...] = _diag_heads(_softmax_pv(s, va), t_new)

    fk = with_new(cfk_ref[0], nfk_ref[...]).astype(BF16)
    fv = with_new(cfv_ref[0], nfv_ref[...]).astype(BF16)
    c = lrep_ref[0]
    shift = 1
    while shift < keys:
        c = c + jnp.where(kidx >= shift, pltpu.roll(c, shift, 1), 0.0)
        shift *= 2
    cq = jnp.sum(jnp.where(kidx == qpos, c, 0.0), axis=-1, keepdims=True)
    s = _dot_nt(qf_ref[0], fk) + (cq - c)
    s = jnp.where(real & (kidx <= qpos), s, NEG)
    ob_ref[...] = _diag_heads(_softmax_pv(s, fv), t_new)


def _sample_attention(qn_bd, qr, qf_bd, lrep, cckv, nckv, ckr, nkr, cfk, nfk, cfv, nfv, wukn, wuv):
    nbatch, past, _ = cckv.shape
    t_new = nckv.shape[0] // nbatch
    b3 = lambda a: pl.BlockSpec((1,) + a.shape[1:], lambda b: (b, 0, 0))
    new = lambda a: pl.BlockSpec((t_new, a.shape[1]), lambda b: (b, 0))
    in_specs = [b3(qn_bd), b3(qr), b3(qf_bd), b3(lrep), b3(cckv), new(nckv), b3(ckr), new(nkr),
                b3(cfk), new(nfk), b3(cfv), new(nfv), _full_spec(wukn.shape), _full_spec(wuv.shape)]
    out = jax.ShapeDtypeStruct((nbatch * t_new, HEADS_W), F32)
    o_spec = pl.BlockSpec((t_new, HEADS_W), lambda b: (b, 0))
    return pl.pallas_call(
        _sample_attn_kernel, grid=(nbatch,), in_specs=in_specs, out_specs=(o_spec, o_spec),
        out_shape=(out, out), compiler_params=_params(1), name="attn_sample",
    )(qn_bd, qr, qf_bd, lrep, cckv, nckv, ckr, nkr, cfk, nfk, cfv, nfv, wukn, wuv)


def _merge_kernel(x_ref, oa_ref, za_ref, ob_ref, zb_ref, ga_ref, gb_ref, woa_ref, wob_ref, wout_ref,
                  fg_ref, xn_ref, *maybe_y_ref):
    za = za_ref[...]
    zb = zb_ref[...]
    a = _dot((oa_ref[...] * (za * _sigmoid(za))).astype(BF16), woa_ref[...])
    b = _dot((ob_ref[...] * (zb * _sigmoid(zb))).astype(BF16), wob_ref[...])
    mix = _sigmoid(ga_ref[...]) * a + _sigmoid(gb_ref[...]) * b
    xn = x_ref[...] + _dot(mix.astype(BF16), wout_ref[...])
    xn_ref[...] = xn
    if maybe_y_ref:
        maybe_y_ref[0][...] = _rms(xn, fg_ref[...])


def _merge(x, oa, za, ob, zb, ga, gb, woa, wob, wout, fg, tm, final):
    m = x.shape[0]
    row = lambda c: pl.BlockSpec((tm, c), lambda i: (i, 0))
    in_specs = [row(D_MODEL), row(HEADS_W), row(HEADS_W), row(HEADS_W), row(HEADS_W), row(D_MODEL),
                row(D_MODEL), _full_spec(woa.shape), _full_spec(wob.shape), _full_spec(wout.shape),
                _full_spec(fg.shape)]
    n_out = 2 if final else 1
    out_shape = tuple(jax.ShapeDtypeStruct((m, D_MODEL), F32) for _ in range(n_out))
    out_specs = tuple(row(D_MODEL) for _ in range(n_out))
    return pl.pallas_call(
        _merge_kernel, grid=(m // tm,), in_specs=in_specs, out_specs=out_specs, out_shape=out_shape,
        compiler_params=_params(1), name="merge",
    )(x, oa, za, ob, zb, ga, gb, woa, wob, wout, fg)


def _pad_cols(w, width):
    return jnp.pad(w, ((0, 0), (0, width - w.shape[1])))


def _layer_weights(w_in, w_uq, w_ukv, w_oa, w_ob, w_out, b_f):
    offs = np.cumsum(IN_SIZES)[:-1].tolist()
    w_cq, w_ckv, w_kr, w_za, w_fq, w_fk, w_fv, w_zf, w_zb, w_ga, w_gb = jnp.split(w_in, offs, axis=1)
    w_kr_swapped = jnp.concatenate([w_kr[:, HALF_ROPE:], w_kr[:, :HALF_ROPE]], axis=1)
    w_mla = jnp.concatenate([w_cq, w_ckv, _pad_cols(w_kr, LANES), _pad_cols(w_kr_swapped, LANES),
                             w_za, w_ga], axis=1).astype(BF16)
    w_fox = jnp.concatenate([w_fk, w_fv, _pad_cols(w_zf, LANES), w_zb, w_gb], axis=1).astype(BF16)
    uq = w_uq.reshape(Q_LORA, N_HEADS, NOPE_DIM + ROPE_DIM)
    uq_rows = jnp.concatenate([
        uq[:, :, :NOPE_DIM].reshape(Q_LORA, -1),
        uq[:, :, NOPE_DIM:NOPE_DIM + HALF_ROPE].reshape(Q_LORA, -1),
        uq[:, :, NOPE_DIM + HALF_ROPE:].reshape(Q_LORA, -1)], axis=1)
    ukv = w_ukv.reshape(KV_LORA, N_HEADS, NOPE_DIM + HEAD_DV)
    w_ukn = ukv[:, :, :NOPE_DIM].reshape(KV_LORA, -1).astype(BF16)
    w_uv = ukv[:, :, NOPE_DIM:].reshape(KV_LORA, -1).astype(BF16)
    return dict(
        w_mla=w_mla, w_fox=w_fox, w_uqT=uq_rows.T.astype(BF16), w_ukn=w_ukn, w_uv=w_uv, w_uvT=w_uv.T,
        w_fqT=w_fq.T.astype(BF16), w_fvT=w_fv.T.astype(BF16),
        b_f=_pad_cols(b_f[None, :], LANES),
        w_oa=w_oa.astype(BF16), w_ob=w_ob.astype(BF16), w_out=w_out.astype(BF16))


def _rope_tables(pos):
    inv = jnp.exp(-math.log(ROPE_THETA) * jnp.arange(HALF_ROPE, dtype=F32) / HALF_ROPE)
    ang = pos.astype(F32)[:, None] * inv[None, :]
    cos = jnp.cos(ang)
    sin = jnp.sin(ang)
    cosT = jnp.tile(cos.T, (N_HEADS, 1))
    sinT = jnp.tile(sin.T, (N_HEADS, 1))
    ctok = _pad_cols(jnp.concatenate([cos, cos], axis=1), LANES)
    stok = _pad_cols(jnp.concatenate([-sin, sin], axis=1), LANES)
    return cosT, sinT, ctok, stok


def _bias_placement():
    pk = np.zeros((3 * LANES, LANES), np.float32)
    pqt = np.zeros((N_HEADS * AUG_PER_HEAD, 3 * LANES), np.float32)
    for c in range(3):
        for h in range(N_HEADS):
            pk[c * LANES + h, h * AUG_PER_HEAD + 3 + c] = -1.0
            pqt[h * AUG_PER_HEAD + c, c * LANES + h] = 1.0
    return jnp.asarray(pk, BF16), jnp.asarray(pqt, BF16)


def _tri(n):
    return jnp.asarray(np.tril(np.ones((n, n), np.float32)), BF16)


def _block_diag_queries(qT, nbatch, t_new):
    q = qT.T.reshape(nbatch, t_new, N_HEADS, HEAD_DV)
    eye = jnp.eye(N_HEADS, dtype=q.dtype)
    return jnp.einsum('bthj,hg->bhtgj', q, eye).reshape(nbatch, N_HEADS * t_new, HEADS_W)


def _project(x, lw, norm_g, g_q, g_kv, tables, tri, pk, pqt, tm):
    cosT, sinT, ctok, stok = tables
    g = norm_g[None, :]
    mla = _proj_mla(x, g, lw['w_mla'], g_q[None, :], g_kv[None, :], lw['w_uqT'], lw['w_ukn'], lw['w_uvT'],
                    cosT, sinT, ctok, stok, tm)
    fox = _proj_fox(x, g, lw['w_fox'], lw['w_fqT'], lw['w_fvT'], lw['b_f'], tri, pk, pqt, tm)
    return mla, fox


def kernel(x_prompt, x_sample, cache_mla_ckv, cache_mla_krope, cache_fox_k, cache_fox_v, cache_fox_logf,
           norm_g, w_in, g_q, w_uq, g_kv, w_ukv, b_f, w_oa, w_ob, w_out, final_g):
    depth = w_in.shape[0]
    _, seq, _ = x_prompt.shape
    nbatch, t_new, _ = x_sample.shape
    past = cache_mla_ckv.shape[2]
    m_s = nbatch * t_new
    key_pad = -(past + t_new) % LANES

    tabs_p = _rope_tables(jnp.arange(seq, dtype=jnp.int32))
    tabs_s = _rope_tables(past + jnp.arange(m_s, dtype=jnp.int32) % t_new)
    pk, pqt = _bias_placement()
    tri_p = _tri(SEQ_BLOCK)
    tri_s = _tri(m_s)
    fg = final_g[None, :]

    xp = x_prompt.reshape(seq, D_MODEL)
    xs = x_sample.reshape(m_s, D_MODEL)
    outs = {k: [] for k in ('p_ckv', 'p_kr', 'p_fk', 'p_fv', 'p_lf', 's_ckv', 's_kr', 's_fk', 's_fv', 's_lf')}
    yp = ys = None
    for l in range(depth):
        final = l == depth - 1
        lw = _layer_weights(w_in[l], w_uq[l], w_ukv[l], w_oa[l], w_ob[l], w_out[l], b_f[l])

        mla, fox = _project(xp, lw, norm_g[l], g_q[l], g_kv[l], tabs_p, tri_p, pk, pqt, SEQ_BLOCK)
        ckv, kr, za, ga, qnT, qrT, knb, krb, vT = mla
        fk, fv, logf, zb, gb, fqT, fkb, fvT, augk, augqT, base = fox
        no_base = jnp.zeros((N_HEADS, base.shape[0]), F32)
        o_a = _attention(no_base, qnT, qrT, knb, krb, vT, fox=False)
        o_b = _attention(base[:, 0, :N_HEADS].T, fqT, augqT, fkb, augk, fvT, fox=True)
        res = _merge(xp, o_a, za, o_b, zb, ga, gb, lw['w_oa'], lw['w_ob'], lw['w_out'], fg, SEQ_BLOCK, final)
        xp = res[0]
        if final:
            yp = res[1]
        outs['p_ckv'].append(ckv); outs['p_kr'].append(kr[:, :ROPE_DIM]); outs['p_fk'].append(fk)
        outs['p_fv'].append(fv); outs['p_lf'].append(logf)

        mla, fox = _project(xs, lw, norm_g[l], g_q[l], g_kv[l], tabs_s, tri_s, pk, pqt, m_s)
        ckv, kr, za, ga, qnT, qrT, _, _, _ = mla
        fk, fv, logf, zb, gb, fqT, _, _, _, _, _ = fox
        qn_bd = _block_diag_queries(qnT, nbatch, t_new)
        qf_bd = _block_diag_queries(fqT, nbatch, t_new)
        qr = qrT.reshape(N_HEADS, ROPE_DIM, nbatch, t_new).transpose(2, 0, 3, 1).reshape(
            nbatch, N_HEADS * t_new, ROPE_DIM)
        lf_all = jnp.concatenate([cache_fox_logf[l], logf.reshape(nbatch, t_new, N_HEADS),
                                  jnp.zeros((nbatch, key_pad, N_HEADS), F32)], axis=1)
        lrep = jnp.repeat(lf_all.transpose(0, 2, 1), t_new, axis=1)
        o_a, o_b = _sample_attention(
            qn_bd, qr, qf_bd, lrep, cache_mla_ckv[l], ckv, cache_mla_krope[l], kr,
            cache_fox_k[l].reshape(nbatch, past, HEADS_W), fk,
            cache_fox_v[l].reshape(nbatch, past, HEADS_W), fv, lw['w_ukn'], lw['w_uv'])
        res = _merge(xs, o_a, za, o_b, zb, ga, gb, lw['w_oa'], lw['w_ob'], lw['w_out'], fg, m_s, final)
        xs = res[0]
        if final:
            ys = res[1]
        outs['s_ckv'].append(ckv); outs['s_kr'].append(kr[:, :ROPE_DIM]); outs['s_fk'].append(fk)
        outs['s_fv'].append(fv); outs['s_lf'].append(logf)

    st = lambda name, shape: jnp.stack(outs[name]).reshape((depth,) + shape)
    return (yp.reshape(1, seq, D_MODEL), ys.reshape(nbatch, t_new, D_MODEL),
            st('p_ckv', (1, seq, KV_LORA)), st('p_kr', (1, seq, ROPE_DIM)),
            st('p_fk', (1, seq, N_HEADS, HEAD_DV)), st('p_fv', (1, seq, N_HEADS, HEAD_DV)),
            st('p_lf', (1, seq, N_HEADS)),
            st('s_ckv', (nbatch, t_new, KV_LORA)), st('s_kr', (nbatch, t_new, ROPE_DIM)),
            st('s_fk', (nbatch, t_new, N_HEADS, HEAD_DV)), st('s_fv', (nbatch, t_new, N_HEADS, HEAD_DV)),
            st('s_lf', (nbatch, t_new, N_HEADS)))
```

```python
import functools
import math

import numpy as np
import jax
import jax.numpy as jnp
from jax import lax
from jax.experimental import pallas as pl
from jax.experimental.pallas import tpu as pltpu

D_MODEL = 1024
N_HEADS = 8
NOPE_DIM = 64
ROPE_DIM = 32
HALF_ROPE = ROPE_DIM // 2
HEAD_DV = 64
Q_LORA = 384
KV_LORA = 256
HEADS_W = N_HEADS * HEAD_DV
CHUNK = 64
ROPE_THETA = 10000.0
LOG2E = math.log2(math.e)
MLA_QSCALE = LOG2E / math.sqrt(NOPE_DIM + ROPE_DIM)
FOX_QSCALE = LOG2E / math.sqrt(HEAD_DV)
EPS = 1e-6
NEG = -1e30
IN_SIZES = (Q_LORA, KV_LORA, ROPE_DIM, HEADS_W, HEADS_W, HEADS_W, HEADS_W, N_HEADS, HEADS_W, D_MODEL, D_MODEL)

LANES = 128
SEQ_BLOCK = 512
KEY_SUB = 256
DEN_ROWS = 16
LOOKAHEAD = 2
AUG_PER_HEAD = 8
VMEM_LIMIT_BYTES = 56 * 1024 * 1024

F32 = jnp.float32
BF16 = jnp.bfloat16


def _dot(a, b):
    return jnp.dot(a, b, preferred_element_type=F32)


def _dot_nt(a, b):
    return lax.dot_general(a, b, (((1,), (1,)), ((), ())), preferred_element_type=F32)


def _rms(x, g):
    return x * lax.rsqrt(jnp.mean(x * x, axis=-1, keepdims=True) + EPS) * g


def _sigmoid(x):
    return 1.0 / (1.0 + jnp.exp(-x))


def _split3(x):
    hi = x.astype(BF16)
    r1 = x - hi.astype(F32)
    mid = r1.astype(BF16)
    lo = (r1 - mid.astype(F32)).astype(BF16)
    return hi, mid, lo


def _full_spec(shape):
    nd = len(shape)
    return pl.BlockSpec(shape, lambda *_: (0,) * nd)


def _params(n_axes):
    return pltpu.CompilerParams(dimension_semantics=("arbitrary",) * n_axes,
                                vmem_limit_bytes=VMEM_LIMIT_BYTES)


MLA_IN_COLS = Q_LORA + KV_LORA + 2 * LANES + HEADS_W + D_MODEL


def _proj_mla_kernel(x_ref, g_ref, w_ref, gq_ref, gkv_ref, wuq_ref, wukn_ref, wuvt_ref,
                     cosT_ref, sinT_ref, ctok_ref, stok_ref,
                     ckv_ref, kr_ref, za_ref, ga_ref, qnT_ref, qrT_ref, knb_ref, krb_ref, vT_ref):
    hb = _rms(x_ref[...], g_ref[...]).astype(BF16)
    o = 0
    zcq = _dot(hb, w_ref[:, o:o + Q_LORA]); o += Q_LORA
    zckv = _dot(hb, w_ref[:, o:o + KV_LORA]); o += KV_LORA
    zka = _dot(hb, w_ref[:, o:o + LANES]); o += LANES
    zkb = _dot(hb, w_ref[:, o:o + LANES]); o += LANES
    za_ref[...] = _dot(hb, w_ref[:, o:o + HEADS_W]); o += HEADS_W
    ga_ref[...] = _dot(hb, w_ref[:, o:o + D_MODEL])

    cqb = _rms(zcq, gq_ref[...]).astype(BF16)
    qT = _dot_nt(wuq_ref[...], cqb)
    qnT_ref[...] = (qT[0:HEADS_W] * MLA_QSCALE).astype(BF16)
    x1 = qT[HEADS_W:HEADS_W + LANES]
    x2 = qT[HEADS_W + LANES:HEADS_W + 2 * LANES]
    c = cosT_ref[...]
    s = sinT_ref[...]
    n1 = ((x1 * c - x2 * s) * MLA_QSCALE).astype(BF16)
    n2 = ((x1 * s + x2 * c) * MLA_QSCALE).astype(BF16)
    for h in range(N_HEADS):
        qrT_ref[h * ROPE_DIM:h * ROPE_DIM + HALF_ROPE, :] = n1[h * HALF_ROPE:(h + 1) * HALF_ROPE]
        qrT_ref[h * ROPE_DIM + HALF_ROPE:(h + 1) * ROPE_DIM, :] = n2[h * HALF_ROPE:(h + 1) * HALF_ROPE]

    ckv = _rms(zckv, gkv_ref[...])
    ckv_ref[...] = ckv
    cb = ckv.astype(BF16)
    knb_ref[0] = _dot(cb, wukn_ref[...]).astype(BF16)
    vT_ref[0] = _dot_nt(wuvt_ref[...], cb).astype(BF16)

    kr = zka * ctok_ref[...] + zkb * stok_ref[...]
    kr_ref[...] = kr
    krb_ref[0] = kr.astype(BF16)


def _proj_mla(x, g, w, gq, gkv, wuq, wukn, wuvt, cosT, sinT, ctok, stok, tm):
    m = x.shape[0]
    nb = m // tm
    row = lambda c: pl.BlockSpec((tm, c), lambda i: (i, 0))
    colT = lambda r: pl.BlockSpec((r, tm), lambda i: (0, i))
    blk3 = lambda a, b: pl.BlockSpec((1, a, b), lambda i: (i, 0, 0))
    out_shape = (
        jax.ShapeDtypeStruct((m, KV_LORA), F32),
        jax.ShapeDtypeStruct((m, LANES), F32),
        jax.ShapeDtypeStruct((m, HEADS_W), F32),
        jax.ShapeDtypeStruct((m, D_MODEL), F32),
        jax.ShapeDtypeStruct((HEADS_W, m), BF16),
        jax.ShapeDtypeStruct((N_HEADS * ROPE_DIM, m), BF16),
        jax.ShapeDtypeStruct((nb, tm, HEADS_W), BF16),
        jax.ShapeDtypeStruct((nb, tm, LANES), BF16),
        jax.ShapeDtypeStruct((nb, HEADS_W, tm), BF16),
    )
    out_specs = (row(KV_LORA), row(LANES), row(HEADS_W), row(D_MODEL), colT(HEADS_W),
                 colT(N_HEADS * ROPE_DIM), blk3(tm, HEADS_W), blk3(tm, LANES), blk3(HEADS_W, tm))
    in_specs = [row(D_MODEL), _full_spec(g.shape), _full_spec(w.shape), _full_spec(gq.shape),
                _full_spec(gkv.shape), _full_spec(wuq.shape), _full_spec(wukn.shape), _full_spec(wuvt.shape),
                colT(LANES), colT(LANES), row(LANES), row(LANES)]
    return pl.pallas_call(
        _proj_mla_kernel, grid=(nb,), in_specs=in_specs, out_specs=out_specs, out_shape=out_shape,
        compiler_params=_params(1), name="proj_mla",
    )(x, g, w, gq, gkv, wuq, wukn, wuvt, cosT, sinT, ctok, stok)


FOX_IN_COLS = 2 * HEADS_W + LANES + HEADS_W + D_MODEL


def _proj_fox_kernel(x_ref, g_ref, w_ref, wfqt_ref, wfvt_ref, bf_ref, tri_ref, pk_ref, pqt_ref,
                     fk_ref, fv_ref, logf_ref, zb_ref, gb_ref, fqT_ref, fkb_ref, fvT_ref,
                     augk_ref, augqT_ref, base_ref, carry_ref):
    i = pl.program_id(0)
    tm = x_ref.shape[0]
    hb = _rms(x_ref[...], g_ref[...]).astype(BF16)
    o = 0
    fk = _dot(hb, w_ref[:, o:o + HEADS_W]); o += HEADS_W
    fk_ref[...] = fk
    fkb_ref[0] = fk.astype(BF16)
    fv_ref[...] = _dot(hb, w_ref[:, o:o + HEADS_W]); o += HEADS_W
    zf = _dot(hb, w_ref[:, o:o + LANES]) + bf_ref[...]; o += LANES
    zb_ref[...] = _dot(hb, w_ref[:, o:o + HEADS_W]); o += HEADS_W
    gb_ref[...] = _dot(hb, w_ref[:, o:o + D_MODEL])
    fqT_ref[...] = (_dot_nt(wfqt_ref[...], hb) * FOX_QSCALE).astype(BF16)
    fvT_ref[0] = _dot_nt(wfvt_ref[...], hb).astype(BF16)

    lane = lax.broadcasted_iota(jnp.int32, (tm, LANES), 1)
    logf = jnp.minimum(zf, 0.0) - jnp.log(1.0 + jnp.exp(-jnp.abs(zf)))
    logf = jnp.where(lane < N_HEADS, logf, 0.0)
    logf_ref[...] = logf[:, 0:N_HEADS]

    hi, mid, lo = _split3(logf)
    tri = tri_ref[...]
    r = _dot(tri, hi) + _dot(tri, mid) + _dot(tri, lo)

    @pl.when(i == 0)
    def _():
        carry_ref[...] = jnp.zeros_like(carry_ref)

    base_ref[0] = carry_ref[...] * LOG2E
    carry_ref[...] = carry_ref[...] + r[tm - 1:tm, :]

    rh, rm, rl = _split3(r * LOG2E)
    rcat = jnp.concatenate([rh, rm, rl], axis=1)
    slot = lane % AUG_PER_HEAD
    ones_k = jnp.where((lane < N_HEADS * AUG_PER_HEAD) & (slot < 3), 1.0, 0.0)
    augk_ref[0] = (_dot(rcat, pk_ref[...]) + ones_k).astype(BF16)
    rowq = lax.broadcasted_iota(jnp.int32, (N_HEADS * AUG_PER_HEAD, tm), 0) % AUG_PER_HEAD
    ones_q = jnp.where((rowq >= 3) & (rowq < 6), 1.0, 0.0)
    augqT_ref[...] = (_dot_nt(pqt_ref[...], rcat) + ones_q).astype(BF16)


def _proj_fox(x, g, w, wfqt, wfvt, bf, tri, pk, pqt, tm):
    m = x.shape[0]
    nb = m // tm
    row = lambda c: pl.BlockSpec((tm, c), lambda i: (i, 0))
    colT = lambda r: pl.BlockSpec((r, tm), lambda i: (0, i))
    blk3 = lambda a, b: pl.BlockSpec((1, a, b), lambda i: (i, 0, 0))
    out_shape = (
        jax.ShapeDtypeStruct((m, HEADS_W), F32),
        jax.ShapeDtypeStruct((m, HEADS_W), F32),
        jax.ShapeDtypeStruct((m, N_HEADS), F32),
        jax.ShapeDtypeStruct((m, HEADS_W), F32),
        jax.ShapeDtypeStruct((m, D_MODEL), F32),
        jax.ShapeDtypeStruct((HEADS_W, m), BF16),
        jax.ShapeDtypeStruct((nb, tm, HEADS_W), BF16),
        jax.ShapeDtypeStruct((nb, HEADS_W, tm), BF16),
        jax.ShapeDtypeStruct((nb, tm, LANES), BF16),
        jax.ShapeDtypeStruct((N_HEADS * AUG_PER_HEAD, m), BF16),
        jax.ShapeDtypeStruct((nb, 1, LANES), F32),
    )
    out_specs = (row(HEADS_W), row(HEADS_W), row(N_HEADS), row(HEADS_W), row(D_MODEL), colT(HEADS_W),
                 blk3(tm, HEADS_W), blk3(HEADS_W, tm), blk3(tm, LANES), colT(N_HEADS * AUG_PER_HEAD),
                 blk3(1, LANES))
    in_specs = [row(D_MODEL)] + [_full_spec(a.shape) for a in (g, w, wfqt, wfvt, bf, tri, pk, pqt)]
    return pl.pallas_call(
        _proj_fox_kernel, grid=(nb,), in_specs=in_specs, out_specs=out_specs, out_shape=out_shape,
        scratch_shapes=[pltpu.VMEM((1, LANES), F32)],
        compiler_params=_params(1), name="proj_fox",
    )(x, g, w, wfqt, wfvt, bf, tri, pk, pqt)


def _attn_kernel(base_ref, qp_ref, qe_ref, ka_ref, ke_ref, vT_ref, o_ref, acc_ref, s_ref, *, fox):
    p = pl.program_id(0)
    i = pl.program_id(1)
    tq = qp_ref.shape[1]
    tk = ka_ref.shape[1]
    rows_p = lax.broadcasted_iota(jnp.int32, (LANES, tq), 0)
    rows_e = lax.broadcasted_iota(jnp.int32, qe_ref.shape, 0)
    kpos = lax.broadcasted_iota(jnp.int32, (tk, tq), 0)
    qpos = lax.broadcasted_iota(jnp.int32, (tk, tq), 1)
    if fox:
        visible = kpos <= qpos
    else:
        visible = (kpos // CHUNK) <= (qpos // CHUNK)
    qp = qp_ref[...]
    qe = qe_ref[...]

    ws = []
    for hh in range(2):
        keep_p = (rows_p >= hh * HEAD_DV) & (rows_p < (hh + 1) * HEAD_DV)
        if fox:
            h = 2 * p + hh
            keep_e = (rows_e >= h * AUG_PER_HEAD) & (rows_e < (h + 1) * AUG_PER_HEAD)
            extra = jnp.where(keep_e, qe, jnp.zeros_like(qe))
        else:
            extra = qe[hh * ROPE_DIM:(hh + 1) * ROPE_DIM]
        zero_rows = jnp.zeros((LANES - extra.shape[0], tq), BF16)
        ws.append(jnp.concatenate([jnp.where(keep_p, qp, jnp.zeros_like(qp)), extra, zero_rows], axis=0))

    steps = [(u, hh) for u in range(tk // KEY_SUB) for hh in range(2)]
    ones_rows = jnp.ones((DEN_ROWS, KEY_SUB), BF16)

    def scores(j, step):
        u, hh = step
        rows = slice(u * KEY_SUB, (u + 1) * KEY_SUB)
        lhs = jnp.concatenate([ka_ref[j, rows, :], ke_ref[j, rows, :]], axis=1)
        return _dot(lhs, ws[hh])

    def block(j, stats, masked, j_next):
        stats = list(stats)
        n_steps = len(steps)
        tiles = {}
        for n, (u, hh) in enumerate(steps):
            ahead = n + LOOKAHEAD
            if ahead < n_steps:
                tiles[ahead] = scores(j, steps[ahead])
            elif j_next is not None:
                tiles[ahead] = scores(j_next, steps[ahead - n_steps])
            sT = tiles.pop(n) if n in tiles else s_ref[n]
            rows = slice(u * KEY_SUB, (u + 1) * KEY_SUB)
            m_old = stats[hh]
            if masked:
                sT = jnp.where(visible[rows], sT, NEG)
            if fox:
                d = base_ref[2 * p + hh, i] - base_ref[2 * p + hh, j]
            else:
                d = 0.0
            m_new = jnp.maximum(m_old, jnp.max(sT, axis=0, keepdims=True) + d)
            alpha = jnp.exp2(m_old - m_new)
            pT = jnp.exp2((sT - (m_new - d)).astype(BF16))
            stats[hh] = m_new
            v = jnp.concatenate([vT_ref[j, hh * HEAD_DV:(hh + 1) * HEAD_DV, rows], ones_rows], axis=0)
            acc_ref[hh] = alpha * acc_ref[hh] + _dot(v, pT)
        for n, tile in tiles.items():
            s_ref[n - n_steps] = tile
        return tuple(stats)

    acc_ref[...] = jnp.zeros_like(acc_ref)
    for n in range(LOOKAHEAD):
        s_ref[n] = scores(0, steps[n])
    m0 = jnp.full((1, tq), NEG, F32)
    stats = lax.fori_loop(0, i, lambda j, c: block(j, c, False, j + 1), (m0, m0))
    block(i, stats, True, None)
    outs = [acc_ref[hh, 0:HEAD_DV, :] / acc_ref[hh, HEAD_DV:HEAD_DV + 1, :] for hh in range(2)]
    o_ref[...] = jnp.concatenate(outs, axis=0).T


def _attention(base, qpT, qeT, ka, ke, vT, fox):
    nb, tk, _ = ka.shape
    s = qpT.shape[1]
    tq = tk
    e_rows = qeT.shape[0] if fox else 2 * ROPE_DIM
    e_map = (lambda p, i, b: (0, i)) if fox else (lambda p, i, b: (p, i))
    grid_spec = pltpu.PrefetchScalarGridSpec(
        num_scalar_prefetch=1,
        grid=(N_HEADS // 2, nb),
        in_specs=[
            pl.BlockSpec((LANES, tq), lambda p, i, b: (p, i)),
            pl.BlockSpec((e_rows, tq), e_map),
            pl.BlockSpec((nb, tk, LANES), lambda p, i, b: (0, 0, p)),
            pl.BlockSpec((nb, tk, LANES), lambda p, i, b: (0, 0, 0)),
            pl.BlockSpec((nb, LANES, tk), lambda p, i, b: (0, p, 0)),
        ],
        out_specs=pl.BlockSpec((tq, LANES), lambda p, i, b: (i, p)),
        scratch_shapes=[pltpu.VMEM((2, HEAD_DV + DEN_ROWS, tq), F32), pltpu.VMEM((LOOKAHEAD, KEY_SUB, tq), F32)],
    )
    return pl.pallas_call(
        functools.partial(_attn_kernel, fox=fox), grid_spec=grid_spec,
        out_shape=jax.ShapeDtypeStruct((s, HEADS_W), F32),
        compiler_params=_params(2), name="attn_fox" if fox else "attn_mla",
    )(base, qpT, qeT, ka, ke, vT)


def _softmax_pv(s, v):
    m = jnp.max(s, axis=-1, keepdims=True)
    p = jnp.exp2(s - m)
    l = jnp.sum(p, axis=-1, keepdims=True)
    return _dot(p.astype(BF16), v) / l


def _diag_heads(o_big, t_new):
    rows = lax.broadcasted_iota(jnp.int32, o_big.shape, 0) // t_new
    cols = lax.broadcasted_iota(jnp.int32, o_big.shape, 1) // HEAD_DV
    kept = jnp.where(rows == cols, o_big, 0.0)
    return jnp.sum(kept.reshape(N_HEADS, t_new, o_big.shape[1]), axis=0)


def _sample_attn_kernel(qn_ref, qr_ref, qf_ref, lrep_ref, cckv_ref, nckv_ref, ckr_ref, nkr_ref,
                        cfk_ref, nfk_ref, cfv_ref, nfv_ref, wukn_ref, wuv_ref, oa_ref, ob_ref):
    past = cckv_ref.shape[1]
    t_new = nckv_ref.shape[0]
    rows = N_HEADS * t_new
    pad = lrep_ref.shape[2] - past - t_new
    keys = past + t_new + pad

    def with_new(cache, new):
        return jnp.concatenate([cache, new, jnp.zeros((pad, new.shape[1]), new.dtype)], axis=0)

    kidx = lax.broadcasted_iota(jnp.int32, (rows, keys), 1)
    qpos = past + lax.broadcasted_iota(jnp.int32, (rows, keys), 0) % t_new
    real = kidx < past + t_new

    ckv_all = with_new(cckv_ref[0], nckv_ref[...]).astype(BF16)
    kn = _dot(ckv_all, wukn_ref[...]).astype(BF16)
    va = _dot(ckv_all, wuv_ref[...]).astype(BF16)
    kr = with_new(ckr_ref[0], nkr_ref[:, 0:ROPE_DIM]).astype(BF16)
    s = _dot_nt(qn_ref[0], kn) + _dot_nt(qr_ref[0], kr)
    s = jnp.where(real & ((kidx // CHUNK) <= (qpos // CHUNK)), s, NEG)
    oa_ref[...] = _diag_heads(_softmax_pv(s, va), t_new)

    fk = with_new(cfk_ref[0], nfk_ref[...]).astype(BF16)
    fv = with_new(cfv_ref[0], nfv_ref[...]).astype(BF16)
    c = lrep_ref[0]
    shift = 1
    while shift < keys:
        c = c + jnp.where(kidx >= shift, pltpu.roll(c, shift, 1), 0.0)
        shift *= 2
    cq = jnp.sum(jnp.where(kidx == qpos, c, 0.0), axis=-1, keepdims=True)
    s = _dot_nt(qf_ref[0], fk) + (cq - c) * LOG2E
    s = jnp.where(real & (kidx <= qpos), s, NEG)
    ob_ref[...] = _diag_heads(_softmax_pv(s, fv), t_new)


def _sample_attention(qn_bd, qr, qf_bd, lrep, cckv, nckv, ckr, nkr, cfk, nfk, cfv, nfv, wukn, wuv):
    nbatch, past, _ = cckv.shape
    t_new = nckv.shape[0] // nbatch
    b3 = lambda a: pl.BlockSpec((1,) + a.shape[1:], lambda b: (b, 0, 0))
    new = lambda a: pl.BlockSpec((t_new, a.shape[1]), lambda b: (b, 0))
    in_specs = [b3(qn_bd), b3(qr), b3(qf_bd), b3(lrep), b3(cckv), new(nckv), b3(ckr), new(nkr),
                b3(cfk), new(nfk), b3(cfv), new(nfv), _full_spec(wukn.shape), _full_spec(wuv.shape)]
    out = jax.ShapeDtypeStruct((nbatch * t_new, HEADS_W), F32)
    o_spec = pl.BlockSpec((t_new, HEADS_W), lambda b: (b, 0))
    return pl.pallas_call(
        _sample_attn_kernel, grid=(nbatch,), in_specs=in_specs, out_specs=(o_spec, o_spec),
        out_shape=(out, out), compiler_params=_params(1), name="attn_sample",
    )(qn_bd, qr, qf_bd, lrep, cckv, nckv, ckr, nkr, cfk, nfk, cfv, nfv, wukn, wuv)


def _merge_kernel(x_ref, oa_ref, za_ref, ob_ref, zb_ref, ga_ref, gb_ref, woa_ref, wob_ref, wout_ref,
                  fg_ref, xn_ref, *maybe_y_ref):
    za = za_ref[...]
    zb = zb_ref[...]
    a = _dot((oa_ref[...] * (za * _sigmoid(za))).astype(BF16), woa_ref[...])
    b = _dot((ob_ref[...] * (zb * _sigmoid(zb))).astype(BF16), wob_ref[...])
    mix = _sigmoid(ga_ref[...]) * a + _sigmoid(gb_ref[...]) * b
    xn = x_ref[...] + _dot(mix.astype(BF16), wout_ref[...])
    xn_ref[...] = xn
    if maybe_y_ref:
        maybe_y_ref[0][...] = _rms(xn, fg_ref[...])


def _merge(x, oa, za, ob, zb, ga, gb, woa, wob, wout, fg, tm, final):
    m = x.shape[0]
    row = lambda c: pl.BlockSpec((tm, c), lambda i: (i, 0))
    in_specs = [row(D_MODEL), row(HEADS_W), row(HEADS_W), row(HEADS_W), row(HEADS_W), row(D_MODEL),
                row(D_MODEL), _full_spec(woa.shape), _full_spec(wob.shape), _full_spec(wout.shape),
                _full_spec(fg.shape)]
    n_out = 2 if final else 1
    out_shape = tuple(jax.ShapeDtypeStruct((m, D_MODEL), F32) for _ in range(n_out))
    out_specs = tuple(row(D_MODEL) for _ in range(n_out))
    return pl.pallas_call(
        _merge_kernel, grid=(m // tm,), in_specs=in_specs, out_specs=out_specs, out_shape=out_shape,
        compiler_params=_params(1), name="merge",
    )(x, oa, za, ob, zb, ga, gb, woa, wob, wout, fg)


def _pad_cols(w, width):
    return jnp.pad(w, ((0, 0), (0, width - w.shape[1])))


def _layer_weights(w_in, w_uq, w_ukv, w_oa, w_ob, w_out, b_f):
    offs = np.cumsum(IN_SIZES)[:-1].tolist()
    w_cq, w_ckv, w_kr, w_za, w_fq, w_fk, w_fv, w_zf, w_zb, w_ga, w_gb = jnp.split(w_in, offs, axis=1)
    w_kr_swapped = jnp.concatenate([w_kr[:, HALF_ROPE:], w_kr[:, :HALF_ROPE]], axis=1)
    w_mla = jnp.concatenate([w_cq, w_ckv, _pad_cols(w_kr, LANES), _pad_cols(w_kr_swapped, LANES),
                             w_za, w_ga], axis=1).astype(BF16)
    w_fox = jnp.concatenate([w_fk, w_fv, _pad_cols(w_zf, LANES), w_zb, w_gb], axis=1).astype(BF16)
    uq = w_uq.reshape(Q_LORA, N_HEADS, NOPE_DIM + ROPE_DIM)
    uq_rows = jnp.concatenate([
        uq[:, :, :NOPE_DIM].reshape(Q_LORA, -1),
        uq[:, :, NOPE_DIM:NOPE_DIM + HALF_ROPE].reshape(Q_LORA, -1),
        uq[:, :, NOPE_DIM + HALF_ROPE:].reshape(Q_LORA, -1)], axis=1)
    ukv = w_ukv.reshape(KV_LORA, N_HEADS, NOPE_DIM + HEAD_DV)
    w_ukn = ukv[:, :, :NOPE_DIM].reshape(KV_LORA, -1).astype(BF16)
    w_uv = ukv[:, :, NOPE_DIM:].reshape(KV_LORA, -1).astype(BF16)
    return dict(
        w_mla=w_mla, w_fox=w_fox, w_uqT=uq_rows.T.astype(BF16), w_ukn=w_ukn, w_uv=w_uv, w_uvT=w_uv.T,
        w_fqT=w_fq.T.astype(BF16), w_fvT=w_fv.T.astype(BF16),
        b_f=_pad_cols(b_f[None, :], LANES),
        w_oa=w_oa.astype(BF16), w_ob=w_ob.astype(BF16), w_out=w_out.astype(BF16))


def _rope_tables(pos):
    inv = jnp.exp(-math.log(ROPE_THETA) * jnp.arange(HALF_ROPE, dtype=F32) / HALF_ROPE)
    ang = pos.astype(F32)[:, None] * inv[None, :]
    cos = jnp.cos(ang)
    sin = jnp.sin(ang)
    cosT = jnp.tile(cos.T, (N_HEADS, 1))
    sinT = jnp.tile(sin.T, (N_HEADS, 1))
    ctok = _pad_cols(jnp.concatenate([cos, cos], axis=1), LANES)
    stok = _pad_cols(jnp.concatenate([-sin, sin], axis=1), LANES)
    return cosT, sinT, ctok, stok


def _bias_placement():
    pk = np.zeros((3 * LANES, LANES), np.float32)
    pqt = np.zeros((N_HEADS * AUG_PER_HEAD, 3 * LANES), np.float32)
    for c in range(3):
        for h in range(N_HEADS):
            pk[c * LANES + h, h * AUG_PER_HEAD + 3 + c] = -1.0
            pqt[h * AUG_PER_HEAD + c, c * LANES + h] = 1.0
    return jnp.asarray(pk, BF16), jnp.asarray(pqt, BF16)


def _tri(n):
    return jnp.asarray(np.tril(np.ones((n, n), np.float32)), BF16)


def _block_diag_queries(qT, nbatch, t_new):
    q = qT.T.reshape(nbatch, t_new, N_HEADS, HEAD_DV)
    eye = jnp.eye(N_HEADS, dtype=q.dtype)
    return jnp.einsum('bthj,hg->bhtgj', q, eye).reshape(nbatch, N_HEADS * t_new, HEADS_W)


def _project(x, lw, norm_g, g_q, g_kv, tables, tri, pk, pqt, tm):
    cosT, sinT, ctok, stok = tables
    g = norm_g[None, :]
    mla = _proj_mla(x, g, lw['w_mla'], g_q[None, :], g_kv[None, :], lw['w_uqT'], lw['w_ukn'], lw['w_uvT'],
                    cosT, sinT, ctok, stok, tm)
    fox = _proj_fox(x, g, lw['w_fox'], lw['w_fqT'], lw['w_fvT'], lw['b_f'], tri, pk, pqt, tm)
    return mla, fox


def kernel(x_prompt, x_sample, cache_mla_ckv, cache_mla_krope, cache_fox_k, cache_fox_v, cache_fox_logf,
           norm_g, w_in, g_q, w_uq, g_kv, w_ukv, b_f, w_oa, w_ob, w_out, final_g):
    depth = w_in.shape[0]
    _, seq, _ = x_prompt.shape
    nbatch, t_new, _ = x_sample.shape
    past = cache_mla_ckv.shape[2]
    m_s = nbatch * t_new
    key_pad = -(past + t_new) % LANES

    tabs_p = _rope_tables(jnp.arange(seq, dtype=jnp.int32))
    tabs_s = _rope_tables(past + jnp.arange(m_s, dtype=jnp.int32) % t_new)
    pk, pqt = _bias_placement()
    tri_p = _tri(SEQ_BLOCK)
    tri_s = _tri(m_s)
    fg = final_g[None, :]

    xp = x_prompt.reshape(seq, D_MODEL)
    xs = x_sample.reshape(m_s, D_MODEL)
    outs = {k: [] for k in ('p_ckv', 'p_kr', 'p_fk', 'p_fv', 'p_lf', 's_ckv', 's_kr', 's_fk', 's_fv', 's_lf')}
    yp = ys = None
    for l in range(depth):
        final = l == depth - 1
        lw = _layer_weights(w_in[l], w_uq[l], w_ukv[l], w_oa[l], w_ob[l], w_out[l], b_f[l])

        mla, fox = _project(xp, lw, norm_g[l], g_q[l], g_kv[l], tabs_p, tri_p, pk, pqt, SEQ_BLOCK)
        ckv, kr, za, ga, qnT, qrT, knb, krb, vT = mla
        fk, fv, logf, zb, gb, fqT, fkb, fvT, augk, augqT, base = fox
        no_base = jnp.zeros((N_HEADS, base.shape[0]), F32)
        o_a = _attention(no_base, qnT, qrT, knb, krb, vT, fox=False)
        o_b = _attention(base[:, 0, :N_HEADS].T, fqT, augqT, fkb, augk, fvT, fox=True)
        res = _merge(xp, o_a, za, o_b, zb, ga, gb, lw['w_oa'], lw['w_ob'], lw['w_out'], fg, SEQ_BLOCK, final)
        xp = res[0]
        if final:
            yp = res[1]
        outs['p_ckv'].append(ckv); outs['p_kr'].append(kr[:, :ROPE_DIM]); outs['p_fk'].append(fk)
        outs['p_fv'].append(fv); outs['p_lf'].append(logf)

        mla, fox = _project(xs, lw, norm_g[l], g_q[l], g_kv[l], tabs_s, tri_s, pk, pqt, m_s)
        ckv, kr, za, ga, qnT, qrT, _, _, _ = mla
        fk, fv, logf, zb, gb, fqT, _, _, _, _, _ = fox
        qn_bd = _block_diag_queries(qnT, nbatch, t_new)
        qf_bd = _block_diag_queries(fqT, nbatch, t_new)
        qr = qrT.reshape(N_HEADS, ROPE_DIM, nbatch, t_new).transpose(2, 0, 3, 1).reshape(
            nbatch, N_HEADS * t_new, ROPE_DIM)
        lf_all = jnp.concatenate([cache_fox_logf[l], logf.reshape(nbatch, t_new, N_HEADS),
                                  jnp.zeros((nbatch, key_pad, N_HEADS), F32)], axis=1)
        lrep = jnp.repeat(lf_all.transpose(0, 2, 1), t_new, axis=1)
        o_a, o_b = _sample_attention(
            qn_bd, qr, qf_bd, lrep, cache_mla_ckv[l], ckv, cache_mla_krope[l], kr,
            cache_fox_k[l].reshape(nbatch, past, HEADS_W), fk,
            cache_fox_v[l].reshape(nbatch, past, HEADS_W), fv, lw['w_ukn'], lw['w_uv'])
        res = _merge(xs, o_a, za, o_b, zb, ga, gb, lw['w_oa'], lw['w_ob'], lw['w_out'], fg, m_s, final)
        xs = res[0]
        if final:
            ys = res[1]
        outs['s_ckv'].append(ckv); outs['s_kr'].append(kr[:, :ROPE_DIM]); outs['s_fk'].append(fk)
        outs['s_fv'].append(fv); outs['s_lf'].append(logf)

    st = lambda name, shape: jnp.stack(outs[name]).reshape((depth,) + shape)
    return (yp.reshape(1, seq, D_MODEL), ys.reshape(nbatch, t_new, D_MODEL),
            st('p_ckv', (1, seq, KV_LORA)), st('p_kr', (1, seq, ROPE_DIM)),
            st('p_fk', (1, seq, N_HEADS, HEAD_DV)), st('p_fv', (1, seq, N_HEADS, HEAD_DV)),
            st('p_lf', (1, seq, N_HEADS)),
            st('s_ckv', (nbatch, t_new, KV_LORA)), st('s_kr', (nbatch, t_new, ROPE_DIM)),
            st('s_fk', (nbatch, t_new, N_HEADS, HEAD_DV)), st('s_fv', (nbatch, t_new, N_HEADS, HEAD_DV)),
            st('s_lf', (nbatch, t_new, N_HEADS)))
```

```python
import functools
import math

import numpy as np
import jax
import jax.numpy as jnp
from jax import lax
from jax.experimental import pallas as pl
from jax.experimental.pallas import tpu as pltpu

D_MODEL = 1024
N_HEADS = 8
NOPE_DIM = 64
ROPE_DIM = 32
HALF_ROPE = ROPE_DIM // 2
HEAD_DV = 64
Q_LORA = 384
KV_LORA = 256
HEADS_W = N_HEADS * HEAD_DV
CHUNK = 64
ROPE_THETA = 10000.0
LOG2E = math.log2(math.e)
MLA_QSCALE = LOG2E / math.sqrt(NOPE_DIM + ROPE_DIM)
FOX_QSCALE = LOG2E / math.sqrt(HEAD_DV)
EPS = 1e-6
NEG = -1e30
IN_SIZES = (Q_LORA, KV_LORA, ROPE_DIM, HEADS_W, HEADS_W, HEADS_W, HEADS_W, N_HEADS, HEADS_W, D_MODEL, D_MODEL)

LANES = 128
SEQ_BLOCK = 512
KEY_SUB = 256
DEN_ROWS = 16
HEADS_PER_STEP = 4
LOOKAHEAD = 3
SKIP_LOG2 = 64.0
NORM_SLACK = 1.01
AUG_PER_HEAD = 8
VMEM_LIMIT_BYTES = 56 * 1024 * 1024

F32 = jnp.float32
BF16 = jnp.bfloat16


def _dot(a, b):
    return jnp.dot(a, b, preferred_element_type=F32)


def _dot_nt(a, b):
    return lax.dot_general(a, b, (((1,), (1,)), ((), ())), preferred_element_type=F32)


def _rms(x, g):
    return x * lax.rsqrt(jnp.mean(x * x, axis=-1, keepdims=True) + EPS) * g


def _sigmoid(x):
    return 1.0 / (1.0 + jnp.exp(-x))


def _split3(x):
    hi = x.astype(BF16)
    r1 = x - hi.astype(F32)
    mid = r1.astype(BF16)
    lo = (r1 - mid.astype(F32)).astype(BF16)
    return hi, mid, lo


def _full_spec(shape):
    nd = len(shape)
    return pl.BlockSpec(shape, lambda *_: (0,) * nd)


def _params(n_axes):
    return pltpu.CompilerParams(dimension_semantics=("arbitrary",) * n_axes,
                                vmem_limit_bytes=VMEM_LIMIT_BYTES)


MLA_IN_COLS = Q_LORA + KV_LORA + 2 * LANES + HEADS_W + D_MODEL


def _proj_mla_kernel(x_ref, g_ref, w_ref, gq_ref, gkv_ref, wuq_ref, wukn_ref, wuvt_ref,
                     cosT_ref, sinT_ref, ctok_ref, stok_ref,
                     ckv_ref, kr_ref, za_ref, ga_ref, qnT_ref, qrT_ref, knb_ref, krb_ref, vT_ref):
    hb = _rms(x_ref[...], g_ref[...]).astype(BF16)
    o = 0
    zcq = _dot(hb, w_ref[:, o:o + Q_LORA]); o += Q_LORA
    zckv = _dot(hb, w_ref[:, o:o + KV_LORA]); o += KV_LORA
    zka = _dot(hb, w_ref[:, o:o + LANES]); o += LANES
    zkb = _dot(hb, w_ref[:, o:o + LANES]); o += LANES
    za_ref[...] = _dot(hb, w_ref[:, o:o + HEADS_W]); o += HEADS_W
    ga_ref[...] = _dot(hb, w_ref[:, o:o + D_MODEL])

    cqb = _rms(zcq, gq_ref[...]).astype(BF16)
    qT = _dot_nt(wuq_ref[...], cqb)
    qnT_ref[...] = (qT[0:HEADS_W] * MLA_QSCALE).astype(BF16)
    x1 = qT[HEADS_W:HEADS_W + LANES]
    x2 = qT[HEADS_W + LANES:HEADS_W + 2 * LANES]
    c = cosT_ref[...]
    s = sinT_ref[...]
    n1 = ((x1 * c - x2 * s) * MLA_QSCALE).astype(BF16)
    n2 = ((x1 * s + x2 * c) * MLA_QSCALE).astype(BF16)
    for h in range(N_HEADS):
        qrT_ref[h * ROPE_DIM:h * ROPE_DIM + HALF_ROPE, :] = n1[h * HALF_ROPE:(h + 1) * HALF_ROPE]
        qrT_ref[h * ROPE_DIM + HALF_ROPE:(h + 1) * ROPE_DIM, :] = n2[h * HALF_ROPE:(h + 1) * HALF_ROPE]

    ckv = _rms(zckv, gkv_ref[...])
    ckv_ref[...] = ckv
    cb = ckv.astype(BF16)
    knb_ref[0] = _dot(cb, wukn_ref[...]).astype(BF16)
    vT_ref[0] = _dot_nt(wuvt_ref[...], cb).astype(BF16)

    kr = zka * ctok_ref[...] + zkb * stok_ref[...]
    kr_ref[...] = kr
    krb_ref[0] = kr.astype(BF16)


def _proj_mla(x, g, w, gq, gkv, wuq, wukn, wuvt, cosT, sinT, ctok, stok, tm):
    m = x.shape[0]
    nb = m // tm
    row = lambda c: pl.BlockSpec((tm, c), lambda i: (i, 0))
    colT = lambda r: pl.BlockSpec((r, tm), lambda i: (0, i))
    blk3 = lambda a, b: pl.BlockSpec((1, a, b), lambda i: (i, 0, 0))
    out_shape = (
        jax.ShapeDtypeStruct((m, KV_LORA), F32),
        jax.ShapeDtypeStruct((m, LANES), F32),
        jax.ShapeDtypeStruct((m, HEADS_W), F32),
        jax.ShapeDtypeStruct((m, D_MODEL), F32),
        jax.ShapeDtypeStruct((HEADS_W, m), BF16),
        jax.ShapeDtypeStruct((N_HEADS * ROPE_DIM, m), BF16),
        jax.ShapeDtypeStruct((nb, tm, HEADS_W), BF16),
        jax.ShapeDtypeStruct((nb, tm, LANES), BF16),
        jax.ShapeDtypeStruct((nb, HEADS_W, tm), BF16),
    )
    out_specs = (row(KV_LORA), row(LANES), row(HEADS_W), row(D_MODEL), colT(HEADS_W),
                 colT(N_HEADS * ROPE_DIM), blk3(tm, HEADS_W), blk3(tm, LANES), blk3(HEADS_W, tm))
    in_specs = [row(D_MODEL), _full_spec(g.shape), _full_spec(w.shape), _full_spec(gq.shape),
                _full_spec(gkv.shape), _full_spec(wuq.shape), _full_spec(wukn.shape), _full_spec(wuvt.shape),
                colT(LANES), colT(LANES), row(LANES), row(LANES)]
    return pl.pallas_call(
        _proj_mla_kernel, grid=(nb,), in_specs=in_specs, out_specs=out_specs, out_shape=out_shape,
        compiler_params=_params(1), name="proj_mla",
    )(x, g, w, gq, gkv, wuq, wukn, wuvt, cosT, sinT, ctok, stok)


FOX_IN_COLS = 2 * HEADS_W + LANES + HEADS_W + D_MODEL


def _proj_fox_kernel(x_ref, g_ref, w_ref, wfqt_ref, wfvt_ref, bf_ref, tri_ref, pk_ref, pqt_ref, grp_ref,
                     fk_ref, fv_ref, logf_ref, zb_ref, gb_ref, fqT_ref, fkb_ref, fvT_ref,
                     augk_ref, augqT_ref, base_ref, qn2_ref, kn2_ref, carry_ref):
    i = pl.program_id(0)
    tm = x_ref.shape[0]
    hb = _rms(x_ref[...], g_ref[...]).astype(BF16)
    o = 0
    fk = _dot(hb, w_ref[:, o:o + HEADS_W]); o += HEADS_W
    fk_ref[...] = fk
    fkb_ref[0] = fk.astype(BF16)
    kn2_ref[0] = jnp.max(_dot((fk * fk).astype(BF16), grp_ref[...]), axis=0, keepdims=True)
    fv_ref[...] = _dot(hb, w_ref[:, o:o + HEADS_W]); o += HEADS_W
    zf = _dot(hb, w_ref[:, o:o + LANES]) + bf_ref[...]; o += LANES
    zb_ref[...] = _dot(hb, w_ref[:, o:o + HEADS_W]); o += HEADS_W
    gb_ref[...] = _dot(hb, w_ref[:, o:o + D_MODEL])
    fqT = _dot_nt(wfqt_ref[...], hb) * FOX_QSCALE
    fqT_ref[...] = fqT.astype(BF16)
    qn2 = jnp.sum((fqT * fqT).reshape(N_HEADS, HEAD_DV, tm), axis=1)
    qn2_ref[0] = jnp.broadcast_to(jnp.max(qn2, axis=1, keepdims=True), (N_HEADS, LANES))
    fvT_ref[0] = _dot_nt(wfvt_ref[...], hb).astype(BF16)

    lane = lax.broadcasted_iota(jnp.int32, (tm, LANES), 1)
    logf = jnp.minimum(zf, 0.0) - jnp.log(1.0 + jnp.exp(-jnp.abs(zf)))
    logf = jnp.where(lane < N_HEADS, logf, 0.0)
    logf_ref[...] = logf[:, 0:N_HEADS]

    hi, mid, lo = _split3(logf)
    tri = tri_ref[...]
    r = _dot(tri, hi) + _dot(tri, mid) + _dot(tri, lo)

    @pl.when(i == 0)
    def _():
        carry_ref[...] = jnp.zeros_like(carry_ref)

    base_ref[0] = carry_ref[...] * LOG2E
    carry_ref[...] = carry_ref[...] + r[tm - 1:tm, :]

    rh, rm, rl = _split3(r * LOG2E)
    rcat = jnp.concatenate([rh, rm, rl], axis=1)
    slot = lane % AUG_PER_HEAD
    ones_k = jnp.where((lane < N_HEADS * AUG_PER_HEAD) & (slot < 3), 1.0, 0.0)
    augk_ref[0] = (_dot(rcat, pk_ref[...]) + ones_k).astype(BF16)
    rowq = lax.broadcasted_iota(jnp.int32, (N_HEADS * AUG_PER_HEAD, tm), 0) % AUG_PER_HEAD
    ones_q = jnp.where((rowq >= 3) & (rowq < 6), 1.0, 0.0)
    augqT_ref[...] = (_dot_nt(pqt_ref[...], rcat) + ones_q).astype(BF16)


def _proj_fox(x, g, w, wfqt, wfvt, bf, tri, pk, pqt, grp, tm):
    m = x.shape[0]
    nb = m // tm
    row = lambda c: pl.BlockSpec((tm, c), lambda i: (i, 0))
    colT = lambda r: pl.BlockSpec((r, tm), lambda i: (0, i))
    blk3 = lambda a, b: pl.BlockSpec((1, a, b), lambda i: (i, 0, 0))
    out_shape = (
        jax.ShapeDtypeStruct((m, HEADS_W), F32),
        jax.ShapeDtypeStruct((m, HEADS_W), F32),
        jax.ShapeDtypeStruct((m, N_HEADS), F32),
        jax.ShapeDtypeStruct((m, HEADS_W), F32),
        jax.ShapeDtypeStruct((m, D_MODEL), F32),
        jax.ShapeDtypeStruct((HEADS_W, m), BF16),
        jax.ShapeDtypeStruct((nb, tm, HEADS_W), BF16),
        jax.ShapeDtypeStruct((nb, HEADS_W, tm), BF16),
        jax.ShapeDtypeStruct((nb, tm, LANES), BF16),
        jax.ShapeDtypeStruct((N_HEADS * AUG_PER_HEAD, m), BF16),
        jax.ShapeDtypeStruct((nb, 1, LANES), F32),
        jax.ShapeDtypeStruct((nb, N_HEADS, LANES), F32),
        jax.ShapeDtypeStruct((nb, 1, LANES), F32),
    )
    out_specs = (row(HEADS_W), row(HEADS_W), row(N_HEADS), row(HEADS_W), row(D_MODEL), colT(HEADS_W),
                 blk3(tm, HEADS_W), blk3(HEADS_W, tm), blk3(tm, LANES), colT(N_HEADS * AUG_PER_HEAD),
                 blk3(1, LANES), blk3(N_HEADS, LANES), blk3(1, LANES))
    in_specs = [row(D_MODEL)] + [_full_spec(a.shape) for a in (g, w, wfqt, wfvt, bf, tri, pk, pqt, grp)]
    return pl.pallas_call(
        _proj_fox_kernel, grid=(nb,), in_specs=in_specs, out_specs=out_specs, out_shape=out_shape,
        scratch_shapes=[pltpu.VMEM((1, LANES), F32)],
        compiler_params=_params(1), name="proj_fox",
    )(x, g, w, wfqt, wfvt, bf, tri, pk, pqt, grp)


def _attn_kernel(tab_ref, qp_ref, qe_ref, ka_ref, ke_ref, vT_ref, o_ref, acc_ref, s_ref, *, fox):
    g = pl.program_id(0)
    i = pl.program_id(1)
    tq = qp_ref.shape[1]
    tk = ka_ref.shape[1]
    rows_p = lax.broadcasted_iota(jnp.int32, (LANES, tq), 0)
    rows_e = lax.broadcasted_iota(jnp.int32, qe_ref.shape, 0)
    kpos = lax.broadcasted_iota(jnp.int32, (tk, tq), 0)
    qpos = lax.broadcasted_iota(jnp.int32, (tk, tq), 1)
    if fox:
        visible = kpos <= qpos
    else:
        visible = (kpos // CHUNK) <= (qpos // CHUNK)
    qe = qe_ref[...]

    ws = []
    for hq in range(HEADS_PER_STEP):
        pair, hh = divmod(hq, 2)
        qp = qp_ref[pair * LANES:(pair + 1) * LANES, :]
        keep_p = (rows_p >= hh * HEAD_DV) & (rows_p < (hh + 1) * HEAD_DV)
        if fox:
            h = HEADS_PER_STEP * g + hq
            keep_e = (rows_e >= h * AUG_PER_HEAD) & (rows_e < (h + 1) * AUG_PER_HEAD)
            extra = jnp.where(keep_e, qe, jnp.zeros_like(qe))
        else:
            extra = qe[hq * ROPE_DIM:(hq + 1) * ROPE_DIM]
        zero_rows = jnp.zeros((LANES - extra.shape[0], tq), BF16)
        ws.append(jnp.concatenate([jnp.where(keep_p, qp, jnp.zeros_like(qp)), extra, zero_rows], axis=0))

    steps = [(u, hq) for u in range(tk // KEY_SUB) for hq in range(HEADS_PER_STEP)]
    ones_rows = jnp.ones((DEN_ROWS, KEY_SUB), BF16)

    def scores(j, step):
        u, hq = step
        pair = hq // 2
        rows = slice(u * KEY_SUB, (u + 1) * KEY_SUB)
        lhs = jnp.concatenate([ka_ref[j, rows, pair * LANES:(pair + 1) * LANES], ke_ref[j, rows, :]], axis=1)
        return _dot(lhs, ws[hq])

    def block(j, stats, masked, j_next):
        stats = list(stats)
        n_steps = len(steps)
        tiles = {}
        for n, (u, hq) in enumerate(steps):
            ahead = n + LOOKAHEAD
            if ahead < n_steps:
                tiles[ahead] = scores(j, steps[ahead])
            elif j_next is not None:
                tiles[ahead] = scores(j_next, steps[ahead - n_steps])
            sT = tiles.pop(n) if n in tiles else s_ref[n]
            rows = slice(u * KEY_SUB, (u + 1) * KEY_SUB)
            m_old = stats[hq]
            if masked:
                sT = jnp.where(visible[rows], sT, NEG)
            if fox:
                h = HEADS_PER_STEP * g + hq
                d = tab_ref[0, h, i] - tab_ref[0, h, j]
            else:
                d = 0.0
            m_new = jnp.maximum(m_old, jnp.max(sT, axis=0, keepdims=True) + d)
            alpha = jnp.exp2(m_old - m_new)
            pT = jnp.exp2((sT - (m_new - d)).astype(BF16))
            stats[hq] = m_new
            v = jnp.concatenate([vT_ref[j, hq * HEAD_DV:(hq + 1) * HEAD_DV, rows], ones_rows], axis=0)
            acc_ref[hq] = alpha * acc_ref[hq] + _dot(v, pT)
        for n, tile in tiles.items():
            s_ref[n - n_steps] = tile
        return tuple(stats)

    def more(j_top, stats):
        alive = j_top >= 0
        if fox:
            jt = jnp.maximum(j_top, 0)
            slack = None
            for hq in range(HEADS_PER_STEP):
                h = HEADS_PER_STEP * g + hq
                bound = tab_ref[1, h, i] * tab_ref[2, h, jt] + (tab_ref[0, h, i] - tab_ref[0, h, jt + 1])
                room = stats[hq] - bound
                slack = room if slack is None else jnp.minimum(slack, room)
            alive = alive & (jnp.min(slack) < SKIP_LOG2)
        return alive.astype(jnp.int32)

    acc_ref[...] = jnp.zeros_like(acc_ref)
    for n in range(LOOKAHEAD):
        s_ref[n] = scores(i, steps[n])
    m0 = jnp.full((1, tq), NEG, F32)
    stats = block(i, (m0,) * HEADS_PER_STEP, True, jnp.maximum(i - 1, 0))

    def visit(c):
        j = c[0]
        st = block(j, c[2:], False, jnp.maximum(j - 1, 0))
        return (j - 1, more(j - 1, st)) + tuple(st)

    lax.while_loop(lambda c: c[1] > 0, visit, (i - 1, more(i - 1, stats)) + tuple(stats))
    outs = [acc_ref[hq, 0:HEAD_DV, :] / acc_ref[hq, HEAD_DV:HEAD_DV + 1, :] for hq in range(HEADS_PER_STEP)]
    o_ref[...] = jnp.concatenate(outs, axis=0).T


def _attention(tab, qpT, qeT, ka, ke, vT, fox):
    nb, tk, _ = ka.shape
    s = qpT.shape[1]
    tq = tk
    width = HEADS_PER_STEP * HEAD_DV
    e_rows = qeT.shape[0] if fox else HEADS_PER_STEP * ROPE_DIM
    e_map = (lambda g, i, b: (0, i)) if fox else (lambda g, i, b: (g, i))
    once = dict(pipeline_mode=pl.Buffered(1))
    grid_spec = pltpu.PrefetchScalarGridSpec(
        num_scalar_prefetch=1,
        grid=(N_HEADS // HEADS_PER_STEP, nb),
        in_specs=[
            pl.BlockSpec((width, tq), lambda g, i, b: (g, i)),
            pl.BlockSpec((e_rows, tq), e_map),
            pl.BlockSpec((nb, tk, width), lambda g, i, b: (0, 0, g), **once),
            pl.BlockSpec((nb, tk, LANES), lambda g, i, b: (0, 0, 0), **once),
            pl.BlockSpec((nb, width, tk), lambda g, i, b: (0, g, 0), **once),
        ],
        out_specs=pl.BlockSpec((tq, width), lambda g, i, b: (i, g)),
        scratch_shapes=[pltpu.VMEM((HEADS_PER_STEP, HEAD_DV + DEN_ROWS, tq), F32),
                        pltpu.VMEM((LOOKAHEAD, KEY_SUB, tq), F32)],
    )
    return pl.pallas_call(
        functools.partial(_attn_kernel, fox=fox), grid_spec=grid_spec,
        out_shape=jax.ShapeDtypeStruct((s, HEADS_W), F32),
        compiler_params=_params(2), name="attn_fox" if fox else "attn_mla",
    )(tab, qpT, qeT, ka, ke, vT)


def _softmax_pv(s, v):
    m = jnp.max(s, axis=-1, keepdims=True)
    p = jnp.exp2(s - m)
    l = jnp.sum(p, axis=-1, keepdims=True)
    return _dot(p.astype(BF16), v) / l


def _diag_heads(o_big, t_new):
    rows = lax.broadcasted_iota(jnp.int32, o_big.shape, 0) // t_new
    cols = lax.broadcasted_iota(jnp.int32, o_big.shape, 1) // HEAD_DV
    kept = jnp.where(rows == cols, o_big, 0.0)
    return jnp.sum(kept.reshape(N_HEADS, t_new, o_big.shape[1]), axis=0)


def _sample_attn_kernel(qn_ref, qr_ref, qf_ref, lrep_ref, cckv_ref, nckv_ref, ckr_ref, nkr_ref,
                        cfk_ref, nfk_ref, cfv_ref, nfv_ref, wukn_ref, wuv_ref, oa_ref, ob_ref):
    past = cckv_ref.shape[1]
    t_new = nckv_ref.shape[0]
    rows = N_HEADS * t_new
    pad = lrep_ref.shape[2] - past - t_new
    keys = past + t_new + pad

    def with_new(cache, new):
        return jnp.concatenate([cache, new, jnp.zeros((pad, new.shape[1]), new.dtype)], axis=0)

    kidx = lax.broadcasted_iota(jnp.int32, (rows, keys), 1)
    qpos = past + lax.broadcasted_iota(jnp.int32, (rows, keys), 0) % t_new
    real = kidx < past + t_new

    ckv_all = with_new(cckv_ref[0], nckv_ref[...]).astype(BF16)
    kn = _dot(ckv_all, wukn_ref[...]).astype(BF16)
    va = _dot(ckv_all, wuv_ref[...]).astype(BF16)
    kr = with_new(ckr_ref[0], nkr_ref[:, 0:ROPE_DIM]).astype(BF16)
    s = _dot_nt(qn_ref[0], kn) + _dot_nt(qr_ref[0], kr)
    s = jnp.where(real & ((kidx // CHUNK) <= (qpos // CHUNK)), s, NEG)
    oa_ref[...] = _diag_heads(_softmax_pv(s, va), t_new)

    fk = with_new(cfk_ref[0], nfk_ref[...]).astype(BF16)
    fv = with_new(cfv_ref[0], nfv_ref[...]).astype(BF16)
    c = lrep_ref[0]
    shift = 1
    while shift < keys:
        c = c + jnp.where(kidx >= shift, pltpu.roll(c, shift, 1), 0.0)
        shift *= 2
    cq = jnp.sum(jnp.where(kidx == qpos, c, 0.0), axis=-1, keepdims=True)
    s = _dot_nt(qf_ref[0], fk) + (cq - c) * LOG2E
    s = jnp.where(real & (kidx <= qpos), s, NEG)
    ob_ref[...] = _diag_heads(_softmax_pv(s, fv), t_new)


def _sample_attention(qn_bd, qr, qf_bd, lrep, cckv, nckv, ckr, nkr, cfk, nfk, cfv, nfv, wukn, wuv):
    nbatch, past, _ = cckv.shape
    t_new = nckv.shape[0] // nbatch
    b3 = lambda a: pl.BlockSpec((1,) + a.shape[1:], lambda b: (b, 0, 0))
    new = lambda a: pl.BlockSpec((t_new, a.shape[1]), lambda b: (b, 0))
    in_specs = [b3(qn_bd), b3(qr), b3(qf_bd), b3(lrep), b3(cckv), new(nckv), b3(ckr), new(nkr),
                b3(cfk), new(nfk), b3(cfv), new(nfv), _full_spec(wukn.shape), _full_spec(wuv.shape)]
    out = jax.ShapeDtypeStruct((nbatch * t_new, HEADS_W), F32)
    o_spec = pl.BlockSpec((t_new, HEADS_W), lambda b: (b, 0))
    return pl.pallas_call(
        _sample_attn_kernel, grid=(nbatch,), in_specs=in_specs, out_specs=(o_spec, o_spec),
        out_shape=(out, out), compiler_params=_params(1), name="attn_sample",
    )(qn_bd, qr, qf_bd, lrep, cckv, nckv, ckr, nkr, cfk, nfk, cfv, nfv, wukn, wuv)


def _merge_kernel(x_ref, oa_ref, za_ref, ob_ref, zb_ref, ga_ref, gb_ref, woa_ref, wob_ref, wout_ref,
                  fg_ref, xn_ref, *maybe_y_ref):
    za = za_ref[...]
    zb = zb_ref[...]
    a = _dot((oa_ref[...] * (za * _sigmoid(za))).astype(BF16), woa_ref[...])
    b = _dot((ob_ref[...] * (zb * _sigmoid(zb))).astype(BF16), wob_ref[...])
    mix = _sigmoid(ga_ref[...]) * a + _sigmoid(gb_ref[...]) * b
    xn = x_ref[...] + _dot(mix.astype(BF16), wout_ref[...])
    xn_ref[...] = xn
    if maybe_y_ref:
        maybe_y_ref[0][...] = _rms(xn, fg_ref[...])


def _merge(x, oa, za, ob, zb, ga, gb, woa, wob, wout, fg, tm, final):
    m = x.shape[0]
    row = lambda c: pl.BlockSpec((tm, c), lambda i: (i, 0))
    in_specs = [row(D_MODEL), row(HEADS_W), row(HEADS_W), row(HEADS_W), row(HEADS_W), row(D_MODEL),
                row(D_MODEL), _full_spec(woa.shape), _full_spec(wob.shape), _full_spec(wout.shape),
                _full_spec(fg.shape)]
    n_out = 2 if final else 1
    out_shape = tuple(jax.ShapeDtypeStruct((m, D_MODEL), F32) for _ in range(n_out))
    out_specs = tuple(row(D_MODEL) for _ in range(n_out))
    return pl.pallas_call(
        _merge_kernel, grid=(m // tm,), in_specs=in_specs, out_specs=out_specs, out_shape=out_shape,
        compiler_params=_params(1), name="merge",
    )(x, oa, za, ob, zb, ga, gb, woa, wob, wout, fg)


def _pad_cols(w, width):
    return jnp.pad(w, ((0, 0), (0, width - w.shape[1])))


def _layer_weights(w_in, w_uq, w_ukv, w_oa, w_ob, w_out, b_f):
    offs = np.cumsum(IN_SIZES)[:-1].tolist()
    w_cq, w_ckv, w_kr, w_za, w_fq, w_fk, w_fv, w_zf, w_zb, w_ga, w_gb = jnp.split(w_in, offs, axis=1)
    w_kr_swapped = jnp.concatenate([w_kr[:, HALF_ROPE:], w_kr[:, :HALF_ROPE]], axis=1)
    w_mla = jnp.concatenate([w_cq, w_ckv, _pad_cols(w_kr, LANES), _pad_cols(w_kr_swapped, LANES),
                             w_za, w_ga], axis=1).astype(BF16)
    w_fox = jnp.concatenate([w_fk, w_fv, _pad_cols(w_zf, LANES), w_zb, w_gb], axis=1).astype(BF16)
    uq = w_uq.reshape(Q_LORA, N_HEADS, NOPE_DIM + ROPE_DIM)
    uq_rows = jnp.concatenate([
        uq[:, :, :NOPE_DIM].reshape(Q_LORA, -1),
        uq[:, :, NOPE_DIM:NOPE_DIM + HALF_ROPE].reshape(Q_LORA, -1),
        uq[:, :, NOPE_DIM + HALF_ROPE:].reshape(Q_LORA, -1)], axis=1)
    ukv = w_ukv.reshape(KV_LORA, N_HEADS, NOPE_DIM + HEAD_DV)
    w_ukn = ukv[:, :, :NOPE_DIM].reshape(KV_LORA, -1).astype(BF16)
    w_uv = ukv[:, :, NOPE_DIM:].reshape(KV_LORA, -1).astype(BF16)
    return dict(
        w_mla=w_mla, w_fox=w_fox, w_uqT=uq_rows.T.astype(BF16), w_ukn=w_ukn, w_uv=w_uv, w_uvT=w_uv.T,
        w_fqT=w_fq.T.astype(BF16), w_fvT=w_fv.T.astype(BF16),
        b_f=_pad_cols(b_f[None, :], LANES),
        w_oa=w_oa.astype(BF16), w_ob=w_ob.astype(BF16), w_out=w_out.astype(BF16))


def _rope_tables(pos):
    inv = jnp.exp(-math.log(ROPE_THETA) * jnp.arange(HALF_ROPE, dtype=F32) / HALF_ROPE)
    ang = pos.astype(F32)[:, None] * inv[None, :]
    cos = jnp.cos(ang)
    sin = jnp.sin(ang)
    cosT = jnp.tile(cos.T, (N_HEADS, 1))
    sinT = jnp.tile(sin.T, (N_HEADS, 1))
    ctok = _pad_cols(jnp.concatenate([cos, cos], axis=1), LANES)
    stok = _pad_cols(jnp.concatenate([-sin, sin], axis=1), LANES)
    return cosT, sinT, ctok, stok


def _bias_placement():
    pk = np.zeros((3 * LANES, LANES), np.float32)
    pqt = np.zeros((N_HEADS * AUG_PER_HEAD, 3 * LANES), np.float32)
    for c in range(3):
        for h in range(N_HEADS):
            pk[c * LANES + h, h * AUG_PER_HEAD + 3 + c] = -1.0
            pqt[h * AUG_PER_HEAD + c, c * LANES + h] = 1.0
    return jnp.asarray(pk, BF16), jnp.asarray(pqt, BF16)


def _head_groups():
    grp = np.zeros((HEADS_W, LANES), np.float32)
    for h in range(N_HEADS):
        grp[h * HEAD_DV:(h + 1) * HEAD_DV, h] = 1.0
    return jnp.asarray(grp, BF16)


def _skip_tables(base, qn2, kn2):
    b = base[:, 0, :N_HEADS]
    qmax = jnp.sqrt(qn2[:, :, 0]) * NORM_SLACK
    kmax = lax.cummax(jnp.sqrt(kn2[:, 0, :N_HEADS]) * NORM_SLACK, axis=0)
    return jnp.stack([b, qmax, kmax]).transpose(0, 2, 1)


def _tri(n):
    return jnp.asarray(np.tril(np.ones((n, n), np.float32)), BF16)


def _block_diag_queries(qT, nbatch, t_new):
    q = qT.T.reshape(nbatch, t_new, N_HEADS, HEAD_DV)
    eye = jnp.eye(N_HEADS, dtype=q.dtype)
    return jnp.einsum('bthj,hg->bhtgj', q, eye).reshape(nbatch, N_HEADS * t_new, HEADS_W)


def _project(x, lw, norm_g, g_q, g_kv, tables, tri, pk, pqt, grp, tm):
    cosT, sinT, ctok, stok = tables
    g = norm_g[None, :]
    mla = _proj_mla(x, g, lw['w_mla'], g_q[None, :], g_kv[None, :], lw['w_uqT'], lw['w_ukn'], lw['w_uvT'],
                    cosT, sinT, ctok, stok, tm)
    fox = _proj_fox(x, g, lw['w_fox'], lw['w_fqT'], lw['w_fvT'], lw['b_f'], tri, pk, pqt, grp, tm)
    return mla, fox


def kernel(x_prompt, x_sample, cache_mla_ckv, cache_mla_krope, cache_fox_k, cache_fox_v, cache_fox_logf,
           norm_g, w_in, g_q, w_uq, g_kv, w_ukv, b_f, w_oa, w_ob, w_out, final_g):
    depth = w_in.shape[0]
    _, seq, _ = x_prompt.shape
    nbatch, t_new, _ = x_sample.shape
    past = cache_mla_ckv.shape[2]
    m_s = nbatch * t_new
    key_pad = -(past + t_new) % LANES

    tabs_p = _rope_tables(jnp.arange(seq, dtype=jnp.int32))
    tabs_s = _rope_tables(past + jnp.arange(m_s, dtype=jnp.int32) % t_new)
    pk, pqt = _bias_placement()
    grp = _head_groups()
    tri_p = _tri(SEQ_BLOCK)
    tri_s = _tri(m_s)
    fg = final_g[None, :]

    xp = x_prompt.reshape(seq, D_MODEL)
    xs = x_sample.reshape(m_s, D_MODEL)
    outs = {k: [] for k in ('p_ckv', 'p_kr', 'p_fk', 'p_fv', 'p_lf', 's_ckv', 's_kr', 's_fk', 's_fv', 's_lf')}
    yp = ys = None
    for l in range(depth):
        final = l == depth - 1
        lw = _layer_weights(w_in[l], w_uq[l], w_ukv[l], w_oa[l], w_ob[l], w_out[l], b_f[l])

        mla, fox = _project(xp, lw, norm_g[l], g_q[l], g_kv[l], tabs_p, tri_p, pk, pqt, grp, SEQ_BLOCK)
        ckv, kr, za, ga, qnT, qrT, knb, krb, vT = mla
        fk, fv, logf, zb, gb, fqT, fkb, fvT, augk, augqT, base, qn2, kn2 = fox
        tab = _skip_tables(base, qn2, kn2)
        o_a = _attention(jnp.zeros_like(tab), qnT, qrT, knb, krb, vT, fox=False)
        o_b = _attention(tab, fqT, augqT, fkb, augk, fvT, fox=True)
        res = _merge(xp, o_a, za, o_b, zb, ga, gb, lw['w_oa'], lw['w_ob'], lw['w_out'], fg, SEQ_BLOCK, final)
        xp = res[0]
        if final:
            yp = res[1]
        outs['p_ckv'].append(ckv); outs['p_kr'].append(kr[:, :ROPE_DIM]); outs['p_fk'].append(fk)
        outs['p_fv'].append(fv); outs['p_lf'].append(logf)

        mla, fox = _project(xs, lw, norm_g[l], g_q[l], g_kv[l], tabs_s, tri_s, pk, pqt, grp, m_s)
        ckv, kr, za, ga, qnT, qrT, _, _, _ = mla
        fk, fv, logf, zb, gb, fqT = fox[:6]
        qn_bd = _block_diag_queries(qnT, nbatch, t_new)
        qf_bd = _block_diag_queries(fqT, nbatch, t_new)
        qr = qrT.reshape(N_HEADS, ROPE_DIM, nbatch, t_new).transpose(2, 0, 3, 1).reshape(
            nbatch, N_HEADS * t_new, ROPE_DIM)
        lf_all = jnp.concatenate([cache_fox_logf[l], logf.reshape(nbatch, t_new, N_HEADS),
                                  jnp.zeros((nbatch, key_pad, N_HEADS), F32)], axis=1)
        lrep = jnp.repeat(lf_all.transpose(0, 2, 1), t_new, axis=1)
        o_a, o_b = _sample_attention(
            qn_bd, qr, qf_bd, lrep, cache_mla_ckv[l], ckv, cache_mla_krope[l], kr,
            cache_fox_k[l].reshape(nbatch, past, HEADS_W), fk,
            cache_fox_v[l].reshape(nbatch, past, HEADS_W), fv, lw['w_ukn'], lw['w_uv'])
        res = _merge(xs, o_a, za, o_b, zb, ga, gb, lw['w_oa'], lw['w_ob'], lw['w_out'], fg, m_s, final)
        xs = res[0]
        if final:
            ys = res[1]
        outs['s_ckv'].append(ckv); outs['s_kr'].append(kr[:, :ROPE_DIM]); outs['s_fk'].append(fk)
        outs['s_fv'].append(fv); outs['s_lf'].append(logf)

    st = lambda name, shape: jnp.stack(outs[name]).reshape((depth,) + shape)
    return (yp.reshape(1, seq, D_MODEL), ys.reshape(nbatch, t_new, D_MODEL),
            st('p_ckv', (1, seq, KV_LORA)), st('p_kr', (1, seq, ROPE_DIM)),
            st('p_fk', (1, seq, N_HEADS, HEAD_DV)), st('p_fv', (1, seq, N_HEADS, HEAD_DV)),
            st('p_lf', (1, seq, N_HEADS)),
            st('s_ckv', (nbatch, t_new, KV_LORA)), st('s_kr', (nbatch, t_new, ROPE_DIM)),
            st('s_fk', (nbatch, t_new, N_HEADS, HEAD_DV)), st('s_fv', (nbatch, t_new, N_HEADS, HEAD_DV)),
            st('s_lf', (nbatch, t_new, N_HEADS)))
```

```python
import functools
import math

import numpy as np
import jax
import jax.numpy as jnp
from jax import lax
from jax.experimental import pallas as pl
from jax.experimental.pallas import tpu as pltpu

D_MODEL = 1024
N_HEADS = 8
NOPE_DIM = 64
ROPE_DIM = 32
HALF_ROPE = ROPE_DIM // 2
HEAD_DV = 64
Q_LORA = 384
KV_LORA = 256
HEADS_W = N_HEADS * HEAD_DV
CHUNK = 64
ROPE_THETA = 10000.0
LOG2E = math.log2(math.e)
MLA_QSCALE = LOG2E / math.sqrt(NOPE_DIM + ROPE_DIM)
FOX_QSCALE = LOG2E / math.sqrt(HEAD_DV)
EPS = 1e-6
NEG = -1e30
IN_SIZES = (Q_LORA, KV_LORA, ROPE_DIM, HEADS_W, HEADS_W, HEADS_W, HEADS_W, N_HEADS, HEADS_W, D_MODEL, D_MODEL)

LANES = 128
SEQ_BLOCK = 512
KEY_SUB = 256
DEN_ROWS = 16
HEADS_PER_STEP = 4
LOOKAHEAD = 3
SAFE_LOG2 = 100.0
SKIP_LOG2 = 64.0
NORM_SLACK = 1.01
AUG_PER_HEAD = 8
VMEM_LIMIT_BYTES = 56 * 1024 * 1024

F32 = jnp.float32
BF16 = jnp.bfloat16


def _dot(a, b):
    return jnp.dot(a, b, preferred_element_type=F32)


def _dot_nt(a, b):
    return lax.dot_general(a, b, (((1,), (1,)), ((), ())), preferred_element_type=F32)


def _rms(x, g):
    return x * lax.rsqrt(jnp.mean(x * x, axis=-1, keepdims=True) + EPS) * g


def _sigmoid(x):
    return 1.0 / (1.0 + jnp.exp(-x))


def _split3(x):
    hi = x.astype(BF16)
    r1 = x - hi.astype(F32)
    mid = r1.astype(BF16)
    lo = (r1 - mid.astype(F32)).astype(BF16)
    return hi, mid, lo


def _full_spec(shape):
    nd = len(shape)
    return pl.BlockSpec(shape, lambda *_: (0,) * nd)


def _params(n_axes):
    return pltpu.CompilerParams(dimension_semantics=("arbitrary",) * n_axes,
                                vmem_limit_bytes=VMEM_LIMIT_BYTES)


MLA_IN_COLS = Q_LORA + KV_LORA + 2 * LANES + HEADS_W + D_MODEL


def _proj_mla_kernel(x_ref, g_ref, w_ref, gq_ref, gkv_ref, wuq_ref, wukn_ref, wuvt_ref,
                     cosT_ref, sinT_ref, ctok_ref, stok_ref, grp_ref,
                     ckv_ref, kr_ref, za_ref, ga_ref, qnT_ref, qrT_ref, knb_ref, krb_ref, vT_ref,
                     qn2_ref, kn2_ref):
    tm = x_ref.shape[0]
    hb = _rms(x_ref[...], g_ref[...]).astype(BF16)
    o = 0
    zcq = _dot(hb, w_ref[:, o:o + Q_LORA]); o += Q_LORA
    zckv = _dot(hb, w_ref[:, o:o + KV_LORA]); o += KV_LORA
    zka = _dot(hb, w_ref[:, o:o + LANES]); o += LANES
    zkb = _dot(hb, w_ref[:, o:o + LANES]); o += LANES
    za_ref[...] = _dot(hb, w_ref[:, o:o + HEADS_W]); o += HEADS_W
    ga_ref[...] = _dot(hb, w_ref[:, o:o + D_MODEL])

    cqb = _rms(zcq, gq_ref[...]).astype(BF16)
    qT = _dot_nt(wuq_ref[...], cqb)
    qn = qT[0:HEADS_W] * MLA_QSCALE
    qnT_ref[...] = qn.astype(BF16)
    x1 = qT[HEADS_W:HEADS_W + LANES]
    x2 = qT[HEADS_W + LANES:HEADS_W + 2 * LANES]
    c = cosT_ref[...]
    s = sinT_ref[...]
    r1 = (x1 * c - x2 * s) * MLA_QSCALE
    r2 = (x1 * s + x2 * c) * MLA_QSCALE
    n1 = r1.astype(BF16)
    n2 = r2.astype(BF16)
    for h in range(N_HEADS):
        qrT_ref[h * ROPE_DIM:h * ROPE_DIM + HALF_ROPE, :] = n1[h * HALF_ROPE:(h + 1) * HALF_ROPE]
        qrT_ref[h * ROPE_DIM + HALF_ROPE:(h + 1) * ROPE_DIM, :] = n2[h * HALF_ROPE:(h + 1) * HALF_ROPE]
    qn2 = (jnp.sum((qn * qn).reshape(N_HEADS, NOPE_DIM, tm), axis=1)
           + jnp.sum((r1 * r1 + r2 * r2).reshape(N_HEADS, HALF_ROPE, tm), axis=1))
    qn2_ref[0] = jnp.broadcast_to(jnp.max(qn2, axis=1, keepdims=True), (N_HEADS, LANES))

    ckv = _rms(zckv, gkv_ref[...])
    ckv_ref[...] = ckv
    cb = ckv.astype(BF16)
    kn = _dot(cb, wukn_ref[...])
    knb_ref[0] = kn.astype(BF16)
    vT_ref[0] = _dot_nt(wuvt_ref[...], cb).astype(BF16)

    kr = zka * ctok_ref[...] + zkb * stok_ref[...]
    kr_ref[...] = kr
    krb_ref[0] = kr.astype(BF16)
    kn2 = _dot((kn * kn).astype(BF16), grp_ref[...]) + jnp.sum(kr * kr, axis=1, keepdims=True)
    kn2_ref[0] = jnp.max(kn2, axis=0, keepdims=True)


def _proj_mla(x, g, w, gq, gkv, wuq, wukn, wuvt, cosT, sinT, ctok, stok, grp, tm):
    m = x.shape[0]
    nb = m // tm
    row = lambda c: pl.BlockSpec((tm, c), lambda i: (i, 0))
    colT = lambda r: pl.BlockSpec((r, tm), lambda i: (0, i))
    blk3 = lambda a, b: pl.BlockSpec((1, a, b), lambda i: (i, 0, 0))
    out_shape = (
        jax.ShapeDtypeStruct((m, KV_LORA), F32),
        jax.ShapeDtypeStruct((m, LANES), F32),
        jax.ShapeDtypeStruct((m, HEADS_W), F32),
        jax.ShapeDtypeStruct((m, D_MODEL), F32),
        jax.ShapeDtypeStruct((HEADS_W, m), BF16),
        jax.ShapeDtypeStruct((N_HEADS * ROPE_DIM, m), BF16),
        jax.ShapeDtypeStruct((nb, tm, HEADS_W), BF16),
        jax.ShapeDtypeStruct((nb, tm, LANES), BF16),
        jax.ShapeDtypeStruct((nb, HEADS_W, tm), BF16),
        jax.ShapeDtypeStruct((nb, N_HEADS, LANES), F32),
        jax.ShapeDtypeStruct((nb, 1, LANES), F32),
    )
    out_specs = (row(KV_LORA), row(LANES), row(HEADS_W), row(D_MODEL), colT(HEADS_W),
                 colT(N_HEADS * ROPE_DIM), blk3(tm, HEADS_W), blk3(tm, LANES), blk3(HEADS_W, tm),
                 blk3(N_HEADS, LANES), blk3(1, LANES))
    in_specs = [row(D_MODEL), _full_spec(g.shape), _full_spec(w.shape), _full_spec(gq.shape),
                _full_spec(gkv.shape), _full_spec(wuq.shape), _full_spec(wukn.shape), _full_spec(wuvt.shape),
                colT(LANES), colT(LANES), row(LANES), row(LANES), _full_spec(grp.shape)]
    return pl.pallas_call(
        _proj_mla_kernel, grid=(nb,), in_specs=in_specs, out_specs=out_specs, out_shape=out_shape,
        compiler_params=_params(1), name="proj_mla",
    )(x, g, w, gq, gkv, wuq, wukn, wuvt, cosT, sinT, ctok, stok, grp)


FOX_IN_COLS = 2 * HEADS_W + LANES + HEADS_W + D_MODEL


def _proj_fox_kernel(x_ref, g_ref, w_ref, wfqt_ref, wfvt_ref, bf_ref, tri_ref, pk_ref, pqt_ref, grp_ref,
                     fk_ref, fv_ref, logf_ref, zb_ref, gb_ref, fqT_ref, fkb_ref, fvT_ref,
                     augk_ref, augqT_ref, base_ref, qn2_ref, kn2_ref, carry_ref):
    i = pl.program_id(0)
    tm = x_ref.shape[0]
    hb = _rms(x_ref[...], g_ref[...]).astype(BF16)
    o = 0
    fk = _dot(hb, w_ref[:, o:o + HEADS_W]); o += HEADS_W
    fk_ref[...] = fk
    fkb_ref[0] = fk.astype(BF16)
    kn2_ref[0] = jnp.max(_dot((fk * fk).astype(BF16), grp_ref[...]), axis=0, keepdims=True)
    fv_ref[...] = _dot(hb, w_ref[:, o:o + HEADS_W]); o += HEADS_W
    zf = _dot(hb, w_ref[:, o:o + LANES]) + bf_ref[...]; o += LANES
    zb_ref[...] = _dot(hb, w_ref[:, o:o + HEADS_W]); o += HEADS_W
    gb_ref[...] = _dot(hb, w_ref[:, o:o + D_MODEL])
    fqT = _dot_nt(wfqt_ref[...], hb) * FOX_QSCALE
    fqT_ref[...] = fqT.astype(BF16)
    qn2 = jnp.sum((fqT * fqT).reshape(N_HEADS, HEAD_DV, tm), axis=1)
    qn2_ref[0] = jnp.broadcast_to(jnp.max(qn2, axis=1, keepdims=True), (N_HEADS, LANES))
    fvT_ref[0] = _dot_nt(wfvt_ref[...], hb).astype(BF16)

    lane = lax.broadcasted_iota(jnp.int32, (tm, LANES), 1)
    logf = jnp.minimum(zf, 0.0) - jnp.log(1.0 + jnp.exp(-jnp.abs(zf)))
    logf = jnp.where(lane < N_HEADS, logf, 0.0)
    logf_ref[...] = logf[:, 0:N_HEADS]

    hi, mid, lo = _split3(logf)
    tri = tri_ref[...]
    r = _dot(tri, hi) + _dot(tri, mid) + _dot(tri, lo)

    @pl.when(i == 0)
    def _():
        carry_ref[...] = jnp.zeros_like(carry_ref)

    base_ref[0] = carry_ref[...] * LOG2E
    carry_ref[...] = carry_ref[...] + r[tm - 1:tm, :]

    rh, rm, rl = _split3(r * LOG2E)
    rcat = jnp.concatenate([rh, rm, rl], axis=1)
    slot = lane % AUG_PER_HEAD
    ones_k = jnp.where((lane < N_HEADS * AUG_PER_HEAD) & (slot < 3), 1.0, 0.0)
    augk_ref[0] = (_dot(rcat, pk_ref[...]) + ones_k).astype(BF16)
    rowq = lax.broadcasted_iota(jnp.int32, (N_HEADS * AUG_PER_HEAD, tm), 0) % AUG_PER_HEAD
    ones_q = jnp.where((rowq >= 3) & (rowq < 6), 1.0, 0.0)
    augqT_ref[...] = (_dot_nt(pqt_ref[...], rcat) + ones_q).astype(BF16)


def _proj_fox(x, g, w, wfqt, wfvt, bf, tri, pk, pqt, grp, tm):
    m = x.shape[0]
    nb = m // tm
    row = lambda c: pl.BlockSpec((tm, c), lambda i: (i, 0))
    colT = lambda r: pl.BlockSpec((r, tm), lambda i: (0, i))
    blk3 = lambda a, b: pl.BlockSpec((1, a, b), lambda i: (i, 0, 0))
    out_shape = (
        jax.ShapeDtypeStruct((m, HEADS_W), F32),
        jax.ShapeDtypeStruct((m, HEADS_W), F32),
        jax.ShapeDtypeStruct((m, N_HEADS), F32),
        jax.ShapeDtypeStruct((m, HEADS_W), F32),
        jax.ShapeDtypeStruct((m, D_MODEL), F32),
        jax.ShapeDtypeStruct((HEADS_W, m), BF16),
        jax.ShapeDtypeStruct((nb, tm, HEADS_W), BF16),
        jax.ShapeDtypeStruct((nb, HEADS_W, tm), BF16),
        jax.ShapeDtypeStruct((nb, tm, LANES), BF16),
        jax.ShapeDtypeStruct((N_HEADS * AUG_PER_HEAD, m), BF16),
        jax.ShapeDtypeStruct((nb, 1, LANES), F32),
        jax.ShapeDtypeStruct((nb, N_HEADS, LANES), F32),
        jax.ShapeDtypeStruct((nb, 1, LANES), F32),
    )
    out_specs = (row(HEADS_W), row(HEADS_W), row(N_HEADS), row(HEADS_W), row(D_MODEL), colT(HEADS_W),
                 blk3(tm, HEADS_W), blk3(HEADS_W, tm), blk3(tm, LANES), colT(N_HEADS * AUG_PER_HEAD),
                 blk3(1, LANES), blk3(N_HEADS, LANES), blk3(1, LANES))
    in_specs = [row(D_MODEL)] + [_full_spec(a.shape) for a in (g, w, wfqt, wfvt, bf, tri, pk, pqt, grp)]
    return pl.pallas_call(
        _proj_fox_kernel, grid=(nb,), in_specs=in_specs, out_specs=out_specs, out_shape=out_shape,
        scratch_shapes=[pltpu.VMEM((1, LANES), F32)],
        compiler_params=_params(1), name="proj_fox",
    )(x, g, w, wfqt, wfvt, bf, tri, pk, pqt, grp)


def _attn_kernel(tab_ref, qp_ref, qe_ref, ka_ref, ke_ref, vT_ref, o_ref, acc_ref, s_ref, *, fox):
    g = pl.program_id(0)
    i = pl.program_id(1)
    tq = qp_ref.shape[1]
    tk = ka_ref.shape[1]
    rows_p = lax.broadcasted_iota(jnp.int32, (LANES, tq), 0)
    rows_e = lax.broadcasted_iota(jnp.int32, qe_ref.shape, 0)
    kpos = lax.broadcasted_iota(jnp.int32, (tk, tq), 0)
    qpos = lax.broadcasted_iota(jnp.int32, (tk, tq), 1)
    if fox:
        visible = kpos <= qpos
    else:
        visible = (kpos // CHUNK) <= (qpos // CHUNK)
    qe = qe_ref[...]

    ws = []
    for hq in range(HEADS_PER_STEP):
        pair, hh = divmod(hq, 2)
        qp = qp_ref[pair * LANES:(pair + 1) * LANES, :]
        keep_p = (rows_p >= hh * HEAD_DV) & (rows_p < (hh + 1) * HEAD_DV)
        if fox:
            h = HEADS_PER_STEP * g + hq
            keep_e = (rows_e >= h * AUG_PER_HEAD) & (rows_e < (h + 1) * AUG_PER_HEAD)
            extra = jnp.where(keep_e, qe, jnp.zeros_like(qe))
        else:
            extra = qe[hq * ROPE_DIM:(hq + 1) * ROPE_DIM]
        zero_rows = jnp.zeros((LANES - extra.shape[0], tq), BF16)
        ws.append(jnp.concatenate([jnp.where(keep_p, qp, jnp.zeros_like(qp)), extra, zero_rows], axis=0))

    steps = [(u, hq) for u in range(tk // KEY_SUB) for hq in range(HEADS_PER_STEP)]
    ones_rows = jnp.ones((DEN_ROWS, KEY_SUB), BF16)

    def scores(j, step):
        u, hq = step
        pair = hq // 2
        rows = slice(u * KEY_SUB, (u + 1) * KEY_SUB)
        lhs = jnp.concatenate([ka_ref[j, rows, pair * LANES:(pair + 1) * LANES], ke_ref[j, rows, :]], axis=1)
        return _dot(lhs, ws[hq])

    def block(j, stats, masked, j_next):
        stats = list(stats)
        n_steps = len(steps)
        tiles = {}
        for n, (u, hq) in enumerate(steps):
            ahead = n + LOOKAHEAD
            if ahead < n_steps:
                tiles[ahead] = scores(j, steps[ahead])
            elif j_next is not None:
                tiles[ahead] = scores(j_next, steps[ahead - n_steps])
            sT = tiles.pop(n) if n in tiles else s_ref[n]
            rows = slice(u * KEY_SUB, (u + 1) * KEY_SUB)
            m_old = stats[hq]
            if masked:
                sT = jnp.where(visible[rows], sT, NEG)
            if fox:
                h = HEADS_PER_STEP * g + hq
                d = tab_ref[0, h, i] - tab_ref[0, h, j]
            else:
                d = 0.0
            m_new = jnp.maximum(m_old, jnp.max(sT, axis=0, keepdims=True) + d)
            alpha = jnp.exp2(m_old - m_new)
            pT = jnp.exp2((sT - (m_new - d)).astype(BF16))
            stats[hq] = m_new
            v = jnp.concatenate([vT_ref[j, hq * HEAD_DV:(hq + 1) * HEAD_DV, rows], ones_rows], axis=0)
            acc_ref[hq] = alpha * acc_ref[hq] + _dot(v, pT)
        for n, tile in tiles.items():
            s_ref[n - n_steps] = tile
        return tuple(stats)

    def more(j_top, stats):
        alive = j_top >= 0
        if fox:
            jt = jnp.maximum(j_top, 0)
            slack = None
            for hq in range(HEADS_PER_STEP):
                h = HEADS_PER_STEP * g + hq
                bound = tab_ref[1, h, i] * tab_ref[2, h, jt] + (tab_ref[0, h, i] - tab_ref[0, h, jt + 1])
                room = stats[hq] - bound
                slack = room if slack is None else jnp.minimum(slack, room)
            alive = alive & (jnp.min(slack) < SKIP_LOG2)
        return alive.astype(jnp.int32)

    def block_unshifted(j, masked, j_next):
        n_steps = len(steps)
        tiles = {}
        for n, (u, hq) in enumerate(steps):
            ahead = n + LOOKAHEAD
            if ahead < n_steps:
                tiles[ahead] = scores(j, steps[ahead])
            else:
                tiles[ahead] = scores(j_next, steps[ahead - n_steps])
            sT = tiles.pop(n) if n in tiles else s_ref[n]
            rows = slice(u * KEY_SUB, (u + 1) * KEY_SUB)
            if masked:
                sT = jnp.where(visible[rows], sT, NEG)
            pT = jnp.exp2(sT).astype(BF16)
            v = jnp.concatenate([vT_ref[j, hq * HEAD_DV:(hq + 1) * HEAD_DV, rows], ones_rows], axis=0)
            acc_ref[hq] = acc_ref[hq] + _dot(v, pT)
        for n, tile in tiles.items():
            s_ref[n - n_steps] = tile

    def walk_online():
        m0 = jnp.full((1, tq), NEG, F32)
        stats = block(i, (m0,) * HEADS_PER_STEP, True, jnp.maximum(i - 1, 0))

        def visit(c):
            j = c[0]
            st = block(j, c[2:], False, jnp.maximum(j - 1, 0))
            return (j - 1, more(j - 1, st)) + tuple(st)

        lax.while_loop(lambda c: c[1] > 0, visit, (i - 1, more(i - 1, stats)) + tuple(stats))

    def walk_unshifted():
        block_unshifted(i, True, jnp.maximum(i - 1, 0))

        def visit(k, c):
            j = i - 1 - k
            block_unshifted(j, False, jnp.maximum(j - 1, 0))
            return c

        lax.fori_loop(0, i, visit, 0)

    acc_ref[...] = jnp.zeros_like(acc_ref)
    for n in range(LOOKAHEAD):
        s_ref[n] = scores(i, steps[n])
    if fox:
        walk_online()
    else:
        bounded = None
        for hq in range(HEADS_PER_STEP):
            h = HEADS_PER_STEP * g + hq
            ok = tab_ref[1, h, i] * tab_ref[2, h, i] <= SAFE_LOG2
            bounded = ok if bounded is None else bounded & ok
        pl.when(bounded)(walk_unshifted)
        pl.when(jnp.logical_not(bounded))(walk_online)
    outs = [acc_ref[hq, 0:HEAD_DV, :] / acc_ref[hq, HEAD_DV:HEAD_DV + 1, :] for hq in range(HEADS_PER_STEP)]
    o_ref[...] = jnp.concatenate(outs, axis=0).T


def _attention(tab, qpT, qeT, ka, ke, vT, fox):
    nb, tk, _ = ka.shape
    s = qpT.shape[1]
    tq = tk
    width = HEADS_PER_STEP * HEAD_DV
    e_rows = qeT.shape[0] if fox else HEADS_PER_STEP * ROPE_DIM
    e_map = (lambda g, i, b: (0, i)) if fox else (lambda g, i, b: (g, i))
    once = dict(pipeline_mode=pl.Buffered(1))
    grid_spec = pltpu.PrefetchScalarGridSpec(
        num_scalar_prefetch=1,
        grid=(N_HEADS // HEADS_PER_STEP, nb),
        in_specs=[
            pl.BlockSpec((width, tq), lambda g, i, b: (g, i)),
            pl.BlockSpec((e_rows, tq), e_map),
            pl.BlockSpec((nb, tk, width), lambda g, i, b: (0, 0, g), **once),
            pl.BlockSpec((nb, tk, LANES), lambda g, i, b: (0, 0, 0), **once),
            pl.BlockSpec((nb, width, tk), lambda g, i, b: (0, g, 0), **once),
        ],
        out_specs=pl.BlockSpec((tq, width), lambda g, i, b: (i, g)),
        scratch_shapes=[pltpu.VMEM((HEADS_PER_STEP, HEAD_DV + DEN_ROWS, tq), F32),
                        pltpu.VMEM((LOOKAHEAD, KEY_SUB, tq), F32)],
    )
    return pl.pallas_call(
        functools.partial(_attn_kernel, fox=fox), grid_spec=grid_spec,
        out_shape=jax.ShapeDtypeStruct((s, HEADS_W), F32),
        compiler_params=_params(2), name="attn_fox" if fox else "attn_mla",
    )(tab, qpT, qeT, ka, ke, vT)


def _softmax_pv(s, v):
    m = jnp.max(s, axis=-1, keepdims=True)
    p = jnp.exp2(s - m)
    l = jnp.sum(p, axis=-1, keepdims=True)
    return _dot(p.astype(BF16), v) / l


def _diag_heads(o_big, t_new):
    rows = lax.broadcasted_iota(jnp.int32, o_big.shape, 0) // t_new
    cols = lax.broadcasted_iota(jnp.int32, o_big.shape, 1) // HEAD_DV
    kept = jnp.where(rows == cols, o_big, 0.0)
    return jnp.sum(kept.reshape(N_HEADS, t_new, o_big.shape[1]), axis=0)


def _sample_attn_kernel(qn_ref, qr_ref, qf_ref, lrep_ref, cckv_ref, nckv_ref, ckr_ref, nkr_ref,
                        cfk_ref, nfk_ref, cfv_ref, nfv_ref, wukn_ref, wuv_ref, oa_ref, ob_ref):
    past = cckv_ref.shape[1]
    t_new = nckv_ref.shape[0]
    rows = N_HEADS * t_new
    pad = lrep_ref.shape[2] - past - t_new
    keys = past + t_new + pad

    def with_new(cache, new):
        return jnp.concatenate([cache, new, jnp.zeros((pad, new.shape[1]), new.dtype)], axis=0)

    kidx = lax.broadcasted_iota(jnp.int32, (rows, keys), 1)
    qpos = past + lax.broadcasted_iota(jnp.int32, (rows, keys), 0) % t_new
    real = kidx < past + t_new

    ckv_all = with_new(cckv_ref[0], nckv_ref[...]).astype(BF16)
    kn = _dot(ckv_all, wukn_ref[...]).astype(BF16)
    va = _dot(ckv_all, wuv_ref[...]).astype(BF16)
    kr = with_new(ckr_ref[0], nkr_ref[:, 0:ROPE_DIM]).astype(BF16)
    s = _dot_nt(qn_ref[0], kn) + _dot_nt(qr_ref[0], kr)
    s = jnp.where(real & ((kidx // CHUNK) <= (qpos // CHUNK)), s, NEG)
    oa_ref[...] = _diag_heads(_softmax_pv(s, va), t_new)

    fk = with_new(cfk_ref[0], nfk_ref[...]).astype(BF16)
    fv = with_new(cfv_ref[0], nfv_ref[...]).astype(BF16)
    c = lrep_ref[0]
    shift = 1
    while shift < keys:
        c = c + jnp.where(kidx >= shift, pltpu.roll(c, shift, 1), 0.0)
        shift *= 2
    cq = jnp.sum(jnp.where(kidx == qpos, c, 0.0), axis=-1, keepdims=True)
    s = _dot_nt(qf_ref[0], fk) + (cq - c) * LOG2E
    s = jnp.where(real & (kidx <= qpos), s, NEG)
    ob_ref[...] = _diag_heads(_softmax_pv(s, fv), t_new)


def _sample_attention(qn_bd, qr, qf_bd, lrep, cckv, nckv, ckr, nkr, cfk, nfk, cfv, nfv, wukn, wuv):
    nbatch, past, _ = cckv.shape
    t_new = nckv.shape[0] // nbatch
    b3 = lambda a: pl.BlockSpec((1,) + a.shape[1:], lambda b: (b, 0, 0))
    new = lambda a: pl.BlockSpec((t_new, a.shape[1]), lambda b: (b, 0))
    in_specs = [b3(qn_bd), b3(qr), b3(qf_bd), b3(lrep), b3(cckv), new(nckv), b3(ckr), new(nkr),
                b3(cfk), new(nfk), b3(cfv), new(nfv), _full_spec(wukn.shape), _full_spec(wuv.shape)]
    out = jax.ShapeDtypeStruct((nbatch * t_new, HEADS_W), F32)
    o_spec = pl.BlockSpec((t_new, HEADS_W), lambda b: (b, 0))
    return pl.pallas_call(
        _sample_attn_kernel, grid=(nbatch,), in_specs=in_specs, out_specs=(o_spec, o_spec),
        out_shape=(out, out), compiler_params=_params(1), name="attn_sample",
    )(qn_bd, qr, qf_bd, lrep, cckv, nckv, ckr, nkr, cfk, nfk, cfv, nfv, wukn, wuv)


def _merge_kernel(x_ref, oa_ref, za_ref, ob_ref, zb_ref, ga_ref, gb_ref, woa_ref, wob_ref, wout_ref,
                  fg_ref, xn_ref, *maybe_y_ref):
    za = za_ref[...]
    zb = zb_ref[...]
    a = _dot((oa_ref[...] * (za * _sigmoid(za))).astype(BF16), woa_ref[...])
    b = _dot((ob_ref[...] * (zb * _sigmoid(zb))).astype(BF16), wob_ref[...])
    mix = _sigmoid(ga_ref[...]) * a + _sigmoid(gb_ref[...]) * b
    xn = x_ref[...] + _dot(mix.astype(BF16), wout_ref[...])
    xn_ref[...] = xn
    if maybe_y_ref:
        maybe_y_ref[0][...] = _rms(xn, fg_ref[...])


def _merge(x, oa, za, ob, zb, ga, gb, woa, wob, wout, fg, tm, final):
    m = x.shape[0]
    row = lambda c: pl.BlockSpec((tm, c), lambda i: (i, 0))
    in_specs = [row(D_MODEL), row(HEADS_W), row(HEADS_W), row(HEADS_W), row(HEADS_W), row(D_MODEL),
                row(D_MODEL), _full_spec(woa.shape), _full_spec(wob.shape), _full_spec(wout.shape),
                _full_spec(fg.shape)]
    n_out = 2 if final else 1
    out_shape = tuple(jax.ShapeDtypeStruct((m, D_MODEL), F32) for _ in range(n_out))
    out_specs = tuple(row(D_MODEL) for _ in range(n_out))
    return pl.pallas_call(
        _merge_kernel, grid=(m // tm,), in_specs=in_specs, out_specs=out_specs, out_shape=out_shape,
        compiler_params=_params(1), name="merge",
    )(x, oa, za, ob, zb, ga, gb, woa, wob, wout, fg)


def _pad_cols(w, width):
    return jnp.pad(w, ((0, 0), (0, width - w.shape[1])))


def _layer_weights(w_in, w_uq, w_ukv, w_oa, w_ob, w_out, b_f):
    offs = np.cumsum(IN_SIZES)[:-1].tolist()
    w_cq, w_ckv, w_kr, w_za, w_fq, w_fk, w_fv, w_zf, w_zb, w_ga, w_gb = jnp.split(w_in, offs, axis=1)
    w_kr_swapped = jnp.concatenate([w_kr[:, HALF_ROPE:], w_kr[:, :HALF_ROPE]], axis=1)
    w_mla = jnp.concatenate([w_cq, w_ckv, _pad_cols(w_kr, LANES), _pad_cols(w_kr_swapped, LANES),
                             w_za, w_ga], axis=1).astype(BF16)
    w_fox = jnp.concatenate([w_fk, w_fv, _pad_cols(w_zf, LANES), w_zb, w_gb], axis=1).astype(BF16)
    uq = w_uq.reshape(Q_LORA, N_HEADS, NOPE_DIM + ROPE_DIM)
    uq_rows = jnp.concatenate([
        uq[:, :, :NOPE_DIM].reshape(Q_LORA, -1),
        uq[:, :, NOPE_DIM:NOPE_DIM + HALF_ROPE].reshape(Q_LORA, -1),
        uq[:, :, NOPE_DIM + HALF_ROPE:].reshape(Q_LORA, -1)], axis=1)
    ukv = w_ukv.reshape(KV_LORA, N_HEADS, NOPE_DIM + HEAD_DV)
    w_ukn = ukv[:, :, :NOPE_DIM].reshape(KV_LORA, -1).astype(BF16)
    w_uv = ukv[:, :, NOPE_DIM:].reshape(KV_LORA, -1).astype(BF16)
    return dict(
        w_mla=w_mla, w_fox=w_fox, w_uqT=uq_rows.T.astype(BF16), w_ukn=w_ukn, w_uv=w_uv, w_uvT=w_uv.T,
        w_fqT=w_fq.T.astype(BF16), w_fvT=w_fv.T.astype(BF16),
        b_f=_pad_cols(b_f[None, :], LANES),
        w_oa=w_oa.astype(BF16), w_ob=w_ob.astype(BF16), w_out=w_out.astype(BF16))


def _rope_tables(pos):
    inv = jnp.exp(-math.log(ROPE_THETA) * jnp.arange(HALF_ROPE, dtype=F32) / HALF_ROPE)
    ang = pos.astype(F32)[:, None] * inv[None, :]
    cos = jnp.cos(ang)
    sin = jnp.sin(ang)
    cosT = jnp.tile(cos.T, (N_HEADS, 1))
    sinT = jnp.tile(sin.T, (N_HEADS, 1))
    ctok = _pad_cols(jnp.concatenate([cos, cos], axis=1), LANES)
    stok = _pad_cols(jnp.concatenate([-sin, sin], axis=1), LANES)
    return cosT, sinT, ctok, stok


def _bias_placement():
    pk = np.zeros((3 * LANES, LANES), np.float32)
    pqt = np.zeros((N_HEADS * AUG_PER_HEAD, 3 * LANES), np.float32)
    for c in range(3):
        for h in range(N_HEADS):
            pk[c * LANES + h, h * AUG_PER_HEAD + 3 + c] = -1.0
            pqt[h * AUG_PER_HEAD + c, c * LANES + h] = 1.0
    return jnp.asarray(pk, BF16), jnp.asarray(pqt, BF16)


def _head_groups():
    grp = np.zeros((HEADS_W, LANES), np.float32)
    for h in range(N_HEADS):
        grp[h * HEAD_DV:(h + 1) * HEAD_DV, h] = 1.0
    return jnp.asarray(grp, BF16)


def _skip_tables(base, qn2, kn2):
    b = base[:, 0, :N_HEADS]
    qmax = jnp.sqrt(qn2[:, :, 0]) * NORM_SLACK
    kmax = lax.cummax(jnp.sqrt(kn2[:, 0, :N_HEADS]) * NORM_SLACK, axis=0)
    return jnp.stack([b, qmax, kmax]).transpose(0, 2, 1)


def _tri(n):
    return jnp.asarray(np.tril(np.ones((n, n), np.float32)), BF16)


def _block_diag_queries(qT, nbatch, t_new):
    q = qT.T.reshape(nbatch, t_new, N_HEADS, HEAD_DV)
    eye = jnp.eye(N_HEADS, dtype=q.dtype)
    return jnp.einsum('bthj,hg->bhtgj', q, eye).reshape(nbatch, N_HEADS * t_new, HEADS_W)


def _project(x, lw, norm_g, g_q, g_kv, tables, tri, pk, pqt, grp, tm):
    cosT, sinT, ctok, stok = tables
    g = norm_g[None, :]
    mla = _proj_mla(x, g, lw['w_mla'], g_q[None, :], g_kv[None, :], lw['w_uqT'], lw['w_ukn'], lw['w_uvT'],
                    cosT, sinT, ctok, stok, grp, tm)
    fox = _proj_fox(x, g, lw['w_fox'], lw['w_fqT'], lw['w_fvT'], lw['b_f'], tri, pk, pqt, grp, tm)
    return mla, fox


def kernel(x_prompt, x_sample, cache_mla_ckv, cache_mla_krope, cache_fox_k, cache_fox_v, cache_fox_logf,
           norm_g, w_in, g_q, w_uq, g_kv, w_ukv, b_f, w_oa, w_ob, w_out, final_g):
    depth = w_in.shape[0]
    _, seq, _ = x_prompt.shape
    nbatch, t_new, _ = x_sample.shape
    past = cache_mla_ckv.shape[2]
    m_s = nbatch * t_new
    key_pad = -(past + t_new) % LANES

    tabs_p = _rope_tables(jnp.arange(seq, dtype=jnp.int32))
    tabs_s = _rope_tables(past + jnp.arange(m_s, dtype=jnp.int32) % t_new)
    pk, pqt = _bias_placement()
    grp = _head_groups()
    tri_p = _tri(SEQ_BLOCK)
    tri_s = _tri(m_s)
    fg = final_g[None, :]

    xp = x_prompt.reshape(seq, D_MODEL)
    xs = x_sample.reshape(m_s, D_MODEL)
    outs = {k: [] for k in ('p_ckv', 'p_kr', 'p_fk', 'p_fv', 'p_lf', 's_ckv', 's_kr', 's_fk', 's_fv', 's_lf')}
    yp = ys = None
    for l in range(depth):
        final = l == depth - 1
        lw = _layer_weights(w_in[l], w_uq[l], w_ukv[l], w_oa[l], w_ob[l], w_out[l], b_f[l])

        mla, fox = _project(xp, lw, norm_g[l], g_q[l], g_kv[l], tabs_p, tri_p, pk, pqt, grp, SEQ_BLOCK)
        ckv, kr, za, ga, qnT, qrT, knb, krb, vT, qn2_a, kn2_a = mla
        fk, fv, logf, zb, gb, fqT, fkb, fvT, augk, augqT, base, qn2, kn2 = fox
        tab = _skip_tables(base, qn2, kn2)
        tab_a = _skip_tables(jnp.zeros_like(base), qn2_a, kn2_a)
        o_a = _attention(tab_a, qnT, qrT, knb, krb, vT, fox=False)
        o_b = _attention(tab, fqT, augqT, fkb, augk, fvT, fox=True)
        res = _merge(xp, o_a, za, o_b, zb, ga, gb, lw['w_oa'], lw['w_ob'], lw['w_out'], fg, SEQ_BLOCK, final)
        xp = res[0]
        if final:
            yp = res[1]
        outs['p_ckv'].append(ckv); outs['p_kr'].append(kr[:, :ROPE_DIM]); outs['p_fk'].append(fk)
        outs['p_fv'].append(fv); outs['p_lf'].append(logf)

        mla, fox = _project(xs, lw, norm_g[l], g_q[l], g_kv[l], tabs_s, tri_s, pk, pqt, grp, m_s)
        ckv, kr, za, ga, qnT, qrT = mla[:6]
        fk, fv, logf, zb, gb, fqT = fox[:6]
        qn_bd = _block_diag_queries(qnT, nbatch, t_new)
        qf_bd = _block_diag_queries(fqT, nbatch, t_new)
        qr = qrT.reshape(N_HEADS, ROPE_DIM, nbatch, t_new).transpose(2, 0, 3, 1).reshape(
            nbatch, N_HEADS * t_new, ROPE_DIM)
        lf_all = jnp.concatenate([cache_fox_logf[l], logf.reshape(nbatch, t_new, N_HEADS),
                                  jnp.zeros((nbatch, key_pad, N_HEADS), F32)], axis=1)
        lrep = jnp.repeat(lf_all.transpose(0, 2, 1), t_new, axis=1)
        o_a, o_b = _sample_attention(
            qn_bd, qr, qf_bd, lrep, cache_mla_ckv[l], ckv, cache_mla_krope[l], kr,
            cache_fox_k[l].reshape(nbatch, past, HEADS_W), fk,
            cache_fox_v[l].reshape(nbatch, past, HEADS_W), fv, lw['w_ukn'], lw['w_uv'])
        res = _merge(xs, o_a, za, o_b, zb, ga, gb, lw['w_oa'], lw['w_ob'], lw['w_out'], fg, m_s, final)
        xs = res[0]
        if final:
            ys = res[1]
        outs['s_ckv'].append(ckv); outs['s_kr'].append(kr[:, :ROPE_DIM]); outs['s_fk'].append(fk)
        outs['s_fv'].append(fv); outs['s_lf'].append(logf)

    st = lambda name, shape: jnp.stack(outs[name]).reshape((depth,) + shape)
    return (yp.reshape(1, seq, D_MODEL), ys.reshape(nbatch, t_new, D_MODEL),
            st('p_ckv', (1, seq, KV_LORA)), st('p_kr', (1, seq, ROPE_DIM)),
            st('p_fk', (1, seq, N_HEADS, HEAD_DV)), st('p_fv', (1, seq, N_HEADS, HEAD_DV)),
            st('p_lf', (1, seq, N_HEADS)),
            st('s_ckv', (nbatch, t_new, KV_LORA)), st('s_kr', (nbatch, t_new, ROPE_DIM)),
            st('s_fk', (nbatch, t_new, N_HEADS, HEAD_DV)), st('s_fv', (nbatch, t_new, N_HEADS, HEAD_DV)),
            st('s_lf', (nbatch, t_new, N_HEADS)))
```

```python
import functools
import math

import numpy as np
import jax
import jax.numpy as jnp
from jax import lax
from jax.experimental import pallas as pl
from jax.experimental.pallas import tpu as pltpu

D_MODEL = 1024
N_HEADS = 8
NOPE_DIM = 64
ROPE_DIM = 32
HALF_ROPE = ROPE_DIM // 2
HEAD_DV = 64
Q_LORA = 384
KV_LORA = 256
HEADS_W = N_HEADS * HEAD_DV
CHUNK = 64
ROPE_THETA = 10000.0
LOG2E = math.log2(math.e)
MLA_QSCALE = LOG2E / math.sqrt(NOPE_DIM + ROPE_DIM)
FOX_QSCALE = LOG2E / math.sqrt(HEAD_DV)
EPS = 1e-6
NEG = -1e30
IN_SIZES = (Q_LORA, KV_LORA, ROPE_DIM, HEADS_W, HEADS_W, HEADS_W, HEADS_W, N_HEADS, HEADS_W, D_MODEL, D_MODEL)

LANES = 128
SEQ_BLOCK = 512
KEY_SUB = 256
DEN_ROWS = 16
HEADS_PER_STEP = 8
LOOKAHEAD = 2
SAFE_LOG2 = 100.0
SKIP_LOG2 = 64.0
NORM_SLACK = 1.01
AUG_PER_HEAD = 8
VMEM_LIMIT_BYTES = 56 * 1024 * 1024

F32 = jnp.float32
BF16 = jnp.bfloat16


def _dot(a, b):
    return jnp.dot(a, b, preferred_element_type=F32)


def _dot_nt(a, b):
    return lax.dot_general(a, b, (((1,), (1,)), ((), ())), preferred_element_type=F32)


def _rms(x, g):
    return x * lax.rsqrt(jnp.mean(x * x, axis=-1, keepdims=True) + EPS) * g


def _sigmoid(x):
    return 1.0 / (1.0 + jnp.exp(-x))


def _silu(x):
    return x * _sigmoid(x)


def _split3(x):
    hi = x.astype(BF16)
    r1 = x - hi.astype(F32)
    mid = r1.astype(BF16)
    lo = (r1 - mid.astype(F32)).astype(BF16)
    return hi, mid, lo


def _full_spec(shape):
    nd = len(shape)
    return pl.BlockSpec(shape, lambda *_: (0,) * nd)


def _put(ref, val):
    ref[...] = val.reshape(ref.shape)


def _skip_refs(body, n_in, n_skip, *refs):
    return body(*refs[:n_in], *refs[n_in + n_skip:])


def _stacked(stack, m, tm, cols):
    if stack is None:
        return jax.ShapeDtypeStruct((m, cols), F32), pl.BlockSpec((tm, cols), lambda i: (i, 0))
    layer, depth, _ = stack
    return (jax.ShapeDtypeStruct((depth, m, cols), F32),
            pl.BlockSpec((1, tm, cols), lambda i: (layer, i, 0)))


def _alias_args(stack, n_in, out_indices):
    if stack is None or stack[2] is None:
        return [], [], {}
    prev = list(stack[2])
    specs = [pl.BlockSpec(memory_space=pl.ANY) for _ in prev]
    return prev, specs, {n_in + k: out for k, out in enumerate(out_indices)}


def _params(n_axes):
    return pltpu.CompilerParams(dimension_semantics=("arbitrary",) * n_axes,
                                vmem_limit_bytes=VMEM_LIMIT_BYTES)


MLA_IN_COLS = Q_LORA + KV_LORA + 2 * LANES + HEADS_W + D_MODEL


def _proj_mla_kernel(x_ref, g_ref, w_ref, gq_ref, gkv_ref, wuq_ref, wukn_ref, wuvt_ref,
                     cosT_ref, sinT_ref, ctok_ref, stok_ref, grp_ref,
                     ckv_ref, kr_ref, za_ref, ga_ref, qnT_ref, qrT_ref, knb_ref, krb_ref, vT_ref,
                     qn2_ref, kn2_ref):
    tm = x_ref.shape[0]
    hb = _rms(x_ref[...], g_ref[...]).astype(BF16)
    o = 0
    zcq = _dot(hb, w_ref[:, o:o + Q_LORA]); o += Q_LORA
    zckv = _dot(hb, w_ref[:, o:o + KV_LORA]); o += KV_LORA
    zka = _dot(hb, w_ref[:, o:o + LANES]); o += LANES
    zkb = _dot(hb, w_ref[:, o:o + LANES]); o += LANES
    za_ref[...] = _silu(_dot(hb, w_ref[:, o:o + HEADS_W])).astype(BF16); o += HEADS_W
    ga_ref[...] = _sigmoid(_dot(hb, w_ref[:, o:o + D_MODEL])).astype(BF16)

    cqb = _rms(zcq, gq_ref[...]).astype(BF16)
    qT = _dot_nt(wuq_ref[...], cqb)
    qn = qT[0:HEADS_W] * MLA_QSCALE
    qnT_ref[...] = qn.astype(BF16)
    x1 = qT[HEADS_W:HEADS_W + LANES]
    x2 = qT[HEADS_W + LANES:HEADS_W + 2 * LANES]
    c = cosT_ref[...]
    s = sinT_ref[...]
    r1 = (x1 * c - x2 * s) * MLA_QSCALE
    r2 = (x1 * s + x2 * c) * MLA_QSCALE
    n1 = r1.astype(BF16)
    n2 = r2.astype(BF16)
    for h in range(N_HEADS):
        qrT_ref[h * ROPE_DIM:h * ROPE_DIM + HALF_ROPE, :] = n1[h * HALF_ROPE:(h + 1) * HALF_ROPE]
        qrT_ref[h * ROPE_DIM + HALF_ROPE:(h + 1) * ROPE_DIM, :] = n2[h * HALF_ROPE:(h + 1) * HALF_ROPE]
    qn2 = (jnp.sum((qn * qn).reshape(N_HEADS, NOPE_DIM, tm), axis=1)
           + jnp.sum((r1 * r1 + r2 * r2).reshape(N_HEADS, HALF_ROPE, tm), axis=1))
    qn2_ref[0] = jnp.broadcast_to(jnp.max(qn2, axis=1, keepdims=True), (N_HEADS, LANES))

    ckv = _rms(zckv, gkv_ref[...])
    _put(ckv_ref, ckv)
    cb = ckv.astype(BF16)
    kn = _dot(cb, wukn_ref[...])
    knb_ref[0] = kn.astype(BF16)
    vT_ref[0] = _dot_nt(wuvt_ref[...], cb).astype(BF16)

    kr = zka * ctok_ref[...] + zkb * stok_ref[...]
    kr_ref[...] = kr
    krb_ref[0] = kr.astype(BF16)
    kn2 = _dot((kn * kn).astype(BF16), grp_ref[...]) + jnp.sum(kr * kr, axis=1, keepdims=True)
    kn2_ref[0] = jnp.max(kn2, axis=0, keepdims=True)


def _proj_mla(x, g, w, gq, gkv, wuq, wukn, wuvt, cosT, sinT, ctok, stok, grp, tm, stack=None):
    m = x.shape[0]
    nb = m // tm
    row = lambda c: pl.BlockSpec((tm, c), lambda i: (i, 0))
    colT = lambda r: pl.BlockSpec((r, tm), lambda i: (0, i))
    blk3 = lambda a, b: pl.BlockSpec((1, a, b), lambda i: (i, 0, 0))
    ckv_shape, ckv_spec = _stacked(stack, m, tm, KV_LORA)
    out_shape = (
        ckv_shape,
        jax.ShapeDtypeStruct((m, LANES), F32),
        jax.ShapeDtypeStruct((m, HEADS_W), BF16),
        jax.ShapeDtypeStruct((m, D_MODEL), BF16),
        jax.ShapeDtypeStruct((HEADS_W, m), BF16),
        jax.ShapeDtypeStruct((N_HEADS * ROPE_DIM, m), BF16),
        jax.ShapeDtypeStruct((nb, tm, HEADS_W), BF16),
        jax.ShapeDtypeStruct((nb, tm, LANES), BF16),
        jax.ShapeDtypeStruct((nb, HEADS_W, tm), BF16),
        jax.ShapeDtypeStruct((nb, N_HEADS, LANES), F32),
        jax.ShapeDtypeStruct((nb, 1, LANES), F32),
    )
    out_specs = (ckv_spec, row(LANES), row(HEADS_W), row(D_MODEL), colT(HEADS_W),
                 colT(N_HEADS * ROPE_DIM), blk3(tm, HEADS_W), blk3(tm, LANES), blk3(HEADS_W, tm),
                 blk3(N_HEADS, LANES), blk3(1, LANES))
    in_specs = [row(D_MODEL), _full_spec(g.shape), _full_spec(w.shape), _full_spec(gq.shape),
                _full_spec(gkv.shape), _full_spec(wuq.shape), _full_spec(wukn.shape), _full_spec(wuvt.shape),
                colT(LANES), colT(LANES), row(LANES), row(LANES), _full_spec(grp.shape)]
    prev, prev_specs, aliases = _alias_args(stack, len(in_specs), (0,))
    return pl.pallas_call(
        functools.partial(_skip_refs, _proj_mla_kernel, len(in_specs), len(prev)),
        grid=(nb,), in_specs=in_specs + prev_specs, out_specs=out_specs, out_shape=out_shape,
        input_output_aliases=aliases, compiler_params=_params(1), name="proj_mla",
    )(x, g, w, gq, gkv, wuq, wukn, wuvt, cosT, sinT, ctok, stok, grp, *prev)


FOX_IN_COLS = 2 * HEADS_W + LANES + HEADS_W + D_MODEL


def _proj_fox_kernel(x_ref, g_ref, w_ref, wfqt_ref, wfvt_ref, bf_ref, tri_ref, pk_ref, pqt_ref, grp_ref,
                     fk_ref, fv_ref, logf_ref, zb_ref, gb_ref, fqT_ref, fkb_ref, fvT_ref,
                     augk_ref, augqT_ref, base_ref, qn2_ref, kn2_ref, carry_ref):
    i = pl.program_id(0)
    tm = x_ref.shape[0]
    hb = _rms(x_ref[...], g_ref[...]).astype(BF16)
    o = 0
    fk = _dot(hb, w_ref[:, o:o + HEADS_W]); o += HEADS_W
    _put(fk_ref, fk)
    fkb_ref[0] = fk.astype(BF16)
    kn2_ref[0] = jnp.max(_dot((fk * fk).astype(BF16), grp_ref[...]), axis=0, keepdims=True)
    _put(fv_ref, _dot(hb, w_ref[:, o:o + HEADS_W])); o += HEADS_W
    zf = _dot(hb, w_ref[:, o:o + LANES]) + bf_ref[...]; o += LANES
    zb_ref[...] = _silu(_dot(hb, w_ref[:, o:o + HEADS_W])).astype(BF16); o += HEADS_W
    gb_ref[...] = _sigmoid(_dot(hb, w_ref[:, o:o + D_MODEL])).astype(BF16)
    fqT = _dot_nt(wfqt_ref[...], hb) * FOX_QSCALE
    fqT_ref[...] = fqT.astype(BF16)
    qn2 = jnp.sum((fqT * fqT).reshape(N_HEADS, HEAD_DV, tm), axis=1)
    qn2_ref[0] = jnp.broadcast_to(jnp.max(qn2, axis=1, keepdims=True), (N_HEADS, LANES))
    fvT_ref[0] = _dot_nt(wfvt_ref[...], hb).astype(BF16)

    lane = lax.broadcasted_iota(jnp.int32, (tm, LANES), 1)
    logf = jnp.minimum(zf, 0.0) - jnp.log(1.0 + jnp.exp(-jnp.abs(zf)))
    logf = jnp.where(lane < N_HEADS, logf, 0.0)
    logf_ref[...] = logf[:, 0:N_HEADS]

    hi, mid, lo = _split3(logf)
    tri = tri_ref[...]
    r = _dot(tri, hi) + _dot(tri, mid) + _dot(tri, lo)

    @pl.when(i == 0)
    def _():
        carry_ref[...] = jnp.zeros_like(carry_ref)

    base_ref[0] = carry_ref[...] * LOG2E
    carry_ref[...] = carry_ref[...] + r[tm - 1:tm, :]

    rh, rm, rl = _split3(r * LOG2E)
    rcat = jnp.concatenate([rh, rm, rl], axis=1)
    slot = lane % AUG_PER_HEAD
    ones_k = jnp.where((lane < N_HEADS * AUG_PER_HEAD) & (slot < 3), 1.0, 0.0)
    augk_ref[0] = (_dot(rcat, pk_ref[...]) + ones_k).astype(BF16)
    rowq = lax.broadcasted_iota(jnp.int32, (N_HEADS * AUG_PER_HEAD, tm), 0) % AUG_PER_HEAD
    ones_q = jnp.where((rowq >= 3) & (rowq < 6), 1.0, 0.0)
    augqT_ref[...] = (_dot_nt(pqt_ref[...], rcat) + ones_q).astype(BF16)


def _proj_fox(x, g, w, wfqt, wfvt, bf, tri, pk, pqt, grp, tm, stack=None):
    m = x.shape[0]
    nb = m // tm
    row = lambda c: pl.BlockSpec((tm, c), lambda i: (i, 0))
    colT = lambda r: pl.BlockSpec((r, tm), lambda i: (0, i))
    blk3 = lambda a, b: pl.BlockSpec((1, a, b), lambda i: (i, 0, 0))
    kv_shape, kv_spec = _stacked(stack, m, tm, HEADS_W)
    out_shape = (
        kv_shape,
        kv_shape,
        jax.ShapeDtypeStruct((m, N_HEADS), F32),
        jax.ShapeDtypeStruct((m, HEADS_W), BF16),
        jax.ShapeDtypeStruct((m, D_MODEL), BF16),
        jax.ShapeDtypeStruct((HEADS_W, m), BF16),
        jax.ShapeDtypeStruct((nb, tm, HEADS_W), BF16),
        jax.ShapeDtypeStruct((nb, HEADS_W, tm), BF16),
        jax.ShapeDtypeStruct((nb, tm, LANES), BF16),
        jax.ShapeDtypeStruct((N_HEADS * AUG_PER_HEAD, m), BF16),
        jax.ShapeDtypeStruct((nb, 1, LANES), F32),
        jax.ShapeDtypeStruct((nb, N_HEADS, LANES), F32),
        jax.ShapeDtypeStruct((nb, 1, LANES), F32),
    )
    out_specs = (kv_spec, kv_spec, row(N_HEADS), row(HEADS_W), row(D_MODEL), colT(HEADS_W),
                 blk3(tm, HEADS_W), blk3(HEADS_W, tm), blk3(tm, LANES), colT(N_HEADS * AUG_PER_HEAD),
                 blk3(1, LANES), blk3(N_HEADS, LANES), blk3(1, LANES))
    in_specs = [row(D_MODEL)] + [_full_spec(a.shape) for a in (g, w, wfqt, wfvt, bf, tri, pk, pqt, grp)]
    prev, prev_specs, aliases = _alias_args(stack, len(in_specs), (0, 1))
    return pl.pallas_call(
        functools.partial(_skip_refs, _proj_fox_kernel, len(in_specs), len(prev)),
        grid=(nb,), in_specs=in_specs + prev_specs, out_specs=out_specs, out_shape=out_shape,
        scratch_shapes=[pltpu.VMEM((1, LANES), F32)],
        input_output_aliases=aliases, compiler_params=_params(1), name="proj_fox",
    )(x, g, w, wfqt, wfvt, bf, tri, pk, pqt, grp, *prev)


def _attn_kernel(tab_ref, qp_ref, qe_ref, ka_ref, ke_ref, vT_ref, o_ref, acc_ref, s_ref, *, fox):
    g = pl.program_id(0)
    i = pl.program_id(1)
    tq = qp_ref.shape[1]
    tk = ka_ref.shape[1]
    rows_p = lax.broadcasted_iota(jnp.int32, (LANES, tq), 0)
    rows_e = lax.broadcasted_iota(jnp.int32, qe_ref.shape, 0)
    kpos = lax.broadcasted_iota(jnp.int32, (tk, tq), 0)
    qpos = lax.broadcasted_iota(jnp.int32, (tk, tq), 1)
    if fox:
        visible = kpos <= qpos
    else:
        visible = (kpos // CHUNK) <= (qpos // CHUNK)
    qe = qe_ref[...]

    ws = []
    for hq in range(HEADS_PER_STEP):
        pair, hh = divmod(hq, 2)
        qp = qp_ref[pair * LANES:(pair + 1) * LANES, :]
        keep_p = (rows_p >= hh * HEAD_DV) & (rows_p < (hh + 1) * HEAD_DV)
        if fox:
            h = HEADS_PER_STEP * g + hq
            keep_e = (rows_e >= h * AUG_PER_HEAD) & (rows_e < (h + 1) * AUG_PER_HEAD)
            extra = jnp.where(keep_e, qe, jnp.zeros_like(qe))
        else:
            extra = qe[hq * ROPE_DIM:(hq + 1) * ROPE_DIM]
        zero_rows = jnp.zeros((LANES - extra.shape[0], tq), BF16)
        ws.append(jnp.concatenate([jnp.where(keep_p, qp, jnp.zeros_like(qp)), extra, zero_rows], axis=0))

    steps = [(u, hq) for u in range(tk // KEY_SUB) for hq in range(HEADS_PER_STEP)]
    ones_rows = jnp.ones((DEN_ROWS, KEY_SUB), BF16)

    def scores(j, step):
        u, hq = step
        pair = hq // 2
        rows = slice(u * KEY_SUB, (u + 1) * KEY_SUB)
        lhs = jnp.concatenate([ka_ref[j, rows, pair * LANES:(pair + 1) * LANES], ke_ref[j, rows, :]], axis=1)
        return _dot(lhs, ws[hq])

    def block(j, stats, masked, j_next):
        stats = list(stats)
        n_steps = len(steps)
        tiles = {}
        for n, (u, hq) in enumerate(steps):
            ahead = n + LOOKAHEAD
            if ahead < n_steps:
                tiles[ahead] = scores(j, steps[ahead])
            elif j_next is not None:
                tiles[ahead] = scores(j_next, steps[ahead - n_steps])
            sT = tiles.pop(n) if n in tiles else s_ref[n]
            rows = slice(u * KEY_SUB, (u + 1) * KEY_SUB)
            m_old = stats[hq]
            if masked:
                sT = jnp.where(visible[rows], sT, NEG)
            if fox:
                h = HEADS_PER_STEP * g + hq
                d = tab_ref[0, h, i] - tab_ref[0, h, j]
            else:
                d = 0.0
            m_new = jnp.maximum(m_old, jnp.max(sT, axis=0, keepdims=True) + d)
            alpha = jnp.exp2(m_old - m_new)
            pT = jnp.exp2((sT - (m_new - d)).astype(BF16))
            stats[hq] = m_new
            v = jnp.concatenate([vT_ref[j, hq * HEAD_DV:(hq + 1) * HEAD_DV, rows], ones_rows], axis=0)
            acc_ref[hq] = alpha * acc_ref[hq] + _dot(v, pT)
        for n, tile in tiles.items():
            s_ref[n - n_steps] = tile
        return tuple(stats)

    def more(j_top, stats):
        alive = j_top >= 0
        if fox:
            jt = jnp.maximum(j_top, 0)
            slack = None
            for hq in range(HEADS_PER_STEP):
                h = HEADS_PER_STEP * g + hq
                bound = tab_ref[1, h, i] * tab_ref[2, h, jt] + (tab_ref[0, h, i] - tab_ref[0, h, jt + 1])
                room = stats[hq] - bound
                slack = room if slack is None else jnp.minimum(slack, room)
            alive = alive & (jnp.min(slack) < SKIP_LOG2)
        return alive.astype(jnp.int32)

    def block_unshifted(j, masked, j_next):
        n_steps = len(steps)
        tiles = {}
        for n, (u, hq) in enumerate(steps):
            ahead = n + LOOKAHEAD
            if ahead < n_steps:
                tiles[ahead] = scores(j, steps[ahead])
            else:
                tiles[ahead] = scores(j_next, steps[ahead - n_steps])
            sT = tiles.pop(n) if n in tiles else s_ref[n]
            rows = slice(u * KEY_SUB, (u + 1) * KEY_SUB)
            if masked:
                sT = jnp.where(visible[rows], sT, NEG)
            pT = jnp.exp2(sT).astype(BF16)
            v = jnp.concatenate([vT_ref[j, hq * HEAD_DV:(hq + 1) * HEAD_DV, rows], ones_rows], axis=0)
            acc_ref[hq] = acc_ref[hq] + _dot(v, pT)
        for n, tile in tiles.items():
            s_ref[n - n_steps] = tile

    def walk_online():
        m0 = jnp.full((1, tq), NEG, F32)
        stats = block(i, (m0,) * HEADS_PER_STEP, True, jnp.maximum(i - 1, 0))

        def visit(c):
            j = c[0]
            st = block(j, c[2:], False, jnp.maximum(j - 1, 0))
            return (j - 1, more(j - 1, st)) + tuple(st)

        lax.while_loop(lambda c: c[1] > 0, visit, (i - 1, more(i - 1, stats)) + tuple(stats))

    def walk_unshifted():
        block_unshifted(i, True, jnp.maximum(i - 1, 0))

        def visit(k, c):
            j = i - 1 - k
            block_unshifted(j, False, jnp.maximum(j - 1, 0))
            return c

        lax.fori_loop(0, i, visit, 0)

    acc_ref[...] = jnp.zeros_like(acc_ref)
    for n in range(LOOKAHEAD):
        s_ref[n] = scores(i, steps[n])
    if fox:
        walk_online()
    else:
        bounded = None
        for hq in range(HEADS_PER_STEP):
            h = HEADS_PER_STEP * g + hq
            ok = tab_ref[1, h, i] * tab_ref[2, h, i] <= SAFE_LOG2
            bounded = ok if bounded is None else bounded & ok
        pl.when(bounded)(walk_unshifted)
        pl.when(jnp.logical_not(bounded))(walk_online)
    outs = [acc_ref[hq, 0:HEAD_DV, :] / acc_ref[hq, HEAD_DV:HEAD_DV + 1, :] for hq in range(HEADS_PER_STEP)]
    o_ref[...] = jnp.concatenate(outs, axis=0).T.astype(BF16)


def _attention(tab, qpT, qeT, ka, ke, vT, fox):
    nb, tk, _ = ka.shape
    s = qpT.shape[1]
    tq = tk
    width = HEADS_PER_STEP * HEAD_DV
    e_rows = qeT.shape[0] if fox else HEADS_PER_STEP * ROPE_DIM
    e_map = (lambda g, i, b: (0, i)) if fox else (lambda g, i, b: (g, i))
    once = dict(pipeline_mode=pl.Buffered(1))
    grid_spec = pltpu.PrefetchScalarGridSpec(
        num_scalar_prefetch=1,
        grid=(N_HEADS // HEADS_PER_STEP, nb),
        in_specs=[
            pl.BlockSpec((width, tq), lambda g, i, b: (g, i)),
            pl.BlockSpec((e_rows, tq), e_map),
            pl.BlockSpec((nb, tk, width), lambda g, i, b: (0, 0, g), **once),
            pl.BlockSpec((nb, tk, LANES), lambda g, i, b: (0, 0, 0), **once),
            pl.BlockSpec((nb, width, tk), lambda g, i, b: (0, g, 0), **once),
        ],
        out_specs=pl.BlockSpec((tq, width), lambda g, i, b: (i, g)),
        scratch_shapes=[pltpu.VMEM((HEADS_PER_STEP, HEAD_DV + DEN_ROWS, tq), F32),
                        pltpu.VMEM((LOOKAHEAD, KEY_SUB, tq), F32)],
    )
    return pl.pallas_call(
        functools.partial(_attn_kernel, fox=fox), grid_spec=grid_spec,
        out_shape=jax.ShapeDtypeStruct((s, HEADS_W), BF16),
        compiler_params=_params(2), name="attn_fox" if fox else "attn_mla",
    )(tab, qpT, qeT, ka, ke, vT)


def _softmax_pv(s, v):
    m = jnp.max(s, axis=-1, keepdims=True)
    p = jnp.exp2(s - m)
    l = jnp.sum(p, axis=-1, keepdims=True)
    return _dot(p.astype(BF16), v) / l


def _diag_heads(o_big, t_new):
    rows = lax.broadcasted_iota(jnp.int32, o_big.shape, 0) // t_new
    cols = lax.broadcasted_iota(jnp.int32, o_big.shape, 1) // HEAD_DV
    kept = jnp.where(rows == cols, o_big, 0.0)
    return jnp.sum(kept.reshape(N_HEADS, t_new, o_big.shape[1]), axis=0)


def _sample_attn_kernel(qn_ref, qr_ref, qf_ref, lrep_ref, cckv_ref, nckv_ref, ckr_ref, nkr_ref,
                        cfk_ref, nfk_ref, cfv_ref, nfv_ref, wukn_ref, wuv_ref, oa_ref, ob_ref):
    past = cckv_ref.shape[2]
    t_new = nckv_ref.shape[0]
    rows = N_HEADS * t_new
    pad = lrep_ref.shape[2] - past - t_new
    keys = past + t_new + pad

    def with_new(cache, new):
        return jnp.concatenate([cache, new, jnp.zeros((pad, new.shape[1]), new.dtype)], axis=0)

    kidx = lax.broadcasted_iota(jnp.int32, (rows, keys), 1)
    qpos = past + lax.broadcasted_iota(jnp.int32, (rows, keys), 0) % t_new
    real = kidx < past + t_new

    ckv_all = with_new(cckv_ref[0, 0], nckv_ref[...]).astype(BF16)
    kn = _dot(ckv_all, wukn_ref[...]).astype(BF16)
    va = _dot(ckv_all, wuv_ref[...]).astype(BF16)
    kr = with_new(ckr_ref[0, 0], nkr_ref[:, 0:ROPE_DIM]).astype(BF16)
    s = _dot_nt(qn_ref[0], kn) + _dot_nt(qr_ref[0], kr)
    s = jnp.where(real & ((kidx // CHUNK) <= (qpos // CHUNK)), s, NEG)
    oa_ref[...] = _diag_heads(_softmax_pv(s, va), t_new).astype(BF16)

    fk = with_new(cfk_ref[0, 0], nfk_ref[...]).astype(BF16)
    fv = with_new(cfv_ref[0, 0], nfv_ref[...]).astype(BF16)
    c = lrep_ref[0]
    shift = 1
    while shift < keys:
        c = c + jnp.where(kidx >= shift, pltpu.roll(c, shift, 1), 0.0)
        shift *= 2
    cq = jnp.sum(jnp.where(kidx == qpos, c, 0.0), axis=-1, keepdims=True)
    s = _dot_nt(qf_ref[0], fk) + (cq - c) * LOG2E
    s = jnp.where(real & (kidx <= qpos), s, NEG)
    ob_ref[...] = _diag_heads(_softmax_pv(s, fv), t_new).astype(BF16)


def _sample_attention(layer, qn_bd, qr, qf_bd, lrep, cckv, nckv, ckr, nkr, cfk, nfk, cfv, nfv, wukn, wuv):
    nbatch = cckv.shape[1]
    t_new = nckv.shape[0] // nbatch
    b3 = lambda a: pl.BlockSpec((1,) + a.shape[1:], lambda b: (b, 0, 0))
    cache = lambda a: pl.BlockSpec((1, 1) + a.shape[2:], lambda b: (layer, b, 0, 0))
    new = lambda a: pl.BlockSpec((t_new, a.shape[1]), lambda b: (b, 0))
    in_specs = [b3(qn_bd), b3(qr), b3(qf_bd), b3(lrep), cache(cckv), new(nckv), cache(ckr), new(nkr),
                cache(cfk), new(nfk), cache(cfv), new(nfv), _full_spec(wukn.shape), _full_spec(wuv.shape)]
    out = jax.ShapeDtypeStruct((nbatch * t_new, HEADS_W), BF16)
    o_spec = pl.BlockSpec((t_new, HEADS_W), lambda b: (b, 0))
    return pl.pallas_call(
        _sample_attn_kernel, grid=(nbatch,), in_specs=in_specs, out_specs=(o_spec, o_spec),
        out_shape=(out, out), compiler_params=_params(1), name="attn_sample",
    )(qn_bd, qr, qf_bd, lrep, cckv, nckv, ckr, nkr, cfk, nfk, cfv, nfv, wukn, wuv)


def _merge_kernel(x_ref, oa_ref, za_ref, ob_ref, zb_ref, ga_ref, gb_ref, woa_ref, wob_ref, wout_ref,
                  fg_ref, xn_ref, *maybe_y_ref):
    a = _dot(oa_ref[...] * za_ref[...], woa_ref[...])
    b = _dot(ob_ref[...] * zb_ref[...], wob_ref[...])
    mix = ga_ref[...].astype(F32) * a + gb_ref[...].astype(F32) * b
    xn = x_ref[...] + _dot(mix.astype(BF16), wout_ref[...])
    xn_ref[...] = xn
    if maybe_y_ref:
        maybe_y_ref[0][...] = _rms(xn, fg_ref[...])


def _merge(x, oa, za, ob, zb, ga, gb, woa, wob, wout, fg, tm, final):
    m = x.shape[0]
    row = lambda c: pl.BlockSpec((tm, c), lambda i: (i, 0))
    in_specs = [row(D_MODEL), row(HEADS_W), row(HEADS_W), row(HEADS_W), row(HEADS_W), row(D_MODEL),
                row(D_MODEL), _full_spec(woa.shape), _full_spec(wob.shape), _full_spec(wout.shape),
                _full_spec(fg.shape)]
    n_out = 2 if final else 1
    out_shape = tuple(jax.ShapeDtypeStruct((m, D_MODEL), F32) for _ in range(n_out))
    out_specs = tuple(row(D_MODEL) for _ in range(n_out))
    return pl.pallas_call(
        _merge_kernel, grid=(m // tm,), in_specs=in_specs, out_specs=out_specs, out_shape=out_shape,
        compiler_params=_params(1), name="merge",
    )(x, oa, za, ob, zb, ga, gb, woa, wob, wout, fg)


def _pad_cols(w, width):
    return jnp.pad(w, ((0, 0), (0, width - w.shape[1])))


def _layer_weights(w_in, w_uq, w_ukv, w_oa, w_ob, w_out, b_f):
    offs = np.cumsum(IN_SIZES)[:-1].tolist()
    w_cq, w_ckv, w_kr, w_za, w_fq, w_fk, w_fv, w_zf, w_zb, w_ga, w_gb = jnp.split(w_in, offs, axis=1)
    w_kr_swapped = jnp.concatenate([w_kr[:, HALF_ROPE:], w_kr[:, :HALF_ROPE]], axis=1)
    w_mla = jnp.concatenate([w_cq, w_ckv, _pad_cols(w_kr, LANES), _pad_cols(w_kr_swapped, LANES),
                             w_za, w_ga], axis=1).astype(BF16)
    w_fox = jnp.concatenate([w_fk, w_fv, _pad_cols(w_zf, LANES), w_zb, w_gb], axis=1).astype(BF16)
    uq = w_uq.reshape(Q_LORA, N_HEADS, NOPE_DIM + ROPE_DIM)
    uq_rows = jnp.concatenate([
        uq[:, :, :NOPE_DIM].reshape(Q_LORA, -1),
        uq[:, :, NOPE_DIM:NOPE_DIM + HALF_ROPE].reshape(Q_LORA, -1),
        uq[:, :, NOPE_DIM + HALF_ROPE:].reshape(Q_LORA, -1)], axis=1)
    ukv = w_ukv.reshape(KV_LORA, N_HEADS, NOPE_DIM + HEAD_DV)
    w_ukn = ukv[:, :, :NOPE_DIM].reshape(KV_LORA, -1).astype(BF16)
    w_uv = ukv[:, :, NOPE_DIM:].reshape(KV_LORA, -1).astype(BF16)
    return dict(
        w_mla=w_mla, w_fox=w_fox, w_uqT=uq_rows.T.astype(BF16), w_ukn=w_ukn, w_uv=w_uv, w_uvT=w_uv.T,
        w_fqT=w_fq.T.astype(BF16), w_fvT=w_fv.T.astype(BF16),
        b_f=_pad_cols(b_f[None, :], LANES),
        w_oa=w_oa.astype(BF16), w_ob=w_ob.astype(BF16), w_out=w_out.astype(BF16))


def _rope_tables(pos):
    inv = jnp.exp(-math.log(ROPE_THETA) * jnp.arange(HALF_ROPE, dtype=F32) / HALF_ROPE)
    ang = pos.astype(F32)[:, None] * inv[None, :]
    cos = jnp.cos(ang)
    sin = jnp.sin(ang)
    cosT = jnp.tile(cos.T, (N_HEADS, 1))
    sinT = jnp.tile(sin.T, (N_HEADS, 1))
    ctok = _pad_cols(jnp.concatenate([cos, cos], axis=1), LANES)
    stok = _pad_cols(jnp.concatenate([-sin, sin], axis=1), LANES)
    return cosT, sinT, ctok, stok


def _bias_placement():
    pk = np.zeros((3 * LANES, LANES), np.float32)
    pqt = np.zeros((N_HEADS * AUG_PER_HEAD, 3 * LANES), np.float32)
    for c in range(3):
        for h in range(N_HEADS):
            pk[c * LANES + h, h * AUG_PER_HEAD + 3 + c] = -1.0
            pqt[h * AUG_PER_HEAD + c, c * LANES + h] = 1.0
    return jnp.asarray(pk, BF16), jnp.asarray(pqt, BF16)


def _head_groups():
    grp = np.zeros((HEADS_W, LANES), np.float32)
    for h in range(N_HEADS):
        grp[h * HEAD_DV:(h + 1) * HEAD_DV, h] = 1.0
    return jnp.asarray(grp, BF16)


def _skip_tables(base, qn2, kn2):
    b = base[:, 0, :N_HEADS]
    qmax = jnp.sqrt(qn2[:, :, 0]) * NORM_SLACK
    kmax = lax.cummax(jnp.sqrt(kn2[:, 0, :N_HEADS]) * NORM_SLACK, axis=0)
    return jnp.stack([b, qmax, kmax]).transpose(0, 2, 1)


def _tri(n):
    return jnp.asarray(np.tril(np.ones((n, n), np.float32)), BF16)


def _block_diag_queries(qT, nbatch, t_new):
    q = qT.T.reshape(nbatch, t_new, N_HEADS, HEAD_DV)
    eye = jnp.eye(N_HEADS, dtype=q.dtype)
    return jnp.einsum('bthj,hg->bhtgj', q, eye).reshape(nbatch, N_HEADS * t_new, HEADS_W)


def _project(x, lw, norm_g, g_q, g_kv, tables, tri, pk, pqt, grp, tm, stack_mla=None, stack_fox=None):
    cosT, sinT, ctok, stok = tables
    g = norm_g[None, :]
    mla = _proj_mla(x, g, lw['w_mla'], g_q[None, :], g_kv[None, :], lw['w_uqT'], lw['w_ukn'], lw['w_uvT'],
                    cosT, sinT, ctok, stok, grp, tm, stack_mla)
    fox = _proj_fox(x, g, lw['w_fox'], lw['w_fqT'], lw['w_fvT'], lw['b_f'], tri, pk, pqt, grp, tm, stack_fox)
    return mla, fox


def kernel(x_prompt, x_sample, cache_mla_ckv, cache_mla_krope, cache_fox_k, cache_fox_v, cache_fox_logf,
           norm_g, w_in, g_q, w_uq, g_kv, w_ukv, b_f, w_oa, w_ob, w_out, final_g):
    depth = w_in.shape[0]
    _, seq, _ = x_prompt.shape
    nbatch, t_new, _ = x_sample.shape
    past = cache_mla_ckv.shape[2]
    m_s = nbatch * t_new
    key_pad = -(past + t_new) % LANES

    tabs_p = _rope_tables(jnp.arange(seq, dtype=jnp.int32))
    tabs_s = _rope_tables(past + jnp.arange(m_s, dtype=jnp.int32) % t_new)
    pk, pqt = _bias_placement()
    grp = _head_groups()
    tri_p = _tri(SEQ_BLOCK)
    tri_s = _tri(m_s)
    fg = final_g[None, :]

    xp = x_prompt.reshape(seq, D_MODEL)
    xs = x_sample.reshape(m_s, D_MODEL)
    outs = {k: [] for k in ('p_kr', 'p_lf', 's_ckv', 's_kr', 's_fk', 's_fv', 's_lf')}
    p_ckv_stack = p_kv_stack = None
    yp = ys = None
    for l in range(depth):
        final = l == depth - 1
        lw = _layer_weights(w_in[l], w_uq[l], w_ukv[l], w_oa[l], w_ob[l], w_out[l], b_f[l])

        mla, fox = _project(xp, lw, norm_g[l], g_q[l], g_kv[l], tabs_p, tri_p, pk, pqt, grp, SEQ_BLOCK,
                            (l, depth, p_ckv_stack), (l, depth, p_kv_stack))
        ckv, kr, za, ga, qnT, qrT, knb, krb, vT, qn2_a, kn2_a = mla
        fk, fv, logf, zb, gb, fqT, fkb, fvT, augk, augqT, base, qn2, kn2 = fox
        tab = _skip_tables(base, qn2, kn2)
        tab_a = _skip_tables(jnp.zeros_like(base), qn2_a, kn2_a)
        o_a = _attention(tab_a, qnT, qrT, knb, krb, vT, fox=False)
        o_b = _attention(tab, fqT, augqT, fkb, augk, fvT, fox=True)
        res = _merge(xp, o_a, za, o_b, zb, ga, gb, lw['w_oa'], lw['w_ob'], lw['w_out'], fg, SEQ_BLOCK, final)
        xp = res[0]
        if final:
            yp = res[1]
        p_ckv_stack, p_kv_stack = (ckv,), (fk, fv)
        outs['p_kr'].append(kr[:, :ROPE_DIM]); outs['p_lf'].append(logf)

        mla, fox = _project(xs, lw, norm_g[l], g_q[l], g_kv[l], tabs_s, tri_s, pk, pqt, grp, m_s)
        ckv, kr, za, ga, qnT, qrT = mla[:6]
        fk, fv, logf, zb, gb, fqT = fox[:6]
        qn_bd = _block_diag_queries(qnT, nbatch, t_new)
        qf_bd = _block_diag_queries(fqT, nbatch, t_new)
        qr = qrT.reshape(N_HEADS, ROPE_DIM, nbatch, t_new).transpose(2, 0, 3, 1).reshape(
            nbatch, N_HEADS * t_new, ROPE_DIM)
        lf_all = jnp.concatenate([cache_fox_logf[l], logf.reshape(nbatch, t_new, N_HEADS),
                                  jnp.zeros((nbatch, key_pad, N_HEADS), F32)], axis=1)
        lrep = jnp.repeat(lf_all.transpose(0, 2, 1), t_new, axis=1)
        o_a, o_b = _sample_attention(
            l, qn_bd, qr, qf_bd, lrep, cache_mla_ckv, ckv, cache_mla_krope, kr,
            cache_fox_k.reshape(depth, nbatch, past, HEADS_W), fk,
            cache_fox_v.reshape(depth, nbatch, past, HEADS_W), fv, lw['w_ukn'], lw['w_uv'])
        res = _merge(xs, o_a, za, o_b, zb, ga, gb, lw['w_oa'], lw['w_ob'], lw['w_out'], fg, m_s, final)
        xs = res[0]
        if final:
            ys = res[1]
        outs['s_ckv'].append(ckv); outs['s_kr'].append(kr[:, :ROPE_DIM]); outs['s_fk'].append(fk)
        outs['s_fv'].append(fv); outs['s_lf'].append(logf)

    st = lambda name, shape: jnp.stack(outs[name]).reshape((depth,) + shape)
    return (yp.reshape(1, seq, D_MODEL), ys.reshape(nbatch, t_new, D_MODEL),
            p_ckv_stack[0].reshape(depth, 1, seq, KV_LORA), st('p_kr', (1, seq, ROPE_DIM)),
            p_kv_stack[0].reshape(depth, 1, seq, N_HEADS, HEAD_DV),
            p_kv_stack[1].reshape(depth, 1, seq, N_HEADS, HEAD_DV),
            st('p_lf', (1, seq, N_HEADS)),
            st('s_ckv', (nbatch, t_new, KV_LORA)), st('s_kr', (nbatch, t_new, ROPE_DIM)),
            st('s_fk', (nbatch, t_new, N_HEADS, HEAD_DV)), st('s_fv', (nbatch, t_new, N_HEADS, HEAD_DV)),
            st('s_lf', (nbatch, t_new, N_HEADS)))
```

```python
import functools
import math

import numpy as np
import jax
import jax.numpy as jnp
from jax import lax
from jax.experimental import pallas as pl
from jax.experimental.pallas import tpu as pltpu

D_MODEL = 1024
N_HEADS = 8
NOPE_DIM = 64
ROPE_DIM = 32
HALF_ROPE = ROPE_DIM // 2
HEAD_DV = 64
Q_LORA = 384
KV_LORA = 256
HEADS_W = N_HEADS * HEAD_DV
CHUNK = 64
ROPE_THETA = 10000.0
LOG2E = math.log2(math.e)
MLA_QSCALE = LOG2E / math.sqrt(NOPE_DIM + ROPE_DIM)
FOX_QSCALE = LOG2E / math.sqrt(HEAD_DV)
EPS = 1e-6
NEG = -1e30
IN_SIZES = (Q_LORA, KV_LORA, ROPE_DIM, HEADS_W, HEADS_W, HEADS_W, HEADS_W, N_HEADS, HEADS_W, D_MODEL, D_MODEL)

LANES = 128
SEQ_BLOCK = 512
KEY_SUB = 256
DEN_ROWS = 16
HEADS_PER_STEP = 8
LOOKAHEAD = 2
SAFE_LOG2 = 100.0
SKIP_LOG2 = 64.0
NORM_SLACK = 1.01
AUG_PER_HEAD = 8
VMEM_LIMIT_BYTES = 56 * 1024 * 1024

F32 = jnp.float32
BF16 = jnp.bfloat16


def _dot(a, b):
    return jnp.dot(a, b, preferred_element_type=F32)


def _dot_nt(a, b):
    return lax.dot_general(a, b, (((1,), (1,)), ((), ())), preferred_element_type=F32)


def _rms(x, g):
    return x * lax.rsqrt(jnp.mean(x * x, axis=-1, keepdims=True) + EPS) * g


def _sigmoid(x):
    return 1.0 / (1.0 + jnp.exp(-x))


def _silu(x):
    return x * _sigmoid(x)


def _split3(x):
    hi = x.astype(BF16)
    r1 = x - hi.astype(F32)
    mid = r1.astype(BF16)
    lo = (r1 - mid.astype(F32)).astype(BF16)
    return hi, mid, lo


def _full_spec(shape):
    nd = len(shape)
    return pl.BlockSpec(shape, lambda *_: (0,) * nd)


def _layer_spec(arr, layer):
    nd = arr.ndim
    return pl.BlockSpec((1,) + arr.shape[1:], lambda *_: (layer,) + (0,) * (nd - 1))


def _put(ref, val):
    ref[...] = val.reshape(ref.shape)


def _skip_refs(body, n_in, n_skip, *refs):
    return body(*refs[:n_in], *refs[n_in + n_skip:])


def _stacked(stack, m, tm, cols):
    if stack is None:
        return jax.ShapeDtypeStruct((m, cols), F32), pl.BlockSpec((tm, cols), lambda i: (i, 0))
    layer, depth, _ = stack
    return (jax.ShapeDtypeStruct((depth, m, cols), F32),
            pl.BlockSpec((1, tm, cols), lambda i: (layer, i, 0)))


def _alias_args(stack, n_in, out_indices):
    if stack is None or stack[2] is None:
        return [], [], {}
    prev = list(stack[2])
    specs = [pl.BlockSpec(memory_space=pl.ANY) for _ in prev]
    return prev, specs, {n_in + k: out for k, out in enumerate(out_indices)}


def _params(n_axes):
    return pltpu.CompilerParams(dimension_semantics=("arbitrary",) * n_axes,
                                vmem_limit_bytes=VMEM_LIMIT_BYTES)


MLA_IN_COLS = Q_LORA + KV_LORA + 2 * LANES + HEADS_W + D_MODEL


def _proj_mla_kernel(x_ref, g_ref, w_ref, gq_ref, gkv_ref, wuq_ref, wukn_ref, wuvt_ref,
                     cosT_ref, sinT_ref, ctok_ref, stok_ref, grp_ref,
                     ckv_ref, kr_ref, za_ref, ga_ref, qnT_ref, qrT_ref, knb_ref, krb_ref, vT_ref,
                     qn2_ref, kn2_ref):
    tm = x_ref.shape[0]
    hb = _rms(x_ref[...], g_ref[0]).astype(BF16)
    o = 0
    zcq = _dot(hb, w_ref[0, :, o:o + Q_LORA]); o += Q_LORA
    zckv = _dot(hb, w_ref[0, :, o:o + KV_LORA]); o += KV_LORA
    zka = _dot(hb, w_ref[0, :, o:o + LANES]); o += LANES
    zkb = _dot(hb, w_ref[0, :, o:o + LANES]); o += LANES
    za_ref[...] = _silu(_dot(hb, w_ref[0, :, o:o + HEADS_W])).astype(BF16); o += HEADS_W
    ga_ref[...] = _sigmoid(_dot(hb, w_ref[0, :, o:o + D_MODEL])).astype(BF16)

    cqb = _rms(zcq, gq_ref[0]).astype(BF16)
    qT = _dot_nt(wuq_ref[0], cqb)
    qn = qT[0:HEADS_W] * MLA_QSCALE
    qnT_ref[...] = qn.astype(BF16)
    x1 = qT[HEADS_W:HEADS_W + LANES]
    x2 = qT[HEADS_W + LANES:HEADS_W + 2 * LANES]
    c = cosT_ref[...]
    s = sinT_ref[...]
    r1 = (x1 * c - x2 * s) * MLA_QSCALE
    r2 = (x1 * s + x2 * c) * MLA_QSCALE
    n1 = r1.astype(BF16)
    n2 = r2.astype(BF16)
    for h in range(N_HEADS):
        qrT_ref[h * ROPE_DIM:h * ROPE_DIM + HALF_ROPE, :] = n1[h * HALF_ROPE:(h + 1) * HALF_ROPE]
        qrT_ref[h * ROPE_DIM + HALF_ROPE:(h + 1) * ROPE_DIM, :] = n2[h * HALF_ROPE:(h + 1) * HALF_ROPE]
    qn2 = (jnp.sum((qn * qn).reshape(N_HEADS, NOPE_DIM, tm), axis=1)
           + jnp.sum((r1 * r1 + r2 * r2).reshape(N_HEADS, HALF_ROPE, tm), axis=1))
    qn2_ref[0] = jnp.broadcast_to(jnp.max(qn2, axis=1, keepdims=True), (N_HEADS, LANES))

    ckv = _rms(zckv, gkv_ref[0])
    _put(ckv_ref, ckv)
    cb = ckv.astype(BF16)
    kn = _dot(cb, wukn_ref[0])
    knb_ref[0] = kn.astype(BF16)
    vT_ref[0] = _dot_nt(wuvt_ref[0], cb).astype(BF16)

    kr = zka * ctok_ref[...] + zkb * stok_ref[...]
    kr_ref[...] = kr
    krb_ref[0] = kr.astype(BF16)
    kn2 = _dot((kn * kn).astype(BF16), grp_ref[...]) + jnp.sum(kr * kr, axis=1, keepdims=True)
    kn2_ref[0] = jnp.max(kn2, axis=0, keepdims=True)


def _proj_mla(layer, x, g, w, gq, gkv, wuq, wukn, wuvt, cosT, sinT, ctok, stok, grp, tm, stack=None):
    m = x.shape[0]
    nb = m // tm
    row = lambda c: pl.BlockSpec((tm, c), lambda i: (i, 0))
    colT = lambda r: pl.BlockSpec((r, tm), lambda i: (0, i))
    blk3 = lambda a, b: pl.BlockSpec((1, a, b), lambda i: (i, 0, 0))
    ckv_shape, ckv_spec = _stacked(stack, m, tm, KV_LORA)
    out_shape = (
        ckv_shape,
        jax.ShapeDtypeStruct((m, LANES), F32),
        jax.ShapeDtypeStruct((m, HEADS_W), BF16),
        jax.ShapeDtypeStruct((m, D_MODEL), BF16),
        jax.ShapeDtypeStruct((HEADS_W, m), BF16),
        jax.ShapeDtypeStruct((N_HEADS * ROPE_DIM, m), BF16),
        jax.ShapeDtypeStruct((nb, tm, HEADS_W), BF16),
        jax.ShapeDtypeStruct((nb, tm, LANES), BF16),
        jax.ShapeDtypeStruct((nb, HEADS_W, tm), BF16),
        jax.ShapeDtypeStruct((nb, N_HEADS, LANES), F32),
        jax.ShapeDtypeStruct((nb, 1, LANES), F32),
    )
    out_specs = (ckv_spec, row(LANES), row(HEADS_W), row(D_MODEL), colT(HEADS_W),
                 colT(N_HEADS * ROPE_DIM), blk3(tm, HEADS_W), blk3(tm, LANES), blk3(HEADS_W, tm),
                 blk3(N_HEADS, LANES), blk3(1, LANES))
    in_specs = [row(D_MODEL)] + [_layer_spec(a, layer) for a in (g, w, gq, gkv, wuq, wukn, wuvt)] + [
        colT(LANES), colT(LANES), row(LANES), row(LANES), _full_spec(grp.shape)]
    prev, prev_specs, aliases = _alias_args(stack, len(in_specs), (0,))
    return pl.pallas_call(
        functools.partial(_skip_refs, _proj_mla_kernel, len(in_specs), len(prev)),
        grid=(nb,), in_specs=in_specs + prev_specs, out_specs=out_specs, out_shape=out_shape,
        input_output_aliases=aliases, compiler_params=_params(1), name="proj_mla",
    )(x, g, w, gq, gkv, wuq, wukn, wuvt, cosT, sinT, ctok, stok, grp, *prev)


FOX_IN_COLS = 2 * HEADS_W + LANES + HEADS_W + D_MODEL


def _proj_fox_kernel(x_ref, g_ref, w_ref, wfqt_ref, wfvt_ref, bf_ref, tri_ref, pk_ref, pqt_ref, grp_ref,
                     fk_ref, fv_ref, logf_ref, zb_ref, gb_ref, fqT_ref, fkb_ref, fvT_ref,
                     augk_ref, augqT_ref, base_ref, qn2_ref, kn2_ref, carry_ref):
    i = pl.program_id(0)
    tm = x_ref.shape[0]
    hb = _rms(x_ref[...], g_ref[0]).astype(BF16)
    o = 0
    fk = _dot(hb, w_ref[0, :, o:o + HEADS_W]); o += HEADS_W
    _put(fk_ref, fk)
    fkb_ref[0] = fk.astype(BF16)
    kn2_ref[0] = jnp.max(_dot((fk * fk).astype(BF16), grp_ref[...]), axis=0, keepdims=True)
    _put(fv_ref, _dot(hb, w_ref[0, :, o:o + HEADS_W])); o += HEADS_W
    zf = _dot(hb, w_ref[0, :, o:o + LANES]) + bf_ref[0]; o += LANES
    zb_ref[...] = _silu(_dot(hb, w_ref[0, :, o:o + HEADS_W])).astype(BF16); o += HEADS_W
    gb_ref[...] = _sigmoid(_dot(hb, w_ref[0, :, o:o + D_MODEL])).astype(BF16)
    fqT = _dot_nt(wfqt_ref[0], hb) * FOX_QSCALE
    fqT_ref[...] = fqT.astype(BF16)
    qn2 = jnp.sum((fqT * fqT).reshape(N_HEADS, HEAD_DV, tm), axis=1)
    qn2_ref[0] = jnp.broadcast_to(jnp.max(qn2, axis=1, keepdims=True), (N_HEADS, LANES))
    fvT_ref[0] = _dot_nt(wfvt_ref[0], hb).astype(BF16)

    lane = lax.broadcasted_iota(jnp.int32, (tm, LANES), 1)
    logf = jnp.minimum(zf, 0.0) - jnp.log(1.0 + jnp.exp(-jnp.abs(zf)))
    logf = jnp.where(lane < N_HEADS, logf, 0.0)
    logf_ref[...] = logf[:, 0:N_HEADS]

    hi, mid, lo = _split3(logf)
    tri = tri_ref[...]
    r = _dot(tri, hi) + _dot(tri, mid) + _dot(tri, lo)

    @pl.when(i == 0)
    def _():
        carry_ref[...] = jnp.zeros_like(carry_ref)

    base_ref[0] = carry_ref[...] * LOG2E
    carry_ref[...] = carry_ref[...] + r[tm - 1:tm, :]

    rh, rm, rl = _split3(r * LOG2E)
    rcat = jnp.concatenate([rh, rm, rl], axis=1)
    slot = lane % AUG_PER_HEAD
    ones_k = jnp.where((lane < N_HEADS * AUG_PER_HEAD) & (slot < 3), 1.0, 0.0)
    augk_ref[0] = (_dot(rcat, pk_ref[...]) + ones_k).astype(BF16)
    rowq = lax.broadcasted_iota(jnp.int32, (N_HEADS * AUG_PER_HEAD, tm), 0) % AUG_PER_HEAD
    ones_q = jnp.where((rowq >= 3) & (rowq < 6), 1.0, 0.0)
    augqT_ref[...] = (_dot_nt(pqt_ref[...], rcat) + ones_q).astype(BF16)


def _proj_fox(layer, x, g, w, wfqt, wfvt, bf, tri, pk, pqt, grp, tm, stack=None):
    m = x.shape[0]
    nb = m // tm
    row = lambda c: pl.BlockSpec((tm, c), lambda i: (i, 0))
    colT = lambda r: pl.BlockSpec((r, tm), lambda i: (0, i))
    blk3 = lambda a, b: pl.BlockSpec((1, a, b), lambda i: (i, 0, 0))
    kv_shape, kv_spec = _stacked(stack, m, tm, HEADS_W)
    out_shape = (
        kv_shape,
        kv_shape,
        jax.ShapeDtypeStruct((m, N_HEADS), F32),
        jax.ShapeDtypeStruct((m, HEADS_W), BF16),
        jax.ShapeDtypeStruct((m, D_MODEL), BF16),
        jax.ShapeDtypeStruct((HEADS_W, m), BF16),
        jax.ShapeDtypeStruct((nb, tm, HEADS_W), BF16),
        jax.ShapeDtypeStruct((nb, HEADS_W, tm), BF16),
        jax.ShapeDtypeStruct((nb, tm, LANES), BF16),
        jax.ShapeDtypeStruct((N_HEADS * AUG_PER_HEAD, m), BF16),
        jax.ShapeDtypeStruct((nb, 1, LANES), F32),
        jax.ShapeDtypeStruct((nb, N_HEADS, LANES), F32),
        jax.ShapeDtypeStruct((nb, 1, LANES), F32),
    )
    out_specs = (kv_spec, kv_spec, row(N_HEADS), row(HEADS_W), row(D_MODEL), colT(HEADS_W),
                 blk3(tm, HEADS_W), blk3(HEADS_W, tm), blk3(tm, LANES), colT(N_HEADS * AUG_PER_HEAD),
                 blk3(1, LANES), blk3(N_HEADS, LANES), blk3(1, LANES))
    in_specs = ([row(D_MODEL)] + [_layer_spec(a, layer) for a in (g, w, wfqt, wfvt, bf)]
                + [_full_spec(a.shape) for a in (tri, pk, pqt, grp)])
    prev, prev_specs, aliases = _alias_args(stack, len(in_specs), (0, 1))
    return pl.pallas_call(
        functools.partial(_skip_refs, _proj_fox_kernel, len(in_specs), len(prev)),
        grid=(nb,), in_specs=in_specs + prev_specs, out_specs=out_specs, out_shape=out_shape,
        scratch_shapes=[pltpu.VMEM((1, LANES), F32)],
        input_output_aliases=aliases, compiler_params=_params(1), name="proj_fox",
    )(x, g, w, wfqt, wfvt, bf, tri, pk, pqt, grp, *prev)


def _attn_kernel(tab_ref, qp_ref, qe_ref, ka_ref, ke_ref, vT_ref, o_ref, acc_ref, s_ref, *, fox):
    g = pl.program_id(0)
    i = pl.program_id(1)
    tq = qp_ref.shape[1]
    tk = ka_ref.shape[1]
    rows_p = lax.broadcasted_iota(jnp.int32, (LANES, tq), 0)
    rows_e = lax.broadcasted_iota(jnp.int32, qe_ref.shape, 0)
    kpos = lax.broadcasted_iota(jnp.int32, (tk, tq), 0)
    qpos = lax.broadcasted_iota(jnp.int32, (tk, tq), 1)
    if fox:
        visible = kpos <= qpos
    else:
        visible = (kpos // CHUNK) <= (qpos // CHUNK)
    qe = qe_ref[...]

    ws = []
    for hq in range(HEADS_PER_STEP):
        pair, hh = divmod(hq, 2)
        qp = qp_ref[pair * LANES:(pair + 1) * LANES, :]
        keep_p = (rows_p >= hh * HEAD_DV) & (rows_p < (hh + 1) * HEAD_DV)
        if fox:
            h = HEADS_PER_STEP * g + hq
            keep_e = (rows_e >= h * AUG_PER_HEAD) & (rows_e < (h + 1) * AUG_PER_HEAD)
            extra = jnp.where(keep_e, qe, jnp.zeros_like(qe))
        else:
            extra = qe[hq * ROPE_DIM:(hq + 1) * ROPE_DIM]
        zero_rows = jnp.zeros((LANES - extra.shape[0], tq), BF16)
        ws.append(jnp.concatenate([jnp.where(keep_p, qp, jnp.zeros_like(qp)), extra, zero_rows], axis=0))

    steps = [(u, hq) for u in range(tk // KEY_SUB) for hq in range(HEADS_PER_STEP)]
    ones_rows = jnp.ones((DEN_ROWS, KEY_SUB), BF16)

    def scores(j, step):
        u, hq = step
        pair = hq // 2
        rows = slice(u * KEY_SUB, (u + 1) * KEY_SUB)
        lhs = jnp.concatenate([ka_ref[j, rows, pair * LANES:(pair + 1) * LANES], ke_ref[j, rows, :]], axis=1)
        return _dot(lhs, ws[hq])

    def block(j, stats, masked, j_next):
        stats = list(stats)
        n_steps = len(steps)
        tiles = {}
        for n, (u, hq) in enumerate(steps):
            ahead = n + LOOKAHEAD
            if ahead < n_steps:
                tiles[ahead] = scores(j, steps[ahead])
            elif j_next is not None:
                tiles[ahead] = scores(j_next, steps[ahead - n_steps])
            sT = tiles.pop(n) if n in tiles else s_ref[n]
            rows = slice(u * KEY_SUB, (u + 1) * KEY_SUB)
            m_old = stats[hq]
            if masked:
                sT = jnp.where(visible[rows], sT, NEG)
            if fox:
                h = HEADS_PER_STEP * g + hq
                d = tab_ref[0, h, i] - tab_ref[0, h, j]
            else:
                d = 0.0
            m_new = jnp.maximum(m_old, jnp.max(sT, axis=0, keepdims=True) + d)
            alpha = jnp.exp2(m_old - m_new)
            pT = jnp.exp2((sT - (m_new - d)).astype(BF16))
            stats[hq] = m_new
            v = jnp.concatenate([vT_ref[j, hq * HEAD_DV:(hq + 1) * HEAD_DV, rows], ones_rows], axis=0)
            acc_ref[hq] = alpha * acc_ref[hq] + _dot(v, pT)
        for n, tile in tiles.items():
            s_ref[n - n_steps] = tile
        return tuple(stats)

    def more(j_top, stats):
        alive = j_top >= 0
        if fox:
            jt = jnp.maximum(j_top, 0)
            slack = None
            for hq in range(HEADS_PER_STEP):
                h = HEADS_PER_STEP * g + hq
                bound = tab_ref[1, h, i] * tab_ref[2, h, jt] + (tab_ref[0, h, i] - tab_ref[0, h, jt + 1])
                room = stats[hq] - bound
                slack = room if slack is None else jnp.minimum(slack, room)
            alive = alive & (jnp.min(slack) < SKIP_LOG2)
        return alive.astype(jnp.int32)

    def block_unshifted(j, masked, j_next):
        n_steps = len(steps)
        tiles = {}
        for n, (u, hq) in enumerate(steps):
            ahead = n + LOOKAHEAD
            if ahead < n_steps:
                tiles[ahead] = scores(j, steps[ahead])
            else:
                tiles[ahead] = scores(j_next, steps[ahead - n_steps])
            sT = tiles.pop(n) if n in tiles else s_ref[n]
            rows = slice(u * KEY_SUB, (u + 1) * KEY_SUB)
            if masked:
                sT = jnp.where(visible[rows], sT, NEG)
            pT = jnp.exp2(sT).astype(BF16)
            v = jnp.concatenate([vT_ref[j, hq * HEAD_DV:(hq + 1) * HEAD_DV, rows], ones_rows], axis=0)
            acc_ref[hq] = acc_ref[hq] + _dot(v, pT)
        for n, tile in tiles.items():
            s_ref[n - n_steps] = tile

    def walk_online():
        m0 = jnp.full((1, tq), NEG, F32)
        stats = block(i, (m0,) * HEADS_PER_STEP, True, jnp.maximum(i - 1, 0))

        def visit(c):
            j = c[0]
            st = block(j, c[2:], False, jnp.maximum(j - 1, 0))
            return (j - 1, more(j - 1, st)) + tuple(st)

        lax.while_loop(lambda c: c[1] > 0, visit, (i - 1, more(i - 1, stats)) + tuple(stats))

    def walk_unshifted():
        block_unshifted(i, True, jnp.maximum(i - 1, 0))

        def visit(k, c):
            j = i - 1 - k
            block_unshifted(j, False, jnp.maximum(j - 1, 0))
            return c

        lax.fori_loop(0, i, visit, 0)

    acc_ref[...] = jnp.zeros_like(acc_ref)
    for n in range(LOOKAHEAD):
        s_ref[n] = scores(i, steps[n])
    if fox:
        walk_online()
    else:
        bounded = None
        for hq in range(HEADS_PER_STEP):
            h = HEADS_PER_STEP * g + hq
            ok = tab_ref[1, h, i] * tab_ref[2, h, i] <= SAFE_LOG2
            bounded = ok if bounded is None else bounded & ok
        pl.when(bounded)(walk_unshifted)
        pl.when(jnp.logical_not(bounded))(walk_online)
    outs = [acc_ref[hq, 0:HEAD_DV, :] / acc_ref[hq, HEAD_DV:HEAD_DV + 1, :] for hq in range(HEADS_PER_STEP)]
    o_ref[...] = jnp.concatenate(outs, axis=0).T.astype(BF16)


def _attention(tab, qpT, qeT, ka, ke, vT, fox):
    nb, tk, _ = ka.shape
    s = qpT.shape[1]
    tq = tk
    width = HEADS_PER_STEP * HEAD_DV
    e_rows = qeT.shape[0] if fox else HEADS_PER_STEP * ROPE_DIM
    e_map = (lambda g, i, b: (0, i)) if fox else (lambda g, i, b: (g, i))
    once = dict(pipeline_mode=pl.Buffered(1))
    grid_spec = pltpu.PrefetchScalarGridSpec(
        num_scalar_prefetch=1,
        grid=(N_HEADS // HEADS_PER_STEP, nb),
        in_specs=[
            pl.BlockSpec((width, tq), lambda g, i, b: (g, i)),
            pl.BlockSpec((e_rows, tq), e_map),
            pl.BlockSpec((nb, tk, width), lambda g, i, b: (0, 0, g), **once),
            pl.BlockSpec((nb, tk, LANES), lambda g, i, b: (0, 0, 0), **once),
            pl.BlockSpec((nb, width, tk), lambda g, i, b: (0, g, 0), **once),
        ],
        out_specs=pl.BlockSpec((tq, width), lambda g, i, b: (i, g)),
        scratch_shapes=[pltpu.VMEM((HEADS_PER_STEP, HEAD_DV + DEN_ROWS, tq), F32),
                        pltpu.VMEM((LOOKAHEAD, KEY_SUB, tq), F32)],
    )
    return pl.pallas_call(
        functools.partial(_attn_kernel, fox=fox), grid_spec=grid_spec,
        out_shape=jax.ShapeDtypeStruct((s, HEADS_W), BF16),
        compiler_params=_params(2), name="attn_fox" if fox else "attn_mla",
    )(tab, qpT, qeT, ka, ke, vT)


def _softmax_pv(s, v):
    m = jnp.max(s, axis=-1, keepdims=True)
    p = jnp.exp2(s - m)
    l = jnp.sum(p, axis=-1, keepdims=True)
    return _dot(p.astype(BF16), v) / l


def _diag_heads(o_big, t_new):
    rows = lax.broadcasted_iota(jnp.int32, o_big.shape, 0) // t_new
    cols = lax.broadcasted_iota(jnp.int32, o_big.shape, 1) // HEAD_DV
    kept = jnp.where(rows == cols, o_big, 0.0)
    return jnp.sum(kept.reshape(N_HEADS, t_new, o_big.shape[1]), axis=0)


def _sample_attn_kernel(qn_ref, qr_ref, qf_ref, lrep_ref, cckv_ref, nckv_ref, ckr_ref, nkr_ref,
                        cfk_ref, nfk_ref, cfv_ref, nfv_ref, wukn_ref, wuv_ref, oa_ref, ob_ref):
    past = cckv_ref.shape[2]
    t_new = nckv_ref.shape[0]
    rows = N_HEADS * t_new
    pad = lrep_ref.shape[2] - past - t_new
    keys = past + t_new + pad

    def with_new(cache, new):
        return jnp.concatenate([cache, new, jnp.zeros((pad, new.shape[1]), new.dtype)], axis=0)

    kidx = lax.broadcasted_iota(jnp.int32, (rows, keys), 1)
    qpos = past + lax.broadcasted_iota(jnp.int32, (rows, keys), 0) % t_new
    real = kidx < past + t_new

    ckv_all = with_new(cckv_ref[0, 0], nckv_ref[...]).astype(BF16)
    kn = _dot(ckv_all, wukn_ref[0]).astype(BF16)
    va = _dot(ckv_all, wuv_ref[0]).astype(BF16)
    kr = with_new(ckr_ref[0, 0], nkr_ref[:, 0:ROPE_DIM]).astype(BF16)
    s = _dot_nt(qn_ref[0], kn) + _dot_nt(qr_ref[0], kr)
    s = jnp.where(real & ((kidx // CHUNK) <= (qpos // CHUNK)), s, NEG)
    oa_ref[...] = _diag_heads(_softmax_pv(s, va), t_new).astype(BF16)

    fk = with_new(cfk_ref[0, 0], nfk_ref[...]).astype(BF16)
    fv = with_new(cfv_ref[0, 0], nfv_ref[...]).astype(BF16)
    c = lrep_ref[0]
    shift = 1
    while shift < keys:
        c = c + jnp.where(kidx >= shift, pltpu.roll(c, shift, 1), 0.0)
        shift *= 2
    cq = jnp.sum(jnp.where(kidx == qpos, c, 0.0), axis=-1, keepdims=True)
    s = _dot_nt(qf_ref[0], fk) + (cq - c) * LOG2E
    s = jnp.where(real & (kidx <= qpos), s, NEG)
    ob_ref[...] = _diag_heads(_softmax_pv(s, fv), t_new).astype(BF16)


def _sample_attention(layer, qn_bd, qr, qf_bd, lrep, cckv, nckv, ckr, nkr, cfk, nfk, cfv, nfv, wukn, wuv):
    nbatch = cckv.shape[1]
    t_new = nckv.shape[0] // nbatch
    b3 = lambda a: pl.BlockSpec((1,) + a.shape[1:], lambda b: (b, 0, 0))
    cache = lambda a: pl.BlockSpec((1, 1) + a.shape[2:], lambda b: (layer, b, 0, 0))
    new = lambda a: pl.BlockSpec((t_new, a.shape[1]), lambda b: (b, 0))
    in_specs = [b3(qn_bd), b3(qr), b3(qf_bd), b3(lrep), cache(cckv), new(nckv), cache(ckr), new(nkr),
                cache(cfk), new(nfk), cache(cfv), new(nfv), _layer_spec(wukn, layer), _layer_spec(wuv, layer)]
    out = jax.ShapeDtypeStruct((nbatch * t_new, HEADS_W), BF16)
    o_spec = pl.BlockSpec((t_new, HEADS_W), lambda b: (b, 0))
    return pl.pallas_call(
        _sample_attn_kernel, grid=(nbatch,), in_specs=in_specs, out_specs=(o_spec, o_spec),
        out_shape=(out, out), compiler_params=_params(1), name="attn_sample",
    )(qn_bd, qr, qf_bd, lrep, cckv, nckv, ckr, nkr, cfk, nfk, cfv, nfv, wukn, wuv)


def _merge_kernel(x_ref, oa_ref, za_ref, ob_ref, zb_ref, ga_ref, gb_ref, woa_ref, wob_ref, wout_ref,
                  fg_ref, xn_ref, *maybe_y_ref):
    a = _dot(oa_ref[...] * za_ref[...], woa_ref[0])
    b = _dot(ob_ref[...] * zb_ref[...], wob_ref[0])
    mix = ga_ref[...].astype(F32) * a + gb_ref[...].astype(F32) * b
    xn = x_ref[...] + _dot(mix.astype(BF16), wout_ref[0])
    xn_ref[...] = xn
    if maybe_y_ref:
        maybe_y_ref[0][...] = _rms(xn, fg_ref[...])


def _merge(layer, x, oa, za, ob, zb, ga, gb, woa, wob, wout, fg, tm, final):
    m = x.shape[0]
    row = lambda c: pl.BlockSpec((tm, c), lambda i: (i, 0))
    in_specs = [row(D_MODEL), row(HEADS_W), row(HEADS_W), row(HEADS_W), row(HEADS_W), row(D_MODEL),
                row(D_MODEL), _layer_spec(woa, layer), _layer_spec(wob, layer), _layer_spec(wout, layer),
                _full_spec(fg.shape)]
    n_out = 2 if final else 1
    out_shape = tuple(jax.ShapeDtypeStruct((m, D_MODEL), F32) for _ in range(n_out))
    out_specs = tuple(row(D_MODEL) for _ in range(n_out))
    return pl.pallas_call(
        _merge_kernel, grid=(m // tm,), in_specs=in_specs, out_specs=out_specs, out_shape=out_shape,
        compiler_params=_params(1), name="merge",
    )(x, oa, za, ob, zb, ga, gb, woa, wob, wout, fg)


def _pad_cols(w, width):
    return jnp.pad(w, ((0, 0), (0, width - w.shape[1])))


def _pad_last(w, width):
    return jnp.pad(w, [(0, 0)] * (w.ndim - 1) + [(0, width - w.shape[-1])])


def _stacked_weights(w_in, w_uq, w_ukv, w_oa, w_ob, w_out, b_f, norm_g, g_q, g_kv):
    depth = w_in.shape[0]
    offs = np.cumsum(IN_SIZES)[:-1].tolist()
    w_cq, w_ckv, w_kr, w_za, w_fq, w_fk, w_fv, w_zf, w_zb, w_ga, w_gb = jnp.split(w_in, offs, axis=2)
    w_kr_swapped = jnp.concatenate([w_kr[..., HALF_ROPE:], w_kr[..., :HALF_ROPE]], axis=2)
    w_mla = jnp.concatenate([w_cq, w_ckv, _pad_last(w_kr, LANES), _pad_last(w_kr_swapped, LANES),
                             w_za, w_ga], axis=2).astype(BF16)
    w_fox = jnp.concatenate([w_fk, w_fv, _pad_last(w_zf, LANES), w_zb, w_gb], axis=2).astype(BF16)
    uq = w_uq.reshape(depth, Q_LORA, N_HEADS, NOPE_DIM + ROPE_DIM)
    uq_rows = jnp.concatenate([
        uq[..., :NOPE_DIM].reshape(depth, Q_LORA, -1),
        uq[..., NOPE_DIM:NOPE_DIM + HALF_ROPE].reshape(depth, Q_LORA, -1),
        uq[..., NOPE_DIM + HALF_ROPE:].reshape(depth, Q_LORA, -1)], axis=2)
    ukv = w_ukv.reshape(depth, KV_LORA, N_HEADS, NOPE_DIM + HEAD_DV)
    w_ukn = ukv[..., :NOPE_DIM].reshape(depth, KV_LORA, -1).astype(BF16)
    w_uv = ukv[..., NOPE_DIM:].reshape(depth, KV_LORA, -1).astype(BF16)
    tr = lambda a: jnp.swapaxes(a, 1, 2)
    return dict(
        w_mla=w_mla, w_fox=w_fox, w_uqT=tr(uq_rows).astype(BF16), w_ukn=w_ukn, w_uv=w_uv, w_uvT=tr(w_uv),
        w_fqT=tr(w_fq).astype(BF16), w_fvT=tr(w_fv).astype(BF16),
        b_f=_pad_last(b_f[:, None, :], LANES),
        w_oa=w_oa.astype(BF16), w_ob=w_ob.astype(BF16), w_out=w_out.astype(BF16),
        norm_g=norm_g[:, None, :], g_q=g_q[:, None, :], g_kv=g_kv[:, None, :])


def _rope_tables(pos):
    inv = jnp.exp(-math.log(ROPE_THETA) * jnp.arange(HALF_ROPE, dtype=F32) / HALF_ROPE)
    ang = pos.astype(F32)[:, None] * inv[None, :]
    cos = jnp.cos(ang)
    sin = jnp.sin(ang)
    cosT = jnp.tile(cos.T, (N_HEADS, 1))
    sinT = jnp.tile(sin.T, (N_HEADS, 1))
    ctok = _pad_cols(jnp.concatenate([cos, cos], axis=1), LANES)
    stok = _pad_cols(jnp.concatenate([-sin, sin], axis=1), LANES)
    return cosT, sinT, ctok, stok


def _bias_placement():
    pk = np.zeros((3 * LANES, LANES), np.float32)
    pqt = np.zeros((N_HEADS * AUG_PER_HEAD, 3 * LANES), np.float32)
    for c in range(3):
        for h in range(N_HEADS):
            pk[c * LANES + h, h * AUG_PER_HEAD + 3 + c] = -1.0
            pqt[h * AUG_PER_HEAD + c, c * LANES + h] = 1.0
    return jnp.asarray(pk, BF16), jnp.asarray(pqt, BF16)


def _head_groups():
    grp = np.zeros((HEADS_W, LANES), np.float32)
    for h in range(N_HEADS):
        grp[h * HEAD_DV:(h + 1) * HEAD_DV, h] = 1.0
    return jnp.asarray(grp, BF16)


def _skip_tables(base, qn2, kn2):
    b = base[:, 0, :N_HEADS]
    qmax = jnp.sqrt(qn2[:, :, 0]) * NORM_SLACK
    kmax = lax.cummax(jnp.sqrt(kn2[:, 0, :N_HEADS]) * NORM_SLACK, axis=0)
    return jnp.stack([b, qmax, kmax]).transpose(0, 2, 1)


def _tri(n):
    return jnp.asarray(np.tril(np.ones((n, n), np.float32)), BF16)


def _block_diag_queries(qT, nbatch, t_new):
    q = qT.T.reshape(nbatch, t_new, N_HEADS, HEAD_DV)
    eye = jnp.eye(N_HEADS, dtype=q.dtype)
    return jnp.einsum('bthj,hg->bhtgj', q, eye).reshape(nbatch, N_HEADS * t_new, HEADS_W)


def _project(layer, x, sw, tables, tri, pk, pqt, grp, tm, stack_mla=None, stack_fox=None):
    cosT, sinT, ctok, stok = tables
    mla = _proj_mla(layer, x, sw['norm_g'], sw['w_mla'], sw['g_q'], sw['g_kv'], sw['w_uqT'], sw['w_ukn'],
                    sw['w_uvT'], cosT, sinT, ctok, stok, grp, tm, stack_mla)
    fox = _proj_fox(layer, x, sw['norm_g'], sw['w_fox'], sw['w_fqT'], sw['w_fvT'], sw['b_f'], tri, pk, pqt, grp,
                    tm, stack_fox)
    return mla, fox


def kernel(x_prompt, x_sample, cache_mla_ckv, cache_mla_krope, cache_fox_k, cache_fox_v, cache_fox_logf,
           norm_g, w_in, g_q, w_uq, g_kv, w_ukv, b_f, w_oa, w_ob, w_out, final_g):
    depth = w_in.shape[0]
    _, seq, _ = x_prompt.shape
    nbatch, t_new, _ = x_sample.shape
    past = cache_mla_ckv.shape[2]
    m_s = nbatch * t_new
    key_pad = -(past + t_new) % LANES

    tabs_p = _rope_tables(jnp.arange(seq, dtype=jnp.int32))
    tabs_s = _rope_tables(past + jnp.arange(m_s, dtype=jnp.int32) % t_new)
    pk, pqt = _bias_placement()
    grp = _head_groups()
    tri_p = _tri(SEQ_BLOCK)
    tri_s = _tri(m_s)
    fg = final_g[None, :]

    xp = x_prompt.reshape(seq, D_MODEL)
    xs = x_sample.reshape(m_s, D_MODEL)
    outs = {k: [] for k in ('p_kr', 'p_lf', 's_ckv', 's_kr', 's_fk', 's_fv', 's_lf')}
    p_ckv_stack = p_kv_stack = None
    yp = ys = None
    sw = _stacked_weights(w_in, w_uq, w_ukv, w_oa, w_ob, w_out, b_f, norm_g, g_q, g_kv)
    for l in range(depth):
        final = l == depth - 1

        mla, fox = _project(l, xp, sw, tabs_p, tri_p, pk, pqt, grp, SEQ_BLOCK,
                            (l, depth, p_ckv_stack), (l, depth, p_kv_stack))
        ckv, kr, za, ga, qnT, qrT, knb, krb, vT, qn2_a, kn2_a = mla
        fk, fv, logf, zb, gb, fqT, fkb, fvT, augk, augqT, base, qn2, kn2 = fox
        tab = _skip_tables(base, qn2, kn2)
        tab_a = _skip_tables(jnp.zeros_like(base), qn2_a, kn2_a)
        o_a = _attention(tab_a, qnT, qrT, knb, krb, vT, fox=False)
        o_b = _attention(tab, fqT, augqT, fkb, augk, fvT, fox=True)
        res = _merge(l, xp, o_a, za, o_b, zb, ga, gb, sw['w_oa'], sw['w_ob'], sw['w_out'], fg, SEQ_BLOCK, final)
        xp = res[0]
        if final:
            yp = res[1]
        p_ckv_stack, p_kv_stack = (ckv,), (fk, fv)
        outs['p_kr'].append(kr[:, :ROPE_DIM]); outs['p_lf'].append(logf)

        mla, fox = _project(l, xs, sw, tabs_s, tri_s, pk, pqt, grp, m_s)
        ckv, kr, za, ga, qnT, qrT = mla[:6]
        fk, fv, logf, zb, gb, fqT = fox[:6]
        qn_bd = _block_diag_queries(qnT, nbatch, t_new)
        qf_bd = _block_diag_queries(fqT, nbatch, t_new)
        qr = qrT.reshape(N_HEADS, ROPE_DIM, nbatch, t_new).transpose(2, 0, 3, 1).reshape(
            nbatch, N_HEADS * t_new, ROPE_DIM)
        lf_all = jnp.concatenate([cache_fox_logf[l], logf.reshape(nbatch, t_new, N_HEADS),
                                  jnp.zeros((nbatch, key_pad, N_HEADS), F32)], axis=1)
        lrep = jnp.repeat(lf_all.transpose(0, 2, 1), t_new, axis=1)
        o_a, o_b = _sample_attention(
            l, qn_bd, qr, qf_bd, lrep, cache_mla_ckv, ckv, cache_mla_krope, kr,
            cache_fox_k.reshape(depth, nbatch, past, HEADS_W), fk,
            cache_fox_v.reshape(depth, nbatch, past, HEADS_W), fv, sw['w_ukn'], sw['w_uv'])
        res = _merge(l, xs, o_a, za, o_b, zb, ga, gb, sw['w_oa'], sw['w_ob'], sw['w_out'], fg, m_s, final)
        xs = res[0]
        if final:
            ys = res[1]
        outs['s_ckv'].append(ckv); outs['s_kr'].append(kr[:, :ROPE_DIM]); outs['s_fk'].append(fk)
        outs['s_fv'].append(fv); outs['s_lf'].append(logf)

    st = lambda name, shape: jnp.stack(outs[name]).reshape((depth,) + shape)
    return (yp.reshape(1, seq, D_MODEL), ys.reshape(nbatch, t_new, D_MODEL),
            p_ckv_stack[0].reshape(depth, 1, seq, KV_LORA), st('p_kr', (1, seq, ROPE_DIM)),
            p_kv_stack[0].reshape(depth, 1, seq, N_HEADS, HEAD_DV),
            p_kv_stack[1].reshape(depth, 1, seq, N_HEADS, HEAD_DV),
            st('p_lf', (1, seq, N_HEADS)),
            st('s_ckv', (nbatch, t_new, KV_LORA)), st('s_kr', (nbatch, t_new, ROPE_DIM)),
            st('s_fk', (nbatch, t_new, N_HEADS, HEAD_DV)), st('s_fv', (nbatch, t_new, N_HEADS, HEAD_DV)),
            st('s_lf', (nbatch, t_new, N_HEADS)))
```

```python
import functools
import math

import numpy as np
import jax
import jax.numpy as jnp
from jax import lax
from jax.experimental import pallas as pl
from jax.experimental.pallas import tpu as pltpu

D_MODEL = 1024
N_HEADS = 8
NOPE_DIM = 64
ROPE_DIM = 32
HALF_ROPE = ROPE_DIM // 2
HEAD_DV = 64
Q_LORA = 384
KV_LORA = 256
HEADS_W = N_HEADS * HEAD_DV
CHUNK = 64
ROPE_THETA = 10000.0
LOG2E = math.log2(math.e)
MLA_QSCALE = LOG2E / math.sqrt(NOPE_DIM + ROPE_DIM)
FOX_QSCALE = LOG2E / math.sqrt(HEAD_DV)
EPS = 1e-6
NEG = -1e30
IN_SIZES = (Q_LORA, KV_LORA, ROPE_DIM, HEADS_W, HEADS_W, HEADS_W, HEADS_W, N_HEADS, HEADS_W, D_MODEL, D_MODEL)

LANES = 128
SEQ_BLOCK = 512
KEY_SUB = 256
DEN_ROWS = 16
HEADS_PER_STEP = 8
LOOKAHEAD = 2
SAFE_LOG2 = 100.0
SKIP_LOG2 = 64.0
NORM_SLACK = 1.01
AUG_PER_HEAD = 16
VMEM_LIMIT_BYTES = 56 * 1024 * 1024

F32 = jnp.float32
BF16 = jnp.bfloat16


def _dot(a, b):
    return jnp.dot(a, b, preferred_element_type=F32)


def _dot_nt(a, b):
    return lax.dot_general(a, b, (((1,), (1,)), ((), ())), preferred_element_type=F32)


def _rms(x, g):
    return x * lax.rsqrt(jnp.mean(x * x, axis=-1, keepdims=True) + EPS) * g


def _sigmoid(x):
    return 1.0 / (1.0 + jnp.exp(-x))


def _silu(x):
    return x * _sigmoid(x)


def _split3(x):
    hi = x.astype(BF16)
    r1 = x - hi.astype(F32)
    mid = r1.astype(BF16)
    lo = (r1 - mid.astype(F32)).astype(BF16)
    return hi, mid, lo


def _full_spec(shape):
    nd = len(shape)
    return pl.BlockSpec(shape, lambda *_: (0,) * nd)


def _layer_spec(arr, layer):
    nd = arr.ndim
    return pl.BlockSpec((1,) + arr.shape[1:], lambda *_: (layer,) + (0,) * (nd - 1))


def _put(ref, val):
    ref[...] = val.reshape(ref.shape)


def _skip_refs(body, n_in, n_skip, *refs):
    return body(*refs[:n_in], *refs[n_in + n_skip:])


def _stacked(stack, m, tm, cols):
    if stack is None:
        return jax.ShapeDtypeStruct((m, cols), F32), pl.BlockSpec((tm, cols), lambda i: (i, 0))
    layer, depth, _ = stack
    return (jax.ShapeDtypeStruct((depth, m, cols), F32),
            pl.BlockSpec((1, tm, cols), lambda i: (layer, i, 0)))


def _alias_args(stack, n_in, out_indices):
    if stack is None or stack[2] is None:
        return [], [], {}
    prev = list(stack[2])
    specs = [pl.BlockSpec(memory_space=pl.ANY) for _ in prev]
    return prev, specs, {n_in + k: out for k, out in enumerate(out_indices)}


def _params(n_axes):
    return pltpu.CompilerParams(dimension_semantics=("arbitrary",) * n_axes,
                                vmem_limit_bytes=VMEM_LIMIT_BYTES)


MLA_IN_COLS = Q_LORA + KV_LORA + 2 * LANES + HEADS_W + D_MODEL


def _proj_mla_kernel(x_ref, g_ref, w_ref, gq_ref, gkv_ref, wuq_ref, wukn_ref, wuvt_ref,
                     cosT_ref, sinT_ref, ctok_ref, stok_ref, grp_ref,
                     ckv_ref, kr_ref, za_ref, ga_ref, qnT_ref, qrT_ref, knb_ref, krb_ref, vT_ref,
                     qn2_ref, kn2_ref):
    tm = x_ref.shape[0]
    hb = _rms(x_ref[...], g_ref[0]).astype(BF16)
    o = 0
    zcq = _dot(hb, w_ref[0, :, o:o + Q_LORA]); o += Q_LORA
    zckv = _dot(hb, w_ref[0, :, o:o + KV_LORA]); o += KV_LORA
    zka = _dot(hb, w_ref[0, :, o:o + LANES]); o += LANES
    zkb = _dot(hb, w_ref[0, :, o:o + LANES]); o += LANES
    za_ref[...] = _silu(_dot(hb, w_ref[0, :, o:o + HEADS_W])).astype(BF16); o += HEADS_W
    ga_ref[...] = _sigmoid(_dot(hb, w_ref[0, :, o:o + D_MODEL])).astype(BF16)

    cqb = _rms(zcq, gq_ref[0]).astype(BF16)
    qT = _dot_nt(wuq_ref[0], cqb)
    qn = qT[0:HEADS_W] * MLA_QSCALE
    qnT_ref[...] = qn.astype(BF16)
    x1 = qT[HEADS_W:HEADS_W + LANES]
    x2 = qT[HEADS_W + LANES:HEADS_W + 2 * LANES]
    c = cosT_ref[...]
    s = sinT_ref[...]
    r1 = (x1 * c - x2 * s) * MLA_QSCALE
    r2 = (x1 * s + x2 * c) * MLA_QSCALE
    n1 = r1.astype(BF16)
    n2 = r2.astype(BF16)
    for h in range(N_HEADS):
        qrT_ref[h * ROPE_DIM:h * ROPE_DIM + HALF_ROPE, :] = n1[h * HALF_ROPE:(h + 1) * HALF_ROPE]
        qrT_ref[h * ROPE_DIM + HALF_ROPE:(h + 1) * ROPE_DIM, :] = n2[h * HALF_ROPE:(h + 1) * HALF_ROPE]
    qn2 = (jnp.sum((qn * qn).reshape(N_HEADS, NOPE_DIM, tm), axis=1)
           + jnp.sum((r1 * r1 + r2 * r2).reshape(N_HEADS, HALF_ROPE, tm), axis=1))
    qn2_ref[0] = jnp.broadcast_to(jnp.max(qn2, axis=1, keepdims=True), (N_HEADS, LANES))

    ckv = _rms(zckv, gkv_ref[0])
    _put(ckv_ref, ckv)
    cb = ckv.astype(BF16)
    kn = _dot(cb, wukn_ref[0])
    knb_ref[0] = kn.astype(BF16)
    vT_ref[0] = _dot_nt(wuvt_ref[0], cb).astype(BF16)

    kr = zka * ctok_ref[...] + zkb * stok_ref[...]
    kr_ref[...] = kr
    krb_ref[0] = kr.astype(BF16)
    kn2 = _dot((kn * kn).astype(BF16), grp_ref[...]) + jnp.sum(kr * kr, axis=1, keepdims=True)
    kn2_ref[0] = jnp.max(kn2, axis=0, keepdims=True)


def _proj_mla(layer, x, g, w, gq, gkv, wuq, wukn, wuvt, cosT, sinT, ctok, stok, grp, tm, stack=None):
    m = x.shape[0]
    nb = m // tm
    row = lambda c: pl.BlockSpec((tm, c), lambda i: (i, 0))
    colT = lambda r: pl.BlockSpec((r, tm), lambda i: (0, i))
    blk3 = lambda a, b: pl.BlockSpec((1, a, b), lambda i: (i, 0, 0))
    ckv_shape, ckv_spec = _stacked(stack, m, tm, KV_LORA)
    out_shape = (
        ckv_shape,
        jax.ShapeDtypeStruct((m, LANES), F32),
        jax.ShapeDtypeStruct((m, HEADS_W), BF16),
        jax.ShapeDtypeStruct((m, D_MODEL), BF16),
        jax.ShapeDtypeStruct((HEADS_W, m), BF16),
        jax.ShapeDtypeStruct((N_HEADS * ROPE_DIM, m), BF16),
        jax.ShapeDtypeStruct((nb, tm, HEADS_W), BF16),
        jax.ShapeDtypeStruct((nb, tm, LANES), BF16),
        jax.ShapeDtypeStruct((nb, HEADS_W, tm), BF16),
        jax.ShapeDtypeStruct((nb, N_HEADS, LANES), F32),
        jax.ShapeDtypeStruct((nb, 1, LANES), F32),
    )
    out_specs = (ckv_spec, row(LANES), row(HEADS_W), row(D_MODEL), colT(HEADS_W),
                 colT(N_HEADS * ROPE_DIM), blk3(tm, HEADS_W), blk3(tm, LANES), blk3(HEADS_W, tm),
                 blk3(N_HEADS, LANES), blk3(1, LANES))
    in_specs = [row(D_MODEL)] + [_layer_spec(a, layer) for a in (g, w, gq, gkv, wuq, wukn, wuvt)] + [
        colT(LANES), colT(LANES), row(LANES), row(LANES), _full_spec(grp.shape)]
    prev, prev_specs, aliases = _alias_args(stack, len(in_specs), (0,))
    return pl.pallas_call(
        functools.partial(_skip_refs, _proj_mla_kernel, len(in_specs), len(prev)),
        grid=(nb,), in_specs=in_specs + prev_specs, out_specs=out_specs, out_shape=out_shape,
        input_output_aliases=aliases, compiler_params=_params(1), name="proj_mla",
    )(x, g, w, gq, gkv, wuq, wukn, wuvt, cosT, sinT, ctok, stok, grp, *prev)


FOX_IN_COLS = 2 * HEADS_W + LANES + HEADS_W + D_MODEL


def _proj_fox_kernel(x_ref, g_ref, w_ref, wfqt_ref, wfvt_ref, bf_ref, tri_ref, pk_ref, pqt_ref, grp_ref,
                     fk_ref, fv_ref, logf_ref, zb_ref, gb_ref, fqT_ref, fkb_ref, fvT_ref,
                     augk_ref, augqT_ref, base_ref, qn2_ref, kn2_ref, carry_ref):
    i = pl.program_id(0)
    tm = x_ref.shape[0]
    hb = _rms(x_ref[...], g_ref[0]).astype(BF16)
    o = 0
    fk = _dot(hb, w_ref[0, :, o:o + HEADS_W]); o += HEADS_W
    _put(fk_ref, fk)
    fkb_ref[0] = fk.astype(BF16)
    kn2_ref[0] = jnp.max(_dot((fk * fk).astype(BF16), grp_ref[...]), axis=0, keepdims=True)
    _put(fv_ref, _dot(hb, w_ref[0, :, o:o + HEADS_W])); o += HEADS_W
    zf = _dot(hb, w_ref[0, :, o:o + LANES]) + bf_ref[0]; o += LANES
    zb_ref[...] = _silu(_dot(hb, w_ref[0, :, o:o + HEADS_W])).astype(BF16); o += HEADS_W
    gb_ref[...] = _sigmoid(_dot(hb, w_ref[0, :, o:o + D_MODEL])).astype(BF16)
    fqT = _dot_nt(wfqt_ref[0], hb) * FOX_QSCALE
    fqT_ref[...] = fqT.astype(BF16)
    qn2 = jnp.sum((fqT * fqT).reshape(N_HEADS, HEAD_DV, tm), axis=1)
    qn2_ref[0] = jnp.broadcast_to(jnp.max(qn2, axis=1, keepdims=True), (N_HEADS, LANES))
    fvT_ref[0] = _dot_nt(wfvt_ref[0], hb).astype(BF16)

    lane = lax.broadcasted_iota(jnp.int32, (tm, LANES), 1)
    logf = jnp.minimum(zf, 0.0) - jnp.log(1.0 + jnp.exp(-jnp.abs(zf)))
    logf = jnp.where(lane < N_HEADS, logf, 0.0)
    logf_ref[...] = logf[:, 0:N_HEADS]

    hi, mid, lo = _split3(logf)
    tri = tri_ref[...]
    r = _dot(tri, hi) + _dot(tri, mid) + _dot(tri, lo)

    @pl.when(i == 0)
    def _():
        carry_ref[...] = jnp.zeros_like(carry_ref)

    base_ref[0] = carry_ref[...] * LOG2E
    carry_ref[...] = carry_ref[...] + r[tm - 1:tm, :]

    r2 = r * LOG2E
    rcat = jnp.concatenate(_split3(r2), axis=1)
    ecat = jnp.concatenate(_split3(r2[tm - 1:tm, :] - r2), axis=1)
    slot = lane % AUG_PER_HEAD
    ones_k = jnp.where(slot < 3, 1.0, 0.0)
    augk_ref[0] = (_dot(jnp.concatenate([rcat, ecat], axis=1), pk_ref[...]) + ones_k).astype(BF16)
    rowq = lax.broadcasted_iota(jnp.int32, (N_HEADS * AUG_PER_HEAD, tm), 0) % AUG_PER_HEAD
    ones_q = jnp.where((rowq >= 3) & (rowq < 9), 1.0, 0.0)
    augqT_ref[...] = (_dot_nt(pqt_ref[...], rcat) + ones_q).astype(BF16)


def _proj_fox(layer, x, g, w, wfqt, wfvt, bf, tri, pk, pqt, grp, tm, stack=None):
    m = x.shape[0]
    nb = m // tm
    row = lambda c: pl.BlockSpec((tm, c), lambda i: (i, 0))
    colT = lambda r: pl.BlockSpec((r, tm), lambda i: (0, i))
    blk3 = lambda a, b: pl.BlockSpec((1, a, b), lambda i: (i, 0, 0))
    kv_shape, kv_spec = _stacked(stack, m, tm, HEADS_W)
    out_shape = (
        kv_shape,
        kv_shape,
        jax.ShapeDtypeStruct((m, N_HEADS), F32),
        jax.ShapeDtypeStruct((m, HEADS_W), BF16),
        jax.ShapeDtypeStruct((m, D_MODEL), BF16),
        jax.ShapeDtypeStruct((HEADS_W, m), BF16),
        jax.ShapeDtypeStruct((nb, tm, HEADS_W), BF16),
        jax.ShapeDtypeStruct((nb, HEADS_W, tm), BF16),
        jax.ShapeDtypeStruct((nb, tm, LANES), BF16),
        jax.ShapeDtypeStruct((N_HEADS * AUG_PER_HEAD, m), BF16),
        jax.ShapeDtypeStruct((nb, 1, LANES), F32),
        jax.ShapeDtypeStruct((nb, N_HEADS, LANES), F32),
        jax.ShapeDtypeStruct((nb, 1, LANES), F32),
    )
    out_specs = (kv_spec, kv_spec, row(N_HEADS), row(HEADS_W), row(D_MODEL), colT(HEADS_W),
                 blk3(tm, HEADS_W), blk3(HEADS_W, tm), blk3(tm, LANES), colT(N_HEADS * AUG_PER_HEAD),
                 blk3(1, LANES), blk3(N_HEADS, LANES), blk3(1, LANES))
    in_specs = ([row(D_MODEL)] + [_layer_spec(a, layer) for a in (g, w, wfqt, wfvt, bf)]
                + [_full_spec(a.shape) for a in (tri, pk, pqt, grp)])
    prev, prev_specs, aliases = _alias_args(stack, len(in_specs), (0, 1))
    return pl.pallas_call(
        functools.partial(_skip_refs, _proj_fox_kernel, len(in_specs), len(prev)),
        grid=(nb,), in_specs=in_specs + prev_specs, out_specs=out_specs, out_shape=out_shape,
        scratch_shapes=[pltpu.VMEM((1, LANES), F32)],
        input_output_aliases=aliases, compiler_params=_params(1), name="proj_fox",
    )(x, g, w, wfqt, wfvt, bf, tri, pk, pqt, grp, *prev)


def _attn_kernel(tab_ref, qp_ref, qe_ref, ka_ref, ke_ref, vT_ref, o_ref, acc_ref, s_ref, *, fox):
    g = pl.program_id(0)
    i = pl.program_id(1)
    tq = qp_ref.shape[1]
    tk = ka_ref.shape[1]
    rows_p = lax.broadcasted_iota(jnp.int32, (LANES, tq), 0)
    rows_e = lax.broadcasted_iota(jnp.int32, qe_ref.shape, 0)
    kpos = lax.broadcasted_iota(jnp.int32, (tk, tq), 0)
    qpos = lax.broadcasted_iota(jnp.int32, (tk, tq), 1)
    if fox:
        visible = kpos <= qpos
    else:
        visible = (kpos // CHUNK) <= (qpos // CHUNK)
    qe = qe_ref[...]

    def head_weights(hq, to_tile_end):
        pair, hh = divmod(hq, 2)
        qp = qp_ref[pair * LANES:(pair + 1) * LANES, :]
        keep_p = (rows_p >= hh * HEAD_DV) & (rows_p < (hh + 1) * HEAD_DV)
        if fox:
            slot = rows_e - (HEADS_PER_STEP * g + hq) * AUG_PER_HEAD
            if to_tile_end:
                keep_e = ((slot >= 0) & (slot < 3)) | ((slot >= 6) & (slot < 9))
            else:
                keep_e = (slot >= 0) & (slot < 6)
            extra = jnp.where(keep_e, qe, jnp.zeros_like(qe))
        else:
            extra = qe[hq * ROPE_DIM:(hq + 1) * ROPE_DIM]
        parts = [jnp.where(keep_p, qp, jnp.zeros_like(qp)), extra]
        if extra.shape[0] < LANES:
            parts.append(jnp.zeros((LANES - extra.shape[0], tq), BF16))
        return jnp.concatenate(parts, axis=0)

    ws = [head_weights(hq, False) for hq in range(HEADS_PER_STEP)]
    ws_end = [head_weights(hq, True) for hq in range(HEADS_PER_STEP)] if fox else ws

    steps = [(u, hq) for u in range(tk // KEY_SUB) for hq in range(HEADS_PER_STEP)]
    ones_rows = jnp.ones((DEN_ROWS, KEY_SUB), BF16)

    def scores(j, step, to_tile_end=False):
        u, hq = step
        pair = hq // 2
        rows = slice(u * KEY_SUB, (u + 1) * KEY_SUB)
        lhs = jnp.concatenate([ka_ref[j, rows, pair * LANES:(pair + 1) * LANES], ke_ref[j, rows, :]], axis=1)
        return _dot(lhs, (ws_end if to_tile_end else ws)[hq])

    def block(j, stats, masked, j_next):
        stats = list(stats)
        n_steps = len(steps)
        tiles = {}
        for n, (u, hq) in enumerate(steps):
            ahead = n + LOOKAHEAD
            if ahead < n_steps:
                tiles[ahead] = scores(j, steps[ahead])
            elif j_next is not None:
                tiles[ahead] = scores(j_next, steps[ahead - n_steps])
            sT = tiles.pop(n) if n in tiles else s_ref[n]
            rows = slice(u * KEY_SUB, (u + 1) * KEY_SUB)
            m_old = stats[hq]
            if masked:
                sT = jnp.where(visible[rows], sT, NEG)
            if fox:
                h = HEADS_PER_STEP * g + hq
                d = tab_ref[0, h, i] - tab_ref[0, h, j]
            else:
                d = 0.0
            m_new = jnp.maximum(m_old, jnp.max(sT, axis=0, keepdims=True) + d)
            alpha = jnp.exp2(m_old - m_new)
            pT = jnp.exp2((sT - (m_new - d)).astype(BF16))
            stats[hq] = m_new
            v = jnp.concatenate([vT_ref[j, hq * HEAD_DV:(hq + 1) * HEAD_DV, rows], ones_rows], axis=0)
            acc_ref[hq] = alpha * acc_ref[hq] + _dot(v, pT)
        for n, tile in tiles.items():
            s_ref[n - n_steps] = tile
        return tuple(stats)

    def more(j_top, stats):
        alive = j_top >= 0
        if fox:
            jt = jnp.maximum(j_top, 0)
            slack = None
            for hq in range(HEADS_PER_STEP):
                h = HEADS_PER_STEP * g + hq
                bound = tab_ref[1, h, i] * tab_ref[2, h, jt] + (tab_ref[0, h, i] - tab_ref[0, h, jt + 1])
                room = stats[hq] - bound
                slack = room if slack is None else jnp.minimum(slack, room)
            alive = alive & (jnp.min(slack) < SKIP_LOG2)
        return alive.astype(jnp.int32)

    def block_unshifted(j, lows, masked, j_next):
        lows = list(lows)
        n_steps = len(steps)
        tiles = {}
        for n, (u, hq) in enumerate(steps):
            ahead = n + LOOKAHEAD
            if ahead < n_steps:
                tiles[ahead] = scores(j, steps[ahead], not masked)
            else:
                tiles[ahead] = scores(j_next, steps[ahead - n_steps], True)
            sT = tiles.pop(n) if n in tiles else s_ref[n]
            rows = slice(u * KEY_SUB, (u + 1) * KEY_SUB)
            if masked:
                sT = jnp.where(visible[rows], sT, NEG)
                if fox:
                    lows[hq] = jnp.maximum(lows[hq], jnp.max(sT, axis=0, keepdims=True))
            pT = jnp.exp2(sT).astype(BF16)
            v = jnp.concatenate([vT_ref[j, hq * HEAD_DV:(hq + 1) * HEAD_DV, rows], ones_rows], axis=0)
            pv = _dot(v, pT)
            if fox and not masked:
                h = HEADS_PER_STEP * g + hq
                pv = pv * jnp.exp2(jnp.full((1, 1), tab_ref[0, h, i] - tab_ref[0, h, j + 1], F32))
            acc_ref[hq] = acc_ref[hq] + pv
        for n, tile in tiles.items():
            s_ref[n - n_steps] = tile
        return tuple(lows)

    def walk_online():
        m0 = jnp.full((1, tq), NEG, F32)
        stats = block(i, (m0,) * HEADS_PER_STEP, True, jnp.maximum(i - 1, 0))

        def visit(c):
            j = c[0]
            st = block(j, c[2:], False, jnp.maximum(j - 1, 0))
            return (j - 1, more(j - 1, st)) + tuple(st)

        lax.while_loop(lambda c: c[1] > 0, visit, (i - 1, more(i - 1, stats)) + tuple(stats))

    def walk_unshifted():
        m0 = jnp.full((1, tq), NEG, F32)
        lows = block_unshifted(i, (m0,) * HEADS_PER_STEP, True, jnp.maximum(i - 1, 0))
        if fox:
            def visit(c):
                j = c[0]
                block_unshifted(j, lows, False, jnp.maximum(j - 1, 0))
                return (j - 1, more(j - 1, lows))

            lax.while_loop(lambda c: c[1] > 0, visit, (i - 1, more(i - 1, lows)))
        else:
            def visit(k, c):
                j = i - 1 - k
                block_unshifted(j, lows, False, jnp.maximum(j - 1, 0))
                return c

            lax.fori_loop(0, i, visit, 0)

    acc_ref[...] = jnp.zeros_like(acc_ref)
    for n in range(LOOKAHEAD):
        s_ref[n] = scores(i, steps[n])
    bounded = None
    for hq in range(HEADS_PER_STEP):
        h = HEADS_PER_STEP * g + hq
        ok = tab_ref[1, h, i] * tab_ref[2, h, i] <= SAFE_LOG2
        bounded = ok if bounded is None else bounded & ok
    pl.when(bounded)(walk_unshifted)
    pl.when(jnp.logical_not(bounded))(walk_online)
    outs = [acc_ref[hq, 0:HEAD_DV, :] / acc_ref[hq, HEAD_DV:HEAD_DV + 1, :] for hq in range(HEADS_PER_STEP)]
    o_ref[...] = jnp.concatenate(outs, axis=0).T.astype(BF16)


def _attention(tab, qpT, qeT, ka, ke, vT, fox):
    nb, tk, _ = ka.shape
    s = qpT.shape[1]
    tq = tk
    width = HEADS_PER_STEP * HEAD_DV
    e_rows = qeT.shape[0] if fox else HEADS_PER_STEP * ROPE_DIM
    e_map = (lambda g, i, b: (0, i)) if fox else (lambda g, i, b: (g, i))
    once = dict(pipeline_mode=pl.Buffered(1))
    grid_spec = pltpu.PrefetchScalarGridSpec(
        num_scalar_prefetch=1,
        grid=(N_HEADS // HEADS_PER_STEP, nb),
        in_specs=[
            pl.BlockSpec((width, tq), lambda g, i, b: (g, i)),
            pl.BlockSpec((e_rows, tq), e_map),
            pl.BlockSpec((nb, tk, width), lambda g, i, b: (0, 0, g), **once),
            pl.BlockSpec((nb, tk, LANES), lambda g, i, b: (0, 0, 0), **once),
            pl.BlockSpec((nb, width, tk), lambda g, i, b: (0, g, 0), **once),
        ],
        out_specs=pl.BlockSpec((tq, width), lambda g, i, b: (i, g)),
        scratch_shapes=[pltpu.VMEM((HEADS_PER_STEP, HEAD_DV + DEN_ROWS, tq), F32),
                        pltpu.VMEM((LOOKAHEAD, KEY_SUB, tq), F32)],
    )
    return pl.pallas_call(
        functools.partial(_attn_kernel, fox=fox), grid_spec=grid_spec,
        out_shape=jax.ShapeDtypeStruct((s, HEADS_W), BF16),
        compiler_params=_params(2), name="attn_fox" if fox else "attn_mla",
    )(tab, qpT, qeT, ka, ke, vT)


def _softmax_pv(s, v):
    m = jnp.max(s, axis=-1, keepdims=True)
    p = jnp.exp2(s - m)
    l = jnp.sum(p, axis=-1, keepdims=True)
    return _dot(p.astype(BF16), v) / l


def _diag_heads(o_big, t_new):
    rows = lax.broadcasted_iota(jnp.int32, o_big.shape, 0) // t_new
    cols = lax.broadcasted_iota(jnp.int32, o_big.shape, 1) // HEAD_DV
    kept = jnp.where(rows == cols, o_big, 0.0)
    return jnp.sum(kept.reshape(N_HEADS, t_new, o_big.shape[1]), axis=0)


def _sample_attn_kernel(qn_ref, qr_ref, qf_ref, lrep_ref, cckv_ref, nckv_ref, ckr_ref, nkr_ref,
                        cfk_ref, nfk_ref, cfv_ref, nfv_ref, wukn_ref, wuv_ref, oa_ref, ob_ref):
    past = cckv_ref.shape[2]
    t_new = nckv_ref.shape[0]
    rows = N_HEADS * t_new
    pad = lrep_ref.shape[2] - past - t_new
    keys = past + t_new + pad

    def with_new(cache, new):
        return jnp.concatenate([cache, new, jnp.zeros((pad, new.shape[1]), new.dtype)], axis=0)

    kidx = lax.broadcasted_iota(jnp.int32, (rows, keys), 1)
    qpos = past + lax.broadcasted_iota(jnp.int32, (rows, keys), 0) % t_new
    real = kidx < past + t_new

    ckv_all = with_new(cckv_ref[0, 0], nckv_ref[...]).astype(BF16)
    kn = _dot(ckv_all, wukn_ref[0]).astype(BF16)
    va = _dot(ckv_all, wuv_ref[0]).astype(BF16)
    kr = with_new(ckr_ref[0, 0], nkr_ref[:, 0:ROPE_DIM]).astype(BF16)
    s = _dot_nt(qn_ref[0], kn) + _dot_nt(qr_ref[0], kr)
    s = jnp.where(real & ((kidx // CHUNK) <= (qpos // CHUNK)), s, NEG)
    oa_ref[...] = _diag_heads(_softmax_pv(s, va), t_new).astype(BF16)

    fk = with_new(cfk_ref[0, 0], nfk_ref[...]).astype(BF16)
    fv = with_new(cfv_ref[0, 0], nfv_ref[...]).astype(BF16)
    c = lrep_ref[0]
    shift = 1
    while shift < keys:
        c = c + jnp.where(kidx >= shift, pltpu.roll(c, shift, 1), 0.0)
        shift *= 2
    cq = jnp.sum(jnp.where(kidx == qpos, c, 0.0), axis=-1, keepdims=True)
    s = _dot_nt(qf_ref[0], fk) + (cq - c) * LOG2E
    s = jnp.where(real & (kidx <= qpos), s, NEG)
    ob_ref[...] = _diag_heads(_softmax_pv(s, fv), t_new).astype(BF16)


def _sample_attention(layer, qn_bd, qr, qf_bd, lrep, cckv, nckv, ckr, nkr, cfk, nfk, cfv, nfv, wukn, wuv):
    nbatch = cckv.shape[1]
    t_new = nckv.shape[0] // nbatch
    b3 = lambda a: pl.BlockSpec((1,) + a.shape[1:], lambda b: (b, 0, 0))
    cache = lambda a: pl.BlockSpec((1, 1) + a.shape[2:], lambda b: (layer, b, 0, 0))
    new = lambda a: pl.BlockSpec((t_new, a.shape[1]), lambda b: (b, 0))
    in_specs = [b3(qn_bd), b3(qr), b3(qf_bd), b3(lrep), cache(cckv), new(nckv), cache(ckr), new(nkr),
                cache(cfk), new(nfk), cache(cfv), new(nfv), _layer_spec(wukn, layer), _layer_spec(wuv, layer)]
    out = jax.ShapeDtypeStruct((nbatch * t_new, HEADS_W), BF16)
    o_spec = pl.BlockSpec((t_new, HEADS_W), lambda b: (b, 0))
    return pl.pallas_call(
        _sample_attn_kernel, grid=(nbatch,), in_specs=in_specs, out_specs=(o_spec, o_spec),
        out_shape=(out, out), compiler_params=_params(1), name="attn_sample",
    )(qn_bd, qr, qf_bd, lrep, cckv, nckv, ckr, nkr, cfk, nfk, cfv, nfv, wukn, wuv)


def _merge_kernel(x_ref, oa_ref, za_ref, ob_ref, zb_ref, ga_ref, gb_ref, woa_ref, wob_ref, wout_ref,
                  fg_ref, xn_ref, *maybe_y_ref):
    a = _dot(oa_ref[...] * za_ref[...], woa_ref[0])
    b = _dot(ob_ref[...] * zb_ref[...], wob_ref[0])
    mix = ga_ref[...].astype(F32) * a + gb_ref[...].astype(F32) * b
    xn = x_ref[...] + _dot(mix.astype(BF16), wout_ref[0])
    xn_ref[...] = xn
    if maybe_y_ref:
        maybe_y_ref[0][...] = _rms(xn, fg_ref[...])


def _merge(layer, x, oa, za, ob, zb, ga, gb, woa, wob, wout, fg, tm, final):
    m = x.shape[0]
    row = lambda c: pl.BlockSpec((tm, c), lambda i: (i, 0))
    in_specs = [row(D_MODEL), row(HEADS_W), row(HEADS_W), row(HEADS_W), row(HEADS_W), row(D_MODEL),
                row(D_MODEL), _layer_spec(woa, layer), _layer_spec(wob, layer), _layer_spec(wout, layer),
                _full_spec(fg.shape)]
    n_out = 2 if final else 1
    out_shape = tuple(jax.ShapeDtypeStruct((m, D_MODEL), F32) for _ in range(n_out))
    out_specs = tuple(row(D_MODEL) for _ in range(n_out))
    return pl.pallas_call(
        _merge_kernel, grid=(m // tm,), in_specs=in_specs, out_specs=out_specs, out_shape=out_shape,
        compiler_params=_params(1), name="merge",
    )(x, oa, za, ob, zb, ga, gb, woa, wob, wout, fg)


def _pad_cols(w, width):
    return jnp.pad(w, ((0, 0), (0, width - w.shape[1])))


def _pad_last(w, width):
    return jnp.pad(w, [(0, 0)] * (w.ndim - 1) + [(0, width - w.shape[-1])])


def _stacked_weights(w_in, w_uq, w_ukv, w_oa, w_ob, w_out, b_f, norm_g, g_q, g_kv):
    depth = w_in.shape[0]
    offs = np.cumsum(IN_SIZES)[:-1].tolist()
    w_cq, w_ckv, w_kr, w_za, w_fq, w_fk, w_fv, w_zf, w_zb, w_ga, w_gb = jnp.split(w_in, offs, axis=2)
    w_kr_swapped = jnp.concatenate([w_kr[..., HALF_ROPE:], w_kr[..., :HALF_ROPE]], axis=2)
    w_mla = jnp.concatenate([w_cq, w_ckv, _pad_last(w_kr, LANES), _pad_last(w_kr_swapped, LANES),
                             w_za, w_ga], axis=2).astype(BF16)
    w_fox = jnp.concatenate([w_fk, w_fv, _pad_last(w_zf, LANES), w_zb, w_gb], axis=2).astype(BF16)
    uq = w_uq.reshape(depth, Q_LORA, N_HEADS, NOPE_DIM + ROPE_DIM)
    uq_rows = jnp.concatenate([
        uq[..., :NOPE_DIM].reshape(depth, Q_LORA, -1),
        uq[..., NOPE_DIM:NOPE_DIM + HALF_ROPE].reshape(depth, Q_LORA, -1),
        uq[..., NOPE_DIM + HALF_ROPE:].reshape(depth, Q_LORA, -1)], axis=2)
    ukv = w_ukv.reshape(depth, KV_LORA, N_HEADS, NOPE_DIM + HEAD_DV)
    w_ukn = ukv[..., :NOPE_DIM].reshape(depth, KV_LORA, -1).astype(BF16)
    w_uv = ukv[..., NOPE_DIM:].reshape(depth, KV_LORA, -1).astype(BF16)
    tr = lambda a: jnp.swapaxes(a, 1, 2)
    return dict(
        w_mla=w_mla, w_fox=w_fox, w_uqT=tr(uq_rows).astype(BF16), w_ukn=w_ukn, w_uv=w_uv, w_uvT=tr(w_uv),
        w_fqT=tr(w_fq).astype(BF16), w_fvT=tr(w_fv).astype(BF16),
        b_f=_pad_last(b_f[:, None, :], LANES),
        w_oa=w_oa.astype(BF16), w_ob=w_ob.astype(BF16), w_out=w_out.astype(BF16),
        norm_g=norm_g[:, None, :], g_q=g_q[:, None, :], g_kv=g_kv[:, None, :])


def _rope_tables(pos):
    inv = jnp.exp(-math.log(ROPE_THETA) * jnp.arange(HALF_ROPE, dtype=F32) / HALF_ROPE)
    ang = pos.astype(F32)[:, None] * inv[None, :]
    cos = jnp.cos(ang)
    sin = jnp.sin(ang)
    cosT = jnp.tile(cos.T, (N_HEADS, 1))
    sinT = jnp.tile(sin.T, (N_HEADS, 1))
    ctok = _pad_cols(jnp.concatenate([cos, cos], axis=1), LANES)
    stok = _pad_cols(jnp.concatenate([-sin, sin], axis=1), LANES)
    return cosT, sinT, ctok, stok


def _bias_placement():
    pk = np.zeros((6 * LANES, LANES), np.float32)
    pqt = np.zeros((N_HEADS * AUG_PER_HEAD, 3 * LANES), np.float32)
    for c in range(3):
        for h in range(N_HEADS):
            pk[c * LANES + h, h * AUG_PER_HEAD + 3 + c] = -1.0
            pk[(3 + c) * LANES + h, h * AUG_PER_HEAD + 6 + c] = 1.0
            pqt[h * AUG_PER_HEAD + c, c * LANES + h] = 1.0
    return jnp.asarray(pk, BF16), jnp.asarray(pqt, BF16)


def _head_groups():
    grp = np.zeros((HEADS_W, LANES), np.float32)
    for h in range(N_HEADS):
        grp[h * HEAD_DV:(h + 1) * HEAD_DV, h] = 1.0
    return jnp.asarray(grp, BF16)


def _skip_tables(base, qn2, kn2):
    b = base[:, 0, :N_HEADS]
    qmax = jnp.sqrt(qn2[:, :, 0]) * NORM_SLACK
    kmax = lax.cummax(jnp.sqrt(kn2[:, 0, :N_HEADS]) * NORM_SLACK, axis=0)
    return jnp.stack([b, qmax, kmax]).transpose(0, 2, 1)


def _tri(n):
    return jnp.asarray(np.tril(np.ones((n, n), np.float32)), BF16)


def _block_diag_queries(qT, nbatch, t_new):
    q = qT.T.reshape(nbatch, t_new, N_HEADS, HEAD_DV)
    eye = jnp.eye(N_HEADS, dtype=q.dtype)
    return jnp.einsum('bthj,hg->bhtgj', q, eye).reshape(nbatch, N_HEADS * t_new, HEADS_W)


def _project(layer, x, sw, tables, tri, pk, pqt, grp, tm, stack_mla=None, stack_fox=None):
    cosT, sinT, ctok, stok = tables
    mla = _proj_mla(layer, x, sw['norm_g'], sw['w_mla'], sw['g_q'], sw['g_kv'], sw['w_uqT'], sw['w_ukn'],
                    sw['w_uvT'], cosT, sinT, ctok, stok, grp, tm, stack_mla)
    fox = _proj_fox(layer, x, sw['norm_g'], sw['w_fox'], sw['w_fqT'], sw['w_fvT'], sw['b_f'], tri, pk, pqt, grp,
                    tm, stack_fox)
    return mla, fox


def kernel(x_prompt, x_sample, cache_mla_ckv, cache_mla_krope, cache_fox_k, cache_fox_v, cache_fox_logf,
           norm_g, w_in, g_q, w_uq, g_kv, w_ukv, b_f, w_oa, w_ob, w_out, final_g):
    depth = w_in.shape[0]
    _, seq, _ = x_prompt.shape
    nbatch, t_new, _ = x_sample.shape
    past = cache_mla_ckv.shape[2]
    m_s = nbatch * t_new
    key_pad = -(past + t_new) % LANES

    tabs_p = _rope_tables(jnp.arange(seq, dtype=jnp.int32))
    tabs_s = _rope_tables(past + jnp.arange(m_s, dtype=jnp.int32) % t_new)
    pk, pqt = _bias_placement()
    grp = _head_groups()
    tri_p = _tri(SEQ_BLOCK)
    tri_s = _tri(m_s)
    fg = final_g[None, :]

    xp = x_prompt.reshape(seq, D_MODEL)
    xs = x_sample.reshape(m_s, D_MODEL)
    outs = {k: [] for k in ('p_kr', 'p_lf', 's_ckv', 's_kr', 's_fk', 's_fv', 's_lf')}
    p_ckv_stack = p_kv_stack = None
    yp = ys = None
    sw = _stacked_weights(w_in, w_uq, w_ukv, w_oa, w_ob, w_out, b_f, norm_g, g_q, g_kv)
    for l in range(depth):
        final = l == depth - 1

        mla, fox = _project(l, xp, sw, tabs_p, tri_p, pk, pqt, grp, SEQ_BLOCK,
                            (l, depth, p_ckv_stack), (l, depth, p_kv_stack))
        ckv, kr, za, ga, qnT, qrT, knb, krb, vT, qn2_a, kn2_a = mla
        fk, fv, logf, zb, gb, fqT, fkb, fvT, augk, augqT, base, qn2, kn2 = fox
        tab = _skip_tables(base, qn2, kn2)
        tab_a = _skip_tables(jnp.zeros_like(base), qn2_a, kn2_a)
        o_a = _attention(tab_a, qnT, qrT, knb, krb, vT, fox=False)
        o_b = _attention(tab, fqT, augqT, fkb, augk, fvT, fox=True)
        res = _merge(l, xp, o_a, za, o_b, zb, ga, gb, sw['w_oa'], sw['w_ob'], sw['w_out'], fg, SEQ_BLOCK, final)
        xp = res[0]
        if final:
            yp = res[1]
        p_ckv_stack, p_kv_stack = (ckv,), (fk, fv)
        outs['p_kr'].append(kr[:, :ROPE_DIM]); outs['p_lf'].append(logf)

        mla, fox = _project(l, xs, sw, tabs_s, tri_s, pk, pqt, grp, m_s)
        ckv, kr, za, ga, qnT, qrT = mla[:6]
        fk, fv, logf, zb, gb, fqT = fox[:6]
        qn_bd = _block_diag_queries(qnT, nbatch, t_new)
        qf_bd = _block_diag_queries(fqT, nbatch, t_new)
        qr = qrT.reshape(N_HEADS, ROPE_DIM, nbatch, t_new).transpose(2, 0, 3, 1).reshape(
            nbatch, N_HEADS * t_new, ROPE_DIM)
        lf_all = jnp.concatenate([cache_fox_logf[l], logf.reshape(nbatch, t_new, N_HEADS),
                                  jnp.zeros((nbatch, key_pad, N_HEADS), F32)], axis=1)
        lrep = jnp.repeat(lf_all.transpose(0, 2, 1), t_new, axis=1)
        o_a, o_b = _sample_attention(
            l, qn_bd, qr, qf_bd, lrep, cache_mla_ckv, ckv, cache_mla_krope, kr,
            cache_fox_k.reshape(depth, nbatch, past, HEADS_W), fk,
            cache_fox_v.reshape(depth, nbatch, past, HEADS_W), fv, sw['w_ukn'], sw['w_uv'])
        res = _merge(l, xs, o_a, za, o_b, zb, ga, gb, sw['w_oa'], sw['w_ob'], sw['w_out'], fg, m_s, final)
        xs = res[0]
        if final:
            ys = res[1]
        outs['s_ckv'].append(ckv); outs['s_kr'].append(kr[:, :ROPE_DIM]); outs['s_fk'].append(fk)
        outs['s_fv'].append(fv); outs['s_lf'].append(logf)

    st = lambda name, shape: jnp.stack(outs[name]).reshape((depth,) + shape)
    return (yp.reshape(1, seq, D_MODEL), ys.reshape(nbatch, t_new, D_MODEL),
            p_ckv_stack[0].reshape(depth, 1, seq, KV_LORA), st('p_kr', (1, seq, ROPE_DIM)),
            p_kv_stack[0].reshape(depth, 1, seq, N_HEADS, HEAD_DV),
            p_kv_stack[1].reshape(depth, 1, seq, N_HEADS, HEAD_DV),
            st('p_lf', (1, seq, N_HEADS)),
            st('s_ckv', (nbatch, t_new, KV_LORA)), st('s_kr', (nbatch, t_new, ROPE_DIM)),
            st('s_fk', (nbatch, t_new, N_HEADS, HEAD_DV)), st('s_fv', (nbatch, t_new, N_HEADS, HEAD_DV)),
            st('s_lf', (nbatch, t_new, N_HEADS)))
```

```python
import functools
import math

import numpy as np
import jax
import jax.numpy as jnp
from jax import lax
from jax.experimental import pallas as pl
from jax.experimental.pallas import tpu as pltpu

D_MODEL = 1024
N_HEADS = 8
NOPE_DIM = 64
ROPE_DIM = 32
HALF_ROPE = ROPE_DIM // 2
HEAD_DV = 64
Q_LORA = 384
KV_LORA = 256
HEADS_W = N_HEADS * HEAD_DV
CHUNK = 64
ROPE_THETA = 10000.0
LOG2E = math.log2(math.e)
MLA_QSCALE = LOG2E / math.sqrt(NOPE_DIM + ROPE_DIM)
FOX_QSCALE = LOG2E / math.sqrt(HEAD_DV)
EPS = 1e-6
NEG = -1e30
IN_SIZES = (Q_LORA, KV_LORA, ROPE_DIM, HEADS_W, HEADS_W, HEADS_W, HEADS_W, N_HEADS, HEADS_W, D_MODEL, D_MODEL)

LANES = 128
SEQ_BLOCK = 512
KEY_SUB = 256
DEN_ROWS = 16
HEADS_PER_STEP = 8
LOOKAHEAD = 2
SAFE_LOG2 = 100.0
SKIP_LOG2 = 64.0
NORM_SLACK = 1.01
AUG_PER_HEAD = 16
VMEM_LIMIT_BYTES = 56 * 1024 * 1024

F32 = jnp.float32
BF16 = jnp.bfloat16


def _dot(a, b):
    return jnp.dot(a, b, preferred_element_type=F32)


def _dot_nt(a, b):
    return lax.dot_general(a, b, (((1,), (1,)), ((), ())), preferred_element_type=F32)


def _rms(x, g):
    return x * lax.rsqrt(jnp.mean(x * x, axis=-1, keepdims=True) + EPS) * g


def _sigmoid(x):
    return 1.0 / (1.0 + jnp.exp(-x))


def _silu(x):
    return x * _sigmoid(x)


def _split3(x):
    hi = x.astype(BF16)
    r1 = x - hi.astype(F32)
    mid = r1.astype(BF16)
    lo = (r1 - mid.astype(F32)).astype(BF16)
    return hi, mid, lo


def _full_spec(shape):
    nd = len(shape)
    return pl.BlockSpec(shape, lambda *_: (0,) * nd)


def _layer_spec(arr, layer):
    nd = arr.ndim
    return pl.BlockSpec((1,) + arr.shape[1:], lambda *_: (layer,) + (0,) * (nd - 1))


def _put(ref, val):
    ref[...] = val.reshape(ref.shape)


def _skip_refs(body, n_in, n_skip, *refs):
    return body(*refs[:n_in], *refs[n_in + n_skip:])


def _stacked(stack, m, tm, cols):
    if stack is None:
        return jax.ShapeDtypeStruct((m, cols), F32), pl.BlockSpec((tm, cols), lambda i: (i, 0))
    layer, depth, _ = stack
    return (jax.ShapeDtypeStruct((depth, m, cols), F32),
            pl.BlockSpec((1, tm, cols), lambda i: (layer, i, 0)))


def _alias_args(stack, n_in, out_indices):
    if stack is None or stack[2] is None:
        return [], [], {}
    prev = list(stack[2])
    specs = [pl.BlockSpec(memory_space=pl.ANY) for _ in prev]
    return prev, specs, {n_in + k: out for k, out in enumerate(out_indices)}


def _params(n_axes):
    return pltpu.CompilerParams(dimension_semantics=("arbitrary",) * n_axes,
                                vmem_limit_bytes=VMEM_LIMIT_BYTES)


MLA_IN_COLS = Q_LORA + KV_LORA + 2 * LANES + HEADS_W + D_MODEL


def _proj_mla_kernel(x_ref, g_ref, w_ref, gq_ref, gkv_ref, wuq_ref, wukn_ref, wuvt_ref,
                     cosT_ref, sinT_ref, ctok_ref, stok_ref, grp_ref,
                     ckv_ref, kr_ref, za_ref, ga_ref, qnT_ref, qrT_ref, knb_ref, krb_ref, vT_ref,
                     qn2_ref, kn2_ref):
    tm = x_ref.shape[0]
    hb = _rms(x_ref[...], g_ref[0]).astype(BF16)
    o = 0
    zcq = _dot(hb, w_ref[0, :, o:o + Q_LORA]); o += Q_LORA
    zckv = _dot(hb, w_ref[0, :, o:o + KV_LORA]); o += KV_LORA
    zka = _dot(hb, w_ref[0, :, o:o + LANES]); o += LANES
    zkb = _dot(hb, w_ref[0, :, o:o + LANES]); o += LANES
    za_ref[...] = _silu(_dot(hb, w_ref[0, :, o:o + HEADS_W])).astype(BF16); o += HEADS_W
    ga_ref[...] = _sigmoid(_dot(hb, w_ref[0, :, o:o + D_MODEL])).astype(BF16)

    cqb = _rms(zcq, gq_ref[0]).astype(BF16)
    qT = _dot_nt(wuq_ref[0], cqb)
    qn = qT[0:HEADS_W] * MLA_QSCALE
    qnT_ref[...] = qn.astype(BF16)
    x1 = qT[HEADS_W:HEADS_W + LANES]
    x2 = qT[HEADS_W + LANES:HEADS_W + 2 * LANES]
    c = cosT_ref[...]
    s = sinT_ref[...]
    r1 = (x1 * c - x2 * s) * MLA_QSCALE
    r2 = (x1 * s + x2 * c) * MLA_QSCALE
    n1 = r1.astype(BF16)
    n2 = r2.astype(BF16)
    for h in range(N_HEADS):
        qrT_ref[h * ROPE_DIM:h * ROPE_DIM + HALF_ROPE, :] = n1[h * HALF_ROPE:(h + 1) * HALF_ROPE]
        qrT_ref[h * ROPE_DIM + HALF_ROPE:(h + 1) * ROPE_DIM, :] = n2[h * HALF_ROPE:(h + 1) * HALF_ROPE]
    qn2 = (jnp.sum((qn * qn).reshape(N_HEADS, NOPE_DIM, tm), axis=1)
           + jnp.sum((r1 * r1 + r2 * r2).reshape(N_HEADS, HALF_ROPE, tm), axis=1))
    qn2_ref[0] = jnp.broadcast_to(jnp.max(qn2, axis=1, keepdims=True), (N_HEADS, LANES))

    ckv = _rms(zckv, gkv_ref[0])
    _put(ckv_ref, ckv)
    cb = ckv.astype(BF16)
    kn = _dot(cb, wukn_ref[0])
    knb_ref[0] = kn.astype(BF16)
    vT_ref[0] = _dot_nt(wuvt_ref[0], cb).astype(BF16)

    kr = zka * ctok_ref[...] + zkb * stok_ref[...]
    kr_ref[...] = kr
    krb_ref[0] = kr.astype(BF16)
    kn2 = _dot((kn * kn).astype(BF16), grp_ref[...]) + jnp.sum(kr * kr, axis=1, keepdims=True)
    kn2_ref[0] = jnp.max(kn2, axis=0, keepdims=True)


def _proj_mla(layer, x, g, w, gq, gkv, wuq, wukn, wuvt, cosT, sinT, ctok, stok, grp, tm, stack=None):
    m = x.shape[0]
    nb = m // tm
    row = lambda c: pl.BlockSpec((tm, c), lambda i: (i, 0))
    colT = lambda r: pl.BlockSpec((r, tm), lambda i: (0, i))
    blk3 = lambda a, b: pl.BlockSpec((1, a, b), lambda i: (i, 0, 0))
    ckv_shape, ckv_spec = _stacked(stack, m, tm, KV_LORA)
    out_shape = (
        ckv_shape,
        jax.ShapeDtypeStruct((m, LANES), F32),
        jax.ShapeDtypeStruct((m, HEADS_W), BF16),
        jax.ShapeDtypeStruct((m, D_MODEL), BF16),
        jax.ShapeDtypeStruct((HEADS_W, m), BF16),
        jax.ShapeDtypeStruct((N_HEADS * ROPE_DIM, m), BF16),
        jax.ShapeDtypeStruct((nb, tm, HEADS_W), BF16),
        jax.ShapeDtypeStruct((nb, tm, LANES), BF16),
        jax.ShapeDtypeStruct((nb, HEADS_W, tm), BF16),
        jax.ShapeDtypeStruct((nb, N_HEADS, LANES), F32),
        jax.ShapeDtypeStruct((nb, 1, LANES), F32),
    )
    out_specs = (ckv_spec, row(LANES), row(HEADS_W), row(D_MODEL), colT(HEADS_W),
                 colT(N_HEADS * ROPE_DIM), blk3(tm, HEADS_W), blk3(tm, LANES), blk3(HEADS_W, tm),
                 blk3(N_HEADS, LANES), blk3(1, LANES))
    in_specs = [row(D_MODEL)] + [_layer_spec(a, layer) for a in (g, w, gq, gkv, wuq, wukn, wuvt)] + [
        colT(LANES), colT(LANES), row(LANES), row(LANES), _full_spec(grp.shape)]
    prev, prev_specs, aliases = _alias_args(stack, len(in_specs), (0,))
    return pl.pallas_call(
        functools.partial(_skip_refs, _proj_mla_kernel, len(in_specs), len(prev)),
        grid=(nb,), in_specs=in_specs + prev_specs, out_specs=out_specs, out_shape=out_shape,
        input_output_aliases=aliases, compiler_params=_params(1), name="proj_mla",
    )(x, g, w, gq, gkv, wuq, wukn, wuvt, cosT, sinT, ctok, stok, grp, *prev)


FOX_IN_COLS = 2 * HEADS_W + LANES + HEADS_W + D_MODEL


def _proj_fox_kernel(x_ref, g_ref, w_ref, wfqt_ref, wfvt_ref, bf_ref, tri_ref, pk_ref, pqt_ref, grp_ref,
                     fk_ref, fv_ref, logf_ref, zb_ref, gb_ref, fqT_ref, fkb_ref, fvT_ref,
                     augk_ref, augqT_ref, base_ref, qn2_ref, kn2_ref, carry_ref):
    i = pl.program_id(0)
    tm = x_ref.shape[0]
    hb = _rms(x_ref[...], g_ref[0]).astype(BF16)
    o = 0
    fk = _dot(hb, w_ref[0, :, o:o + HEADS_W]); o += HEADS_W
    _put(fk_ref, fk)
    fkb_ref[0] = fk.astype(BF16)
    kn2_ref[0] = jnp.max(_dot((fk * fk).astype(BF16), grp_ref[...]), axis=0, keepdims=True)
    _put(fv_ref, _dot(hb, w_ref[0, :, o:o + HEADS_W])); o += HEADS_W
    zf = _dot(hb, w_ref[0, :, o:o + LANES]) + bf_ref[0]; o += LANES
    zb_ref[...] = _silu(_dot(hb, w_ref[0, :, o:o + HEADS_W])).astype(BF16); o += HEADS_W
    gb_ref[...] = _sigmoid(_dot(hb, w_ref[0, :, o:o + D_MODEL])).astype(BF16)
    fqT = _dot_nt(wfqt_ref[0], hb) * FOX_QSCALE
    fqT_ref[...] = fqT.astype(BF16)
    qn2 = jnp.sum((fqT * fqT).reshape(N_HEADS, HEAD_DV, tm), axis=1)
    qn2_ref[0] = jnp.broadcast_to(jnp.max(qn2, axis=1, keepdims=True), (N_HEADS, LANES))
    fvT_ref[0] = _dot_nt(wfvt_ref[0], hb).astype(BF16)

    lane = lax.broadcasted_iota(jnp.int32, (tm, LANES), 1)
    logf = jnp.minimum(zf, 0.0) - jnp.log(1.0 + jnp.exp(-jnp.abs(zf)))
    logf = jnp.where(lane < N_HEADS, logf, 0.0)
    logf_ref[...] = logf[:, 0:N_HEADS]

    hi, mid, lo = _split3(logf)
    tri = tri_ref[...]
    r = _dot(tri, hi) + _dot(tri, mid) + _dot(tri, lo)

    @pl.when(i == 0)
    def _():
        carry_ref[...] = jnp.zeros_like(carry_ref)

    base_ref[0] = carry_ref[...] * LOG2E
    carry_ref[...] = carry_ref[...] + r[tm - 1:tm, :]

    r2 = r * LOG2E
    rcat = jnp.concatenate(_split3(r2), axis=1)
    ecat = jnp.concatenate(_split3(r2[tm - 1:tm, :] - r2), axis=1)
    slot = lane % AUG_PER_HEAD
    ones_k = jnp.where(slot < 3, 1.0, 0.0)
    augk_ref[0] = (_dot(jnp.concatenate([rcat, ecat], axis=1), pk_ref[...]) + ones_k).astype(BF16)
    rowq = lax.broadcasted_iota(jnp.int32, (N_HEADS * AUG_PER_HEAD, tm), 0) % AUG_PER_HEAD
    ones_q = jnp.where((rowq >= 3) & (rowq < 9), 1.0, 0.0)
    augqT_ref[...] = (_dot_nt(pqt_ref[...], rcat) + ones_q).astype(BF16)


def _proj_fox(layer, x, g, w, wfqt, wfvt, bf, tri, pk, pqt, grp, tm, stack=None):
    m = x.shape[0]
    nb = m // tm
    row = lambda c: pl.BlockSpec((tm, c), lambda i: (i, 0))
    colT = lambda r: pl.BlockSpec((r, tm), lambda i: (0, i))
    blk3 = lambda a, b: pl.BlockSpec((1, a, b), lambda i: (i, 0, 0))
    kv_shape, kv_spec = _stacked(stack, m, tm, HEADS_W)
    out_shape = (
        kv_shape,
        kv_shape,
        jax.ShapeDtypeStruct((m, N_HEADS), F32),
        jax.ShapeDtypeStruct((m, HEADS_W), BF16),
        jax.ShapeDtypeStruct((m, D_MODEL), BF16),
        jax.ShapeDtypeStruct((HEADS_W, m), BF16),
        jax.ShapeDtypeStruct((nb, tm, HEADS_W), BF16),
        jax.ShapeDtypeStruct((nb, HEADS_W, tm), BF16),
        jax.ShapeDtypeStruct((nb, tm, LANES), BF16),
        jax.ShapeDtypeStruct((N_HEADS * AUG_PER_HEAD, m), BF16),
        jax.ShapeDtypeStruct((nb, 1, LANES), F32),
        jax.ShapeDtypeStruct((nb, N_HEADS, LANES), F32),
        jax.ShapeDtypeStruct((nb, 1, LANES), F32),
    )
    out_specs = (kv_spec, kv_spec, row(N_HEADS), row(HEADS_W), row(D_MODEL), colT(HEADS_W),
                 blk3(tm, HEADS_W), blk3(HEADS_W, tm), blk3(tm, LANES), colT(N_HEADS * AUG_PER_HEAD),
                 blk3(1, LANES), blk3(N_HEADS, LANES), blk3(1, LANES))
    in_specs = ([row(D_MODEL)] + [_layer_spec(a, layer) for a in (g, w, wfqt, wfvt, bf)]
                + [_full_spec(a.shape) for a in (tri, pk, pqt, grp)])
    prev, prev_specs, aliases = _alias_args(stack, len(in_specs), (0, 1))
    return pl.pallas_call(
        functools.partial(_skip_refs, _proj_fox_kernel, len(in_specs), len(prev)),
        grid=(nb,), in_specs=in_specs + prev_specs, out_specs=out_specs, out_shape=out_shape,
        scratch_shapes=[pltpu.VMEM((1, LANES), F32)],
        input_output_aliases=aliases, compiler_params=_params(1), name="proj_fox",
    )(x, g, w, wfqt, wfvt, bf, tri, pk, pqt, grp, *prev)


def _attn_kernel(tab_ref, qp_ref, qe_ref, ka_ref, ke_ref, vT_ref, o_ref, acc_ref, s_ref, *, fox):
    g = pl.program_id(0)
    i = pl.program_id(1)
    tq = qp_ref.shape[1]
    tk = ka_ref.shape[1]
    rows_p = lax.broadcasted_iota(jnp.int32, (LANES, tq), 0)
    rows_e = lax.broadcasted_iota(jnp.int32, qe_ref.shape, 0)
    kpos = lax.broadcasted_iota(jnp.int32, (tk, tq), 0)
    qpos = lax.broadcasted_iota(jnp.int32, (tk, tq), 1)
    if fox:
        visible = kpos <= qpos
    else:
        visible = (kpos // CHUNK) <= (qpos // CHUNK)
    qe = qe_ref[...]

    def head_weights(hq, to_tile_end):
        pair, hh = divmod(hq, 2)
        qp = qp_ref[pair * LANES:(pair + 1) * LANES, :]
        keep_p = (rows_p >= hh * HEAD_DV) & (rows_p < (hh + 1) * HEAD_DV)
        if fox:
            slot = rows_e - (HEADS_PER_STEP * g + hq) * AUG_PER_HEAD
            if to_tile_end:
                keep_e = ((slot >= 0) & (slot < 3)) | ((slot >= 6) & (slot < 9))
            else:
                keep_e = (slot >= 0) & (slot < 6)
            extra = jnp.where(keep_e, qe, jnp.zeros_like(qe))
        else:
            extra = qe[hq * ROPE_DIM:(hq + 1) * ROPE_DIM]
        parts = [jnp.where(keep_p, qp, jnp.zeros_like(qp)), extra]
        if extra.shape[0] < LANES:
            parts.append(jnp.zeros((LANES - extra.shape[0], tq), BF16))
        return jnp.concatenate(parts, axis=0)

    ws = [head_weights(hq, False) for hq in range(HEADS_PER_STEP)]
    ws_end = [head_weights(hq, True) for hq in range(HEADS_PER_STEP)] if fox else ws

    steps = [(u, hq) for u in range(tk // KEY_SUB) for hq in range(HEADS_PER_STEP)]
    ones_rows = jnp.ones((DEN_ROWS, KEY_SUB), BF16)

    def scores(j, step, to_tile_end=False):
        u, hq = step
        pair = hq // 2
        rows = slice(u * KEY_SUB, (u + 1) * KEY_SUB)
        lhs = jnp.concatenate([ka_ref[j, rows, pair * LANES:(pair + 1) * LANES], ke_ref[j, rows, :]], axis=1)
        return _dot(lhs, (ws_end if to_tile_end else ws)[hq])

    def block(j, stats, masked, j_next):
        stats = list(stats)
        n_steps = len(steps)
        tiles = {}
        for n, (u, hq) in enumerate(steps):
            ahead = n + LOOKAHEAD
            if ahead < n_steps:
                tiles[ahead] = scores(j, steps[ahead])
            elif j_next is not None:
                tiles[ahead] = scores(j_next, steps[ahead - n_steps])
            sT = tiles.pop(n) if n in tiles else s_ref[n]
            rows = slice(u * KEY_SUB, (u + 1) * KEY_SUB)
            m_old = stats[hq]
            if masked:
                sT = jnp.where(visible[rows], sT, NEG)
            if fox:
                h = HEADS_PER_STEP * g + hq
                d = tab_ref[0, h, i] - tab_ref[0, h, j]
            else:
                d = 0.0
            m_new = jnp.maximum(m_old, jnp.max(sT, axis=0, keepdims=True) + d)
            alpha = jnp.exp2(m_old - m_new)
            pT = jnp.exp2((sT - (m_new - d)).astype(BF16))
            stats[hq] = m_new
            v = jnp.concatenate([vT_ref[j, hq * HEAD_DV:(hq + 1) * HEAD_DV, rows], ones_rows], axis=0)
            acc_ref[hq] = alpha * acc_ref[hq] + _dot(v, pT)
        for n, tile in tiles.items():
            s_ref[n - n_steps] = tile
        return tuple(stats)

    def more(j_top, stats):
        alive = j_top >= 0
        if fox:
            jt = jnp.maximum(j_top, 0)
            slack = None
            for hq in range(HEADS_PER_STEP):
                h = HEADS_PER_STEP * g + hq
                bound = tab_ref[1, h, i] * tab_ref[2, h, jt] + (tab_ref[0, h, i] - tab_ref[0, h, jt + 1])
                room = stats[hq] - bound
                slack = room if slack is None else jnp.minimum(slack, room)
            alive = alive & (jnp.min(slack) < SKIP_LOG2)
        return alive.astype(jnp.int32)

    def block_unshifted(j, lows, masked, j_next):
        lows = list(lows)
        n_steps = len(steps)
        tiles = {}
        for n, (u, hq) in enumerate(steps):
            ahead = n + LOOKAHEAD
            if ahead < n_steps:
                tiles[ahead] = scores(j, steps[ahead], not masked)
            else:
                tiles[ahead] = scores(j_next, steps[ahead - n_steps], True)
            sT = tiles.pop(n) if n in tiles else s_ref[n]
            rows = slice(u * KEY_SUB, (u + 1) * KEY_SUB)
            if masked:
                sT = jnp.where(visible[rows], sT, NEG)
                if fox:
                    lows[hq] = jnp.maximum(lows[hq], jnp.max(sT, axis=0, keepdims=True))
            pT = jnp.exp2(sT).astype(BF16)
            v = jnp.concatenate([vT_ref[j, hq * HEAD_DV:(hq + 1) * HEAD_DV, rows], ones_rows], axis=0)
            pv = _dot(v, pT)
            if fox and not masked:
                h = HEADS_PER_STEP * g + hq
                d_end = tab_ref[0, h, i] - tab_ref[0, h, j + 1]
                lows[hq] = jnp.maximum(lows[hq], jnp.max(sT, axis=0, keepdims=True) + d_end)
                pv = pv * jnp.exp2(jnp.full((1, 1), d_end, F32))
            acc_ref[hq] = acc_ref[hq] + pv
        for n, tile in tiles.items():
            s_ref[n - n_steps] = tile
        return tuple(lows)

    def walk_online():
        m0 = jnp.full((1, tq), NEG, F32)
        stats = block(i, (m0,) * HEADS_PER_STEP, True, jnp.maximum(i - 1, 0))

        def visit(c):
            j = c[0]
            st = block(j, c[2:], False, jnp.maximum(j - 1, 0))
            return (j - 1, more(j - 1, st)) + tuple(st)

        lax.while_loop(lambda c: c[1] > 0, visit, (i - 1, more(i - 1, stats)) + tuple(stats))

    def walk_unshifted():
        m0 = jnp.full((1, tq), NEG, F32)
        lows = block_unshifted(i, (m0,) * HEADS_PER_STEP, True, jnp.maximum(i - 1, 0))
        if fox:
            def visit(c):
                j = c[0]
                lo = block_unshifted(j, c[2:], False, jnp.maximum(j - 1, 0))
                return (j - 1, more(j - 1, lo)) + tuple(lo)

            lax.while_loop(lambda c: c[1] > 0, visit, (i - 1, more(i - 1, lows)) + tuple(lows))
        else:
            def visit(k, c):
                j = i - 1 - k
                block_unshifted(j, lows, False, jnp.maximum(j - 1, 0))
                return c

            lax.fori_loop(0, i, visit, 0)

    acc_ref[...] = jnp.zeros_like(acc_ref)
    for n in range(LOOKAHEAD):
        s_ref[n] = scores(i, steps[n])
    bounded = None
    for hq in range(HEADS_PER_STEP):
        h = HEADS_PER_STEP * g + hq
        ok = tab_ref[1, h, i] * tab_ref[2, h, i] <= SAFE_LOG2
        bounded = ok if bounded is None else bounded & ok
    pl.when(bounded)(walk_unshifted)
    pl.when(jnp.logical_not(bounded))(walk_online)
    outs = [acc_ref[hq, 0:HEAD_DV, :] / acc_ref[hq, HEAD_DV:HEAD_DV + 1, :] for hq in range(HEADS_PER_STEP)]
    o_ref[...] = jnp.concatenate(outs, axis=0).T.astype(BF16)


def _attention(tab, qpT, qeT, ka, ke, vT, fox):
    nb, tk, _ = ka.shape
    s = qpT.shape[1]
    tq = tk
    width = HEADS_PER_STEP * HEAD_DV
    e_rows = qeT.shape[0] if fox else HEADS_PER_STEP * ROPE_DIM
    e_map = (lambda g, i, b: (0, i)) if fox else (lambda g, i, b: (g, i))
    once = dict(pipeline_mode=pl.Buffered(1))
    grid_spec = pltpu.PrefetchScalarGridSpec(
        num_scalar_prefetch=1,
        grid=(N_HEADS // HEADS_PER_STEP, nb),
        in_specs=[
            pl.BlockSpec((width, tq), lambda g, i, b: (g, i)),
            pl.BlockSpec((e_rows, tq), e_map),
            pl.BlockSpec((nb, tk, width), lambda g, i, b: (0, 0, g), **once),
            pl.BlockSpec((nb, tk, LANES), lambda g, i, b: (0, 0, 0), **once),
            pl.BlockSpec((nb, width, tk), lambda g, i, b: (0, g, 0), **once),
        ],
        out_specs=pl.BlockSpec((tq, width), lambda g, i, b: (i, g)),
        scratch_shapes=[pltpu.VMEM((HEADS_PER_STEP, HEAD_DV + DEN_ROWS, tq), F32),
                        pltpu.VMEM((LOOKAHEAD, KEY_SUB, tq), F32)],
    )
    return pl.pallas_call(
        functools.partial(_attn_kernel, fox=fox), grid_spec=grid_spec,
        out_shape=jax.ShapeDtypeStruct((s, HEADS_W), BF16),
        compiler_params=_params(2), name="attn_fox" if fox else "attn_mla",
    )(tab, qpT, qeT, ka, ke, vT)


def _softmax_pv(s, v):
    m = jnp.max(s, axis=-1, keepdims=True)
    p = jnp.exp2(s - m)
    l = jnp.sum(p, axis=-1, keepdims=True)
    return _dot(p.astype(BF16), v) / l


def _diag_heads(o_big, t_new):
    rows = lax.broadcasted_iota(jnp.int32, o_big.shape, 0) // t_new
    cols = lax.broadcasted_iota(jnp.int32, o_big.shape, 1) // HEAD_DV
    kept = jnp.where(rows == cols, o_big, 0.0)
    return jnp.sum(kept.reshape(N_HEADS, t_new, o_big.shape[1]), axis=0)


def _sample_attn_kernel(qn_ref, qr_ref, qf_ref, lrep_ref, cckv_ref, nckv_ref, ckr_ref, nkr_ref,
                        cfk_ref, nfk_ref, cfv_ref, nfv_ref, wukn_ref, wuv_ref, oa_ref, ob_ref):
    past = cckv_ref.shape[2]
    t_new = nckv_ref.shape[0]
    rows = N_HEADS * t_new
    pad = lrep_ref.shape[2] - past - t_new
    keys = past + t_new + pad

    def with_new(cache, new):
        return jnp.concatenate([cache, new, jnp.zeros((pad, new.shape[1]), new.dtype)], axis=0)

    kidx = lax.broadcasted_iota(jnp.int32, (rows, keys), 1)
    qpos = past + lax.broadcasted_iota(jnp.int32, (rows, keys), 0) % t_new
    real = kidx < past + t_new

    ckv_all = with_new(cckv_ref[0, 0], nckv_ref[...]).astype(BF16)
    kn = _dot(ckv_all, wukn_ref[0]).astype(BF16)
    va = _dot(ckv_all, wuv_ref[0]).astype(BF16)
    kr = with_new(ckr_ref[0, 0], nkr_ref[:, 0:ROPE_DIM]).astype(BF16)
    s = _dot_nt(qn_ref[0], kn) + _dot_nt(qr_ref[0], kr)
    s = jnp.where(real & ((kidx // CHUNK) <= (qpos // CHUNK)), s, NEG)
    oa_ref[...] = _diag_heads(_softmax_pv(s, va), t_new).astype(BF16)

    fk = with_new(cfk_ref[0, 0], nfk_ref[...]).astype(BF16)
    fv = with_new(cfv_ref[0, 0], nfv_ref[...]).astype(BF16)
    c = lrep_ref[0]
    shift = 1
    while shift < keys:
        c = c + jnp.where(kidx >= shift, pltpu.roll(c, shift, 1), 0.0)
        shift *= 2
    cq = jnp.sum(jnp.where(kidx == qpos, c, 0.0), axis=-1, keepdims=True)
    s = _dot_nt(qf_ref[0], fk) + (cq - c) * LOG2E
    s = jnp.where(real & (kidx <= qpos), s, NEG)
    ob_ref[...] = _diag_heads(_softmax_pv(s, fv), t_new).astype(BF16)


def _sample_attention(layer, qn_bd, qr, qf_bd, lrep, cckv, nckv, ckr, nkr, cfk, nfk, cfv, nfv, wukn, wuv):
    nbatch = cckv.shape[1]
    t_new = nckv.shape[0] // nbatch
    b3 = lambda a: pl.BlockSpec((1,) + a.shape[1:], lambda b: (b, 0, 0))
    cache = lambda a: pl.BlockSpec((1, 1) + a.shape[2:], lambda b: (layer, b, 0, 0))
    new = lambda a: pl.BlockSpec((t_new, a.shape[1]), lambda b: (b, 0))
    in_specs = [b3(qn_bd), b3(qr), b3(qf_bd), b3(lrep), cache(cckv), new(nckv), cache(ckr), new(nkr),
                cache(cfk), new(nfk), cache(cfv), new(nfv), _layer_spec(wukn, layer), _layer_spec(wuv, layer)]
    out = jax.ShapeDtypeStruct((nbatch * t_new, HEADS_W), BF16)
    o_spec = pl.BlockSpec((t_new, HEADS_W), lambda b: (b, 0))
    return pl.pallas_call(
        _sample_attn_kernel, grid=(nbatch,), in_specs=in_specs, out_specs=(o_spec, o_spec),
        out_shape=(out, out), compiler_params=_params(1), name="attn_sample",
    )(qn_bd, qr, qf_bd, lrep, cckv, nckv, ckr, nkr, cfk, nfk, cfv, nfv, wukn, wuv)


def _merge_kernel(x_ref, oa_ref, za_ref, ob_ref, zb_ref, ga_ref, gb_ref, woa_ref, wob_ref, wout_ref,
                  fg_ref, xn_ref, *maybe_y_ref):
    a = _dot(oa_ref[...] * za_ref[...], woa_ref[0])
    b = _dot(ob_ref[...] * zb_ref[...], wob_ref[0])
    mix = ga_ref[...].astype(F32) * a + gb_ref[...].astype(F32) * b
    xn = x_ref[...] + _dot(mix.astype(BF16), wout_ref[0])
    xn_ref[...] = xn
    if maybe_y_ref:
        maybe_y_ref[0][...] = _rms(xn, fg_ref[...])


def _merge(layer, x, oa, za, ob, zb, ga, gb, woa, wob, wout, fg, tm, final):
    m = x.shape[0]
    row = lambda c: pl.BlockSpec((tm, c), lambda i: (i, 0))
    in_specs = [row(D_MODEL), row(HEADS_W), row(HEADS_W), row(HEADS_W), row(HEADS_W), row(D_MODEL),
                row(D_MODEL), _layer_spec(woa, layer), _layer_spec(wob, layer), _layer_spec(wout, layer),
                _full_spec(fg.shape)]
    n_out = 2 if final else 1
    out_shape = tuple(jax.ShapeDtypeStruct((m, D_MODEL), F32) for _ in range(n_out))
    out_specs = tuple(row(D_MODEL) for _ in range(n_out))
    return pl.pallas_call(
        _merge_kernel, grid=(m // tm,), in_specs=in_specs, out_specs=out_specs, out_shape=out_shape,
        compiler_params=_params(1), name="merge",
    )(x, oa, za, ob, zb, ga, gb, woa, wob, wout, fg)


def _pad_cols(w, width):
    return jnp.pad(w, ((0, 0), (0, width - w.shape[1])))


def _pad_last(w, width):
    return jnp.pad(w, [(0, 0)] * (w.ndim - 1) + [(0, width - w.shape[-1])])


def _stacked_weights(w_in, w_uq, w_ukv, w_oa, w_ob, w_out, b_f, norm_g, g_q, g_kv):
    depth = w_in.shape[0]
    offs = np.cumsum(IN_SIZES)[:-1].tolist()
    w_cq, w_ckv, w_kr, w_za, w_fq, w_fk, w_fv, w_zf, w_zb, w_ga, w_gb = jnp.split(w_in, offs, axis=2)
    w_kr_swapped = jnp.concatenate([w_kr[..., HALF_ROPE:], w_kr[..., :HALF_ROPE]], axis=2)
    w_mla = jnp.concatenate([w_cq, w_ckv, _pad_last(w_kr, LANES), _pad_last(w_kr_swapped, LANES),
                             w_za, w_ga], axis=2).astype(BF16)
    w_fox = jnp.concatenate([w_fk, w_fv, _pad_last(w_zf, LANES), w_zb, w_gb], axis=2).astype(BF16)
    uq = w_uq.reshape(depth, Q_LORA, N_HEADS, NOPE_DIM + ROPE_DIM)
    uq_rows = jnp.concatenate([
        uq[..., :NOPE_DIM].reshape(depth, Q_LORA, -1),
        uq[..., NOPE_DIM:NOPE_DIM + HALF_ROPE].reshape(depth, Q_LORA, -1),
        uq[..., NOPE_DIM + HALF_ROPE:].reshape(depth, Q_LORA, -1)], axis=2)
    ukv = w_ukv.reshape(depth, KV_LORA, N_HEADS, NOPE_DIM + HEAD_DV)
    w_ukn = ukv[..., :NOPE_DIM].reshape(depth, KV_LORA, -1).astype(BF16)
    w_uv = ukv[..., NOPE_DIM:].reshape(depth, KV_LORA, -1).astype(BF16)
    tr = lambda a: jnp.swapaxes(a, 1, 2)
    return dict(
        w_mla=w_mla, w_fox=w_fox, w_uqT=tr(uq_rows).astype(BF16), w_ukn=w_ukn, w_uv=w_uv, w_uvT=tr(w_uv),
        w_fqT=tr(w_fq).astype(BF16), w_fvT=tr(w_fv).astype(BF16),
        b_f=_pad_last(b_f[:, None, :], LANES),
        w_oa=w_oa.astype(BF16), w_ob=w_ob.astype(BF16), w_out=w_out.astype(BF16),
        norm_g=norm_g[:, None, :], g_q=g_q[:, None, :], g_kv=g_kv[:, None, :])


def _rope_tables(pos):
    inv = jnp.exp(-math.log(ROPE_THETA) * jnp.arange(HALF_ROPE, dtype=F32) / HALF_ROPE)
    ang = pos.astype(F32)[:, None] * inv[None, :]
    cos = jnp.cos(ang)
    sin = jnp.sin(ang)
    cosT = jnp.tile(cos.T, (N_HEADS, 1))
    sinT = jnp.tile(sin.T, (N_HEADS, 1))
    ctok = _pad_cols(jnp.concatenate([cos, cos], axis=1), LANES)
    stok = _pad_cols(jnp.concatenate([-sin, sin], axis=1), LANES)
    return cosT, sinT, ctok, stok


def _bias_placement():
    pk = np.zeros((6 * LANES, LANES), np.float32)
    pqt = np.zeros((N_HEADS * AUG_PER_HEAD, 3 * LANES), np.float32)
    for c in range(3):
        for h in range(N_HEADS):
            pk[c * LANES + h, h * AUG_PER_HEAD + 3 + c] = -1.0
            pk[(3 + c) * LANES + h, h * AUG_PER_HEAD + 6 + c] = 1.0
            pqt[h * AUG_PER_HEAD + c, c * LANES + h] = 1.0
    return jnp.asarray(pk, BF16), jnp.asarray(pqt, BF16)


def _head_groups():
    grp = np.zeros((HEADS_W, LANES), np.float32)
    for h in range(N_HEADS):
        grp[h * HEAD_DV:(h + 1) * HEAD_DV, h] = 1.0
    return jnp.asarray(grp, BF16)


def _skip_tables(base, qn2, kn2):
    b = base[:, 0, :N_HEADS]
    qmax = jnp.sqrt(qn2[:, :, 0]) * NORM_SLACK
    kmax = lax.cummax(jnp.sqrt(kn2[:, 0, :N_HEADS]) * NORM_SLACK, axis=0)
    return jnp.stack([b, qmax, kmax]).transpose(0, 2, 1)


def _tri(n):
    return jnp.asarray(np.tril(np.ones((n, n), np.float32)), BF16)


def _block_diag_queries(qT, nbatch, t_new):
    q = qT.T.reshape(nbatch, t_new, N_HEADS, HEAD_DV)
    eye = jnp.eye(N_HEADS, dtype=q.dtype)
    return jnp.einsum('bthj,hg->bhtgj', q, eye).reshape(nbatch, N_HEADS * t_new, HEADS_W)


def _project(layer, x, sw, tables, tri, pk, pqt, grp, tm, stack_mla=None, stack_fox=None):
    cosT, sinT, ctok, stok = tables
    mla = _proj_mla(layer, x, sw['norm_g'], sw['w_mla'], sw['g_q'], sw['g_kv'], sw['w_uqT'], sw['w_ukn'],
                    sw['w_uvT'], cosT, sinT, ctok, stok, grp, tm, stack_mla)
    fox = _proj_fox(layer, x, sw['norm_g'], sw['w_fox'], sw['w_fqT'], sw['w_fvT'], sw['b_f'], tri, pk, pqt, grp,
                    tm, stack_fox)
    return mla, fox


def kernel(x_prompt, x_sample, cache_mla_ckv, cache_mla_krope, cache_fox_k, cache_fox_v, cache_fox_logf,
           norm_g, w_in, g_q, w_uq, g_kv, w_ukv, b_f, w_oa, w_ob, w_out, final_g):
    depth = w_in.shape[0]
    _, seq, _ = x_prompt.shape
    nbatch, t_new, _ = x_sample.shape
    past = cache_mla_ckv.shape[2]
    m_s = nbatch * t_new
    key_pad = -(past + t_new) % LANES

    tabs_p = _rope_tables(jnp.arange(seq, dtype=jnp.int32))
    tabs_s = _rope_tables(past + jnp.arange(m_s, dtype=jnp.int32) % t_new)
    pk, pqt = _bias_placement()
    grp = _head_groups()
    tri_p = _tri(SEQ_BLOCK)
    tri_s = _tri(m_s)
    fg = final_g[None, :]

    xp = x_prompt.reshape(seq, D_MODEL)
    xs = x_sample.reshape(m_s, D_MODEL)
    outs = {k: [] for k in ('p_kr', 'p_lf', 's_ckv', 's_kr', 's_fk', 's_fv', 's_lf')}
    p_ckv_stack = p_kv_stack = None
    yp = ys = None
    sw = _stacked_weights(w_in, w_uq, w_ukv, w_oa, w_ob, w_out, b_f, norm_g, g_q, g_kv)
    for l in range(depth):
        final = l == depth - 1

        mla, fox = _project(l, xp, sw, tabs_p, tri_p, pk, pqt, grp, SEQ_BLOCK,
                            (l, depth, p_ckv_stack), (l, depth, p_kv_stack))
        ckv, kr, za, ga, qnT, qrT, knb, krb, vT, qn2_a, kn2_a = mla
        fk, fv, logf, zb, gb, fqT, fkb, fvT, augk, augqT, base, qn2, kn2 = fox
        tab = _skip_tables(base, qn2, kn2)
        tab_a = _skip_tables(jnp.zeros_like(base), qn2_a, kn2_a)
        o_a = _attention(tab_a, qnT, qrT, knb, krb, vT, fox=False)
        o_b = _attention(tab, fqT, augqT, fkb, augk, fvT, fox=True)
        res = _merge(l, xp, o_a, za, o_b, zb, ga, gb, sw['w_oa'], sw['w_ob'], sw['w_out'], fg, SEQ_BLOCK, final)
        xp = res[0]
        if final:
            yp = res[1]
        p_ckv_stack, p_kv_stack = (ckv,), (fk, fv)
        outs['p_kr'].append(kr[:, :ROPE_DIM]); outs['p_lf'].append(logf)

        mla, fox = _project(l, xs, sw, tabs_s, tri_s, pk, pqt, grp, m_s)
        ckv, kr, za, ga, qnT, qrT = mla[:6]
        fk, fv, logf, zb, gb, fqT = fox[:6]
        qn_bd = _block_diag_queries(qnT, nbatch, t_new)
        qf_bd = _block_diag_queries(fqT, nbatch, t_new)
        qr = qrT.reshape(N_HEADS, ROPE_DIM, nbatch, t_new).transpose(2, 0, 3, 1).reshape(
            nbatch, N_HEADS * t_new, ROPE_DIM)
        lf_all = jnp.concatenate([cache_fox_logf[l], logf.reshape(nbatch, t_new, N_HEADS),
                                  jnp.zeros((nbatch, key_pad, N_HEADS), F32)], axis=1)
        lrep = jnp.repeat(lf_all.transpose(0, 2, 1), t_new, axis=1)
        o_a, o_b = _sample_attention(
            l, qn_bd, qr, qf_bd, lrep, cache_mla_ckv, ckv, cache_mla_krope, kr,
            cache_fox_k.reshape(depth, nbatch, past, HEADS_W), fk,
            cache_fox_v.reshape(depth, nbatch, past, HEADS_W), fv, sw['w_ukn'], sw['w_uv'])
        res = _merge(l, xs, o_a, za, o_b, zb, ga, gb, sw['w_oa'], sw['w_ob'], sw['w_out'], fg, m_s, final)
        xs = res[0]
        if final:
            ys = res[1]
        outs['s_ckv'].append(ckv); outs['s_kr'].append(kr[:, :ROPE_DIM]); outs['s_fk'].append(fk)
        outs['s_fv'].append(fv); outs['s_lf'].append(logf)

    st = lambda name, shape: jnp.stack(outs[name]).reshape((depth,) + shape)
    return (yp.reshape(1, seq, D_MODEL), ys.reshape(nbatch, t_new, D_MODEL),
            p_ckv_stack[0].reshape(depth, 1, seq, KV_LORA), st('p_kr', (1, seq, ROPE_DIM)),
            p_kv_stack[0].reshape(depth, 1, seq, N_HEADS, HEAD_DV),
            p_kv_stack[1].reshape(depth, 1, seq, N_HEADS, HEAD_DV),
            st('p_lf', (1, seq, N_HEADS)),
            st('s_ckv', (nbatch, t_new, KV_LORA)), st('s_kr', (nbatch, t_new, ROPE_DIM)),
            st('s_fk', (nbatch, t_new, N_HEADS, HEAD_DV)), st('s_fv', (nbatch, t_new, N_HEADS, HEAD_DV)),
            st('s_lf', (nbatch, t_new, N_HEADS)))
```

```python
import functools
import math

import numpy as np
import jax
import jax.numpy as jnp
from jax import lax
from jax.experimental import pallas as pl
from jax.experimental.pallas import tpu as pltpu

D_MODEL = 1024
N_HEADS = 8
NOPE_DIM = 64
ROPE_DIM = 32
HALF_ROPE = ROPE_DIM // 2
HEAD_DV = 64
Q_LORA = 384
KV_LORA = 256
HEADS_W = N_HEADS * HEAD_DV
CHUNK = 64
ROPE_THETA = 10000.0
LOG2E = math.log2(math.e)
MLA_QSCALE = LOG2E / math.sqrt(NOPE_DIM + ROPE_DIM)
FOX_QSCALE = LOG2E / math.sqrt(HEAD_DV)
EPS = 1e-6
NEG = -1e30
IN_SIZES = (Q_LORA, KV_LORA, ROPE_DIM, HEADS_W, HEADS_W, HEADS_W, HEADS_W, N_HEADS, HEADS_W, D_MODEL, D_MODEL)

LANES = 128
SEQ_BLOCK = 512
KEY_SUB = 256
DEN_ROWS = 16
HEADS_PER_STEP = 8
LOOKAHEAD = 2
SAFE_LOG2 = 100.0
SKIP_LOG2 = 64.0
NORM_SLACK = 1.01
AUG_PER_HEAD = 8
VMEM_LIMIT_BYTES = 56 * 1024 * 1024

F32 = jnp.float32
BF16 = jnp.bfloat16


def _dot(a, b):
    return jnp.dot(a, b, preferred_element_type=F32)


def _dot_nt(a, b):
    return lax.dot_general(a, b, (((1,), (1,)), ((), ())), preferred_element_type=F32)


def _rms(x, g):
    return x * lax.rsqrt(jnp.mean(x * x, axis=-1, keepdims=True) + EPS) * g


def _sigmoid(x):
    return 1.0 / (1.0 + jnp.exp(-x))


def _silu(x):
    return x * _sigmoid(x)


def _split3(x):
    hi = x.astype(BF16)
    r1 = x - hi.astype(F32)
    mid = r1.astype(BF16)
    lo = (r1 - mid.astype(F32)).astype(BF16)
    return hi, mid, lo


def _full_spec(shape):
    nd = len(shape)
    return pl.BlockSpec(shape, lambda *_: (0,) * nd)


def _layer_spec(arr, layer):
    nd = arr.ndim
    return pl.BlockSpec((1,) + arr.shape[1:], lambda *_: (layer,) + (0,) * (nd - 1))


def _put(ref, val):
    ref[...] = val.reshape(ref.shape)


def _skip_refs(body, n_in, n_skip, *refs):
    return body(*refs[:n_in], *refs[n_in + n_skip:])


def _stacked(stack, m, tm, cols):
    if stack is None:
        return jax.ShapeDtypeStruct((m, cols), F32), pl.BlockSpec((tm, cols), lambda i: (i, 0))
    layer, depth, _ = stack
    return (jax.ShapeDtypeStruct((depth, m, cols), F32),
            pl.BlockSpec((1, tm, cols), lambda i: (layer, i, 0)))


def _alias_args(stack, n_in, out_indices):
    if stack is None or stack[2] is None:
        return [], [], {}
    prev = list(stack[2])
    specs = [pl.BlockSpec(memory_space=pl.ANY) for _ in prev]
    return prev, specs, {n_in + k: out for k, out in enumerate(out_indices)}


def _params(n_axes):
    return pltpu.CompilerParams(dimension_semantics=("arbitrary",) * n_axes,
                                vmem_limit_bytes=VMEM_LIMIT_BYTES)


MLA_IN_COLS = Q_LORA + KV_LORA + 2 * LANES + HEADS_W + D_MODEL


def _proj_mla_kernel(x_ref, g_ref, w_ref, gq_ref, gkv_ref, wuq_ref, wukn_ref, wuvt_ref,
                     cosT_ref, sinT_ref, ctok_ref, stok_ref, grp_ref,
                     ckv_ref, kr_ref, za_ref, ga_ref, qnT_ref, qrT_ref, knb_ref, krb_ref, vT_ref,
                     qn2_ref, kn2_ref):
    tm = x_ref.shape[0]
    hb = _rms(x_ref[...], g_ref[0]).astype(BF16)
    o = 0
    zcq = _dot(hb, w_ref[0, :, o:o + Q_LORA]); o += Q_LORA
    zckv = _dot(hb, w_ref[0, :, o:o + KV_LORA]); o += KV_LORA
    zka = _dot(hb, w_ref[0, :, o:o + LANES]); o += LANES
    zkb = _dot(hb, w_ref[0, :, o:o + LANES]); o += LANES
    za_ref[...] = _silu(_dot(hb, w_ref[0, :, o:o + HEADS_W])).astype(BF16); o += HEADS_W
    ga_ref[...] = _sigmoid(_dot(hb, w_ref[0, :, o:o + D_MODEL])).astype(BF16)

    cqb = _rms(zcq, gq_ref[0]).astype(BF16)
    qT = _dot_nt(wuq_ref[0], cqb)
    qn = qT[0:HEADS_W] * MLA_QSCALE
    qnT_ref[...] = qn.astype(BF16)
    x1 = qT[HEADS_W:HEADS_W + LANES]
    x2 = qT[HEADS_W + LANES:HEADS_W + 2 * LANES]
    c = cosT_ref[...]
    s = sinT_ref[...]
    r1 = (x1 * c - x2 * s) * MLA_QSCALE
    r2 = (x1 * s + x2 * c) * MLA_QSCALE
    n1 = r1.astype(BF16)
    n2 = r2.astype(BF16)
    for h in range(N_HEADS):
        qrT_ref[h * ROPE_DIM:h * ROPE_DIM + HALF_ROPE, :] = n1[h * HALF_ROPE:(h + 1) * HALF_ROPE]
        qrT_ref[h * ROPE_DIM + HALF_ROPE:(h + 1) * ROPE_DIM, :] = n2[h * HALF_ROPE:(h + 1) * HALF_ROPE]
    qn2 = (jnp.sum((qn * qn).reshape(N_HEADS, NOPE_DIM, tm), axis=1)
           + jnp.sum((r1 * r1 + r2 * r2).reshape(N_HEADS, HALF_ROPE, tm), axis=1))
    qn2_ref[0] = jnp.broadcast_to(jnp.max(qn2, axis=1, keepdims=True), (N_HEADS, LANES))

    ckv = _rms(zckv, gkv_ref[0])
    _put(ckv_ref, ckv)
    cb = ckv.astype(BF16)
    kn = _dot(cb, wukn_ref[0])
    knb_ref[0] = kn.astype(BF16)
    vT_ref[0] = _dot_nt(wuvt_ref[0], cb).astype(BF16)

    kr = zka * ctok_ref[...] + zkb * stok_ref[...]
    kr_ref[...] = kr
    krb_ref[0] = kr.astype(BF16)
    kn2 = _dot((kn * kn).astype(BF16), grp_ref[...]) + jnp.sum(kr * kr, axis=1, keepdims=True)
    kn2_ref[0] = jnp.max(kn2, axis=0, keepdims=True)


def _proj_mla(layer, x, g, w, gq, gkv, wuq, wukn, wuvt, cosT, sinT, ctok, stok, grp, tm, stack=None):
    m = x.shape[0]
    nb = m // tm
    row = lambda c: pl.BlockSpec((tm, c), lambda i: (i, 0))
    colT = lambda r: pl.BlockSpec((r, tm), lambda i: (0, i))
    blk3 = lambda a, b: pl.BlockSpec((1, a, b), lambda i: (i, 0, 0))
    ckv_shape, ckv_spec = _stacked(stack, m, tm, KV_LORA)
    out_shape = (
        ckv_shape,
        jax.ShapeDtypeStruct((m, LANES), F32),
        jax.ShapeDtypeStruct((m, HEADS_W), BF16),
        jax.ShapeDtypeStruct((m, D_MODEL), BF16),
        jax.ShapeDtypeStruct((HEADS_W, m), BF16),
        jax.ShapeDtypeStruct((N_HEADS * ROPE_DIM, m), BF16),
        jax.ShapeDtypeStruct((nb, tm, HEADS_W), BF16),
        jax.ShapeDtypeStruct((nb, tm, LANES), BF16),
        jax.ShapeDtypeStruct((nb, HEADS_W, tm), BF16),
        jax.ShapeDtypeStruct((nb, N_HEADS, LANES), F32),
        jax.ShapeDtypeStruct((nb, 1, LANES), F32),
    )
    out_specs = (ckv_spec, row(LANES), row(HEADS_W), row(D_MODEL), colT(HEADS_W),
                 colT(N_HEADS * ROPE_DIM), blk3(tm, HEADS_W), blk3(tm, LANES), blk3(HEADS_W, tm),
                 blk3(N_HEADS, LANES), blk3(1, LANES))
    in_specs = [row(D_MODEL)] + [_layer_spec(a, layer) for a in (g, w, gq, gkv, wuq, wukn, wuvt)] + [
        colT(LANES), colT(LANES), row(LANES), row(LANES), _full_spec(grp.shape)]
    prev, prev_specs, aliases = _alias_args(stack, len(in_specs), (0,))
    return pl.pallas_call(
        functools.partial(_skip_refs, _proj_mla_kernel, len(in_specs), len(prev)),
        grid=(nb,), in_specs=in_specs + prev_specs, out_specs=out_specs, out_shape=out_shape,
        input_output_aliases=aliases, compiler_params=_params(1), name="proj_mla",
    )(x, g, w, gq, gkv, wuq, wukn, wuvt, cosT, sinT, ctok, stok, grp, *prev)


FOX_IN_COLS = 2 * HEADS_W + LANES + HEADS_W + D_MODEL


def _proj_fox_kernel(x_ref, g_ref, w_ref, wfqt_ref, bf_ref, tri_ref, pk_ref, pqt_ref, grp_ref,
                     fk_ref, fv_ref, logf_ref, zb_ref, gb_ref, fqT_ref, fkb_ref, fvT_ref,
                     augk_ref, augqT_ref, base_ref, qn2_ref, kn2_ref, carry_ref):
    i = pl.program_id(0)
    tm = x_ref.shape[0]
    hb = _rms(x_ref[...], g_ref[0]).astype(BF16)
    o = 0
    fk = _dot(hb, w_ref[0, :, o:o + HEADS_W]); o += HEADS_W
    _put(fk_ref, fk)
    fkb_ref[0] = fk.astype(BF16)
    kn2_ref[0] = jnp.max(_dot((fk * fk).astype(BF16), grp_ref[...]), axis=0, keepdims=True)
    fv = _dot(hb, w_ref[0, :, o:o + HEADS_W]); o += HEADS_W
    _put(fv_ref, fv)
    fvT_ref[0] = fv.T.astype(BF16)
    zf = _dot(hb, w_ref[0, :, o:o + LANES]) + bf_ref[0]; o += LANES
    zb_ref[...] = _silu(_dot(hb, w_ref[0, :, o:o + HEADS_W])).astype(BF16); o += HEADS_W
    gb_ref[...] = _sigmoid(_dot(hb, w_ref[0, :, o:o + D_MODEL])).astype(BF16)
    fqT = _dot_nt(wfqt_ref[0], hb) * FOX_QSCALE
    fqT_ref[...] = fqT.astype(BF16)
    qn2 = jnp.sum((fqT * fqT).reshape(N_HEADS, HEAD_DV, tm), axis=1)
    qn2_ref[0] = jnp.broadcast_to(jnp.max(qn2, axis=1, keepdims=True), (N_HEADS, LANES))

    lane = lax.broadcasted_iota(jnp.int32, (tm, LANES), 1)
    logf = jnp.minimum(zf, 0.0) - jnp.log(1.0 + jnp.exp(-jnp.abs(zf)))
    logf = jnp.where(lane < N_HEADS, logf, 0.0)
    logf_ref[...] = logf[:, 0:N_HEADS]

    hi, mid, lo = _split3(logf)
    packed = (hi.astype(F32) + pltpu.roll(mid.astype(F32), N_HEADS, 1)
              + pltpu.roll(lo.astype(F32), 2 * N_HEADS, 1)).astype(BF16)
    cum3 = _dot(tri_ref[...], packed)
    r = cum3 + pltpu.roll(cum3, LANES - N_HEADS, 1) + pltpu.roll(cum3, LANES - 2 * N_HEADS, 1)
    r = jnp.where(lane < N_HEADS, r, 0.0)

    @pl.when(i == 0)
    def _():
        carry_ref[...] = jnp.zeros_like(carry_ref)

    base_ref[0] = carry_ref[...] * LOG2E
    carry_ref[...] = carry_ref[...] + r[tm - 1:tm, :]

    rh, rm, rl = _split3(r * LOG2E)
    rcat = jnp.concatenate([rh, rm, rl], axis=1)
    slot = lane % AUG_PER_HEAD
    ones_k = jnp.where((lane < N_HEADS * AUG_PER_HEAD) & (slot < 3), 1.0, 0.0)
    augk_ref[0] = (_dot(rcat, pk_ref[...]) + ones_k).astype(BF16)
    rowq = lax.broadcasted_iota(jnp.int32, (N_HEADS * AUG_PER_HEAD, tm), 0) % AUG_PER_HEAD
    ones_q = jnp.where((rowq >= 3) & (rowq < 6), 1.0, 0.0)
    augqT_ref[...] = (_dot_nt(pqt_ref[...], rcat) + ones_q).astype(BF16)


def _proj_fox(layer, x, g, w, wfqt, bf, tri, pk, pqt, grp, tm, stack=None):
    m = x.shape[0]
    nb = m // tm
    row = lambda c: pl.BlockSpec((tm, c), lambda i: (i, 0))
    colT = lambda r: pl.BlockSpec((r, tm), lambda i: (0, i))
    blk3 = lambda a, b: pl.BlockSpec((1, a, b), lambda i: (i, 0, 0))
    kv_shape, kv_spec = _stacked(stack, m, tm, HEADS_W)
    out_shape = (
        kv_shape,
        kv_shape,
        jax.ShapeDtypeStruct((m, N_HEADS), F32),
        jax.ShapeDtypeStruct((m, HEADS_W), BF16),
        jax.ShapeDtypeStruct((m, D_MODEL), BF16),
        jax.ShapeDtypeStruct((HEADS_W, m), BF16),
        jax.ShapeDtypeStruct((nb, tm, HEADS_W), BF16),
        jax.ShapeDtypeStruct((nb, HEADS_W, tm), BF16),
        jax.ShapeDtypeStruct((nb, tm, LANES), BF16),
        jax.ShapeDtypeStruct((N_HEADS * AUG_PER_HEAD, m), BF16),
        jax.ShapeDtypeStruct((nb, 1, LANES), F32),
        jax.ShapeDtypeStruct((nb, N_HEADS, LANES), F32),
        jax.ShapeDtypeStruct((nb, 1, LANES), F32),
    )
    out_specs = (kv_spec, kv_spec, row(N_HEADS), row(HEADS_W), row(D_MODEL), colT(HEADS_W),
                 blk3(tm, HEADS_W), blk3(HEADS_W, tm), blk3(tm, LANES), colT(N_HEADS * AUG_PER_HEAD),
                 blk3(1, LANES), blk3(N_HEADS, LANES), blk3(1, LANES))
    in_specs = ([row(D_MODEL)] + [_layer_spec(a, layer) for a in (g, w, wfqt, bf)]
                + [_full_spec(a.shape) for a in (tri, pk, pqt, grp)])
    prev, prev_specs, aliases = _alias_args(stack, len(in_specs), (0, 1))
    return pl.pallas_call(
        functools.partial(_skip_refs, _proj_fox_kernel, len(in_specs), len(prev)),
        grid=(nb,), in_specs=in_specs + prev_specs, out_specs=out_specs, out_shape=out_shape,
        scratch_shapes=[pltpu.VMEM((1, LANES), F32)],
        input_output_aliases=aliases, compiler_params=_params(1), name="proj_fox",
    )(x, g, w, wfqt, bf, tri, pk, pqt, grp, *prev)


def _attn_kernel(tab_ref, qp_ref, qe_ref, ka_ref, ke_ref, vT_ref, o_ref, acc_ref, s_ref, *, fox):
    g = pl.program_id(0)
    i = pl.program_id(1)
    tq = qp_ref.shape[1]
    tk = ka_ref.shape[1]
    rows_p = lax.broadcasted_iota(jnp.int32, (LANES, tq), 0)
    rows_e = lax.broadcasted_iota(jnp.int32, qe_ref.shape, 0)
    kpos = lax.broadcasted_iota(jnp.int32, (tk, tq), 0)
    qpos = lax.broadcasted_iota(jnp.int32, (tk, tq), 1)
    if fox:
        visible = kpos <= qpos
    else:
        visible = (kpos // CHUNK) <= (qpos // CHUNK)
    qe = qe_ref[...]

    ws = []
    for hq in range(HEADS_PER_STEP):
        pair, hh = divmod(hq, 2)
        qp = qp_ref[pair * LANES:(pair + 1) * LANES, :]
        keep_p = (rows_p >= hh * HEAD_DV) & (rows_p < (hh + 1) * HEAD_DV)
        if fox:
            h = HEADS_PER_STEP * g + hq
            keep_e = (rows_e >= h * AUG_PER_HEAD) & (rows_e < (h + 1) * AUG_PER_HEAD)
            extra = jnp.where(keep_e, qe, jnp.zeros_like(qe))
        else:
            extra = qe[hq * ROPE_DIM:(hq + 1) * ROPE_DIM]
        zero_rows = jnp.zeros((LANES - extra.shape[0], tq), BF16)
        ws.append(jnp.concatenate([jnp.where(keep_p, qp, jnp.zeros_like(qp)), extra, zero_rows], axis=0))

    steps = [(u, hq) for u in range(tk // KEY_SUB) for hq in range(HEADS_PER_STEP)]
    ones_rows = jnp.ones((DEN_ROWS, KEY_SUB), BF16)

    def scores(j, step):
        u, hq = step
        pair = hq // 2
        rows = slice(u * KEY_SUB, (u + 1) * KEY_SUB)
        lhs = jnp.concatenate([ka_ref[j, rows, pair * LANES:(pair + 1) * LANES], ke_ref[j, rows, :]], axis=1)
        return _dot(lhs, ws[hq])

    def block(j, stats, masked, j_next):
        stats = list(stats)
        n_steps = len(steps)
        tiles = {}
        for n, (u, hq) in enumerate(steps):
            ahead = n + LOOKAHEAD
            if ahead < n_steps:
                tiles[ahead] = scores(j, steps[ahead])
            elif j_next is not None:
                tiles[ahead] = scores(j_next, steps[ahead - n_steps])
            sT = tiles.pop(n) if n in tiles else s_ref[n]
            rows = slice(u * KEY_SUB, (u + 1) * KEY_SUB)
            m_old = stats[hq]
            if masked:
                sT = jnp.where(visible[rows], sT, NEG)
            if fox:
                h = HEADS_PER_STEP * g + hq
                d = tab_ref[0, h, i] - tab_ref[0, h, j]
            else:
                d = 0.0
            m_new = jnp.maximum(m_old, jnp.max(sT, axis=0, keepdims=True) + d)
            alpha = jnp.exp2(m_old - m_new)
            pT = jnp.exp2((sT - (m_new - d)).astype(BF16))
            stats[hq] = m_new
            v = jnp.concatenate([vT_ref[j, hq * HEAD_DV:(hq + 1) * HEAD_DV, rows], ones_rows], axis=0)
            acc_ref[hq] = alpha * acc_ref[hq] + _dot(v, pT)
        for n, tile in tiles.items():
            s_ref[n - n_steps] = tile
        return tuple(stats)

    def more(j_top, stats):
        alive = j_top >= 0
        if fox:
            jt = jnp.maximum(j_top, 0)
            slack = None
            for hq in range(HEADS_PER_STEP):
                h = HEADS_PER_STEP * g + hq
                bound = tab_ref[1, h, i] * tab_ref[2, h, jt] + (tab_ref[0, h, i] - tab_ref[0, h, jt + 1])
                room = stats[hq] - bound
                slack = room if slack is None else jnp.minimum(slack, room)
            alive = alive & (jnp.min(slack) < SKIP_LOG2)
        return alive.astype(jnp.int32)

    def block_unshifted(j, masked, j_next):
        n_steps = len(steps)
        tiles = {}
        for n, (u, hq) in enumerate(steps):
            ahead = n + LOOKAHEAD
            if ahead < n_steps:
                tiles[ahead] = scores(j, steps[ahead])
            else:
                tiles[ahead] = scores(j_next, steps[ahead - n_steps])
            sT = tiles.pop(n) if n in tiles else s_ref[n]
            rows = slice(u * KEY_SUB, (u + 1) * KEY_SUB)
            if masked:
                sT = jnp.where(visible[rows], sT, NEG)
            pT = jnp.exp2(sT).astype(BF16)
            v = jnp.concatenate([vT_ref[j, hq * HEAD_DV:(hq + 1) * HEAD_DV, rows], ones_rows], axis=0)
            acc_ref[hq] = acc_ref[hq] + _dot(v, pT)
        for n, tile in tiles.items():
            s_ref[n - n_steps] = tile

    def walk_online():
        m0 = jnp.full((1, tq), NEG, F32)
        stats = block(i, (m0,) * HEADS_PER_STEP, True, jnp.maximum(i - 1, 0))

        def visit(c):
            j = c[0]
            st = block(j, c[2:], False, jnp.maximum(j - 1, 0))
            return (j - 1, more(j - 1, st)) + tuple(st)

        lax.while_loop(lambda c: c[1] > 0, visit, (i - 1, more(i - 1, stats)) + tuple(stats))

    def walk_unshifted():
        block_unshifted(i, True, jnp.maximum(i - 1, 0))

        def visit(k, c):
            j = i - 1 - k
            block_unshifted(j, False, jnp.maximum(j - 1, 0))
            return c

        lax.fori_loop(0, i, visit, 0)

    acc_ref[...] = jnp.zeros_like(acc_ref)
    for n in range(LOOKAHEAD):
        s_ref[n] = scores(i, steps[n])
    if fox:
        walk_online()
    else:
        bounded = None
        for hq in range(HEADS_PER_STEP):
            h = HEADS_PER_STEP * g + hq
            ok = tab_ref[1, h, i] * tab_ref[2, h, i] <= SAFE_LOG2
            bounded = ok if bounded is None else bounded & ok
        pl.when(bounded)(walk_unshifted)
        pl.when(jnp.logical_not(bounded))(walk_online)
    outs = [acc_ref[hq, 0:HEAD_DV, :] / acc_ref[hq, HEAD_DV:HEAD_DV + 1, :] for hq in range(HEADS_PER_STEP)]
    o_ref[...] = jnp.concatenate(outs, axis=0).T.astype(BF16)


def _attention(tab, qpT, qeT, ka, ke, vT, fox):
    nb, tk, _ = ka.shape
    s = qpT.shape[1]
    tq = tk
    width = HEADS_PER_STEP * HEAD_DV
    e_rows = qeT.shape[0] if fox else HEADS_PER_STEP * ROPE_DIM
    e_map = (lambda g, i, b: (0, i)) if fox else (lambda g, i, b: (g, i))
    once = dict(pipeline_mode=pl.Buffered(1))
    grid_spec = pltpu.PrefetchScalarGridSpec(
        num_scalar_prefetch=1,
        grid=(N_HEADS // HEADS_PER_STEP, nb),
        in_specs=[
            pl.BlockSpec((width, tq), lambda g, i, b: (g, i)),
            pl.BlockSpec((e_rows, tq), e_map),
            pl.BlockSpec((nb, tk, width), lambda g, i, b: (0, 0, g), **once),
            pl.BlockSpec((nb, tk, LANES), lambda g, i, b: (0, 0, 0), **once),
            pl.BlockSpec((nb, width, tk), lambda g, i, b: (0, g, 0), **once),
        ],
        out_specs=pl.BlockSpec((tq, width), lambda g, i, b: (i, g)),
        scratch_shapes=[pltpu.VMEM((HEADS_PER_STEP, HEAD_DV + DEN_ROWS, tq), F32),
                        pltpu.VMEM((LOOKAHEAD, KEY_SUB, tq), F32)],
    )
    return pl.pallas_call(
        functools.partial(_attn_kernel, fox=fox), grid_spec=grid_spec,
        out_shape=jax.ShapeDtypeStruct((s, HEADS_W), BF16),
        compiler_params=_params(2), name="attn_fox" if fox else "attn_mla",
    )(tab, qpT, qeT, ka, ke, vT)


def _softmax_pv(s, v):
    m = jnp.max(s, axis=-1, keepdims=True)
    p = jnp.exp2(s - m)
    l = jnp.sum(p, axis=-1, keepdims=True)
    return _dot(p.astype(BF16), v) / l


def _diag_heads(o_big, t_new):
    rows = lax.broadcasted_iota(jnp.int32, o_big.shape, 0) // t_new
    cols = lax.broadcasted_iota(jnp.int32, o_big.shape, 1) // HEAD_DV
    kept = jnp.where(rows == cols, o_big, 0.0)
    return jnp.sum(kept.reshape(N_HEADS, t_new, o_big.shape[1]), axis=0)


def _sample_attn_kernel(qn_ref, qr_ref, qf_ref, lrep_ref, cckv_ref, nckv_ref, ckr_ref, nkr_ref,
                        cfk_ref, nfk_ref, cfv_ref, nfv_ref, wukn_ref, wuv_ref, oa_ref, ob_ref):
    past = cckv_ref.shape[2]
    t_new = nckv_ref.shape[0]
    rows = N_HEADS * t_new
    pad = lrep_ref.shape[2] - past - t_new
    keys = past + t_new + pad

    def with_new(cache, new):
        return jnp.concatenate([cache, new, jnp.zeros((pad, new.shape[1]), new.dtype)], axis=0)

    kidx = lax.broadcasted_iota(jnp.int32, (rows, keys), 1)
    qpos = past + lax.broadcasted_iota(jnp.int32, (rows, keys), 0) % t_new
    real = kidx < past + t_new

    ckv_all = with_new(cckv_ref[0, 0], nckv_ref[...]).astype(BF16)
    kn = _dot(ckv_all, wukn_ref[0]).astype(BF16)
    va = _dot(ckv_all, wuv_ref[0]).astype(BF16)
    kr = with_new(ckr_ref[0, 0], nkr_ref[:, 0:ROPE_DIM]).astype(BF16)
    s = _dot_nt(qn_ref[0], kn) + _dot_nt(qr_ref[0], kr)
    s = jnp.where(real & ((kidx // CHUNK) <= (qpos // CHUNK)), s, NEG)
    oa_ref[...] = _diag_heads(_softmax_pv(s, va), t_new).astype(BF16)

    fk = with_new(cfk_ref[0, 0], nfk_ref[...]).astype(BF16)
    fv = with_new(cfv_ref[0, 0], nfv_ref[...]).astype(BF16)
    c = lrep_ref[0]
    shift = 1
    while shift < keys:
        c = c + jnp.where(kidx >= shift, pltpu.roll(c, shift, 1), 0.0)
        shift *= 2
    cq = jnp.sum(jnp.where(kidx == qpos, c, 0.0), axis=-1, keepdims=True)
    s = _dot_nt(qf_ref[0], fk) + (cq - c) * LOG2E
    s = jnp.where(real & (kidx <= qpos), s, NEG)
    ob_ref[...] = _diag_heads(_softmax_pv(s, fv), t_new).astype(BF16)


def _sample_attention(layer, qn_bd, qr, qf_bd, lrep, cckv, nckv, ckr, nkr, cfk, nfk, cfv, nfv, wukn, wuv):
    nbatch = cckv.shape[1]
    t_new = nckv.shape[0] // nbatch
    b3 = lambda a: pl.BlockSpec((1,) + a.shape[1:], lambda b: (b, 0, 0))
    cache = lambda a: pl.BlockSpec((1, 1) + a.shape[2:], lambda b: (layer, b, 0, 0))
    new = lambda a: pl.BlockSpec((t_new, a.shape[1]), lambda b: (b, 0))
    in_specs = [b3(qn_bd), b3(qr), b3(qf_bd), b3(lrep), cache(cckv), new(nckv), cache(ckr), new(nkr),
                cache(cfk), new(nfk), cache(cfv), new(nfv), _layer_spec(wukn, layer), _layer_spec(wuv, layer)]
    out = jax.ShapeDtypeStruct((nbatch * t_new, HEADS_W), BF16)
    o_spec = pl.BlockSpec((t_new, HEADS_W), lambda b: (b, 0))
    return pl.pallas_call(
        _sample_attn_kernel, grid=(nbatch,), in_specs=in_specs, out_specs=(o_spec, o_spec),
        out_shape=(out, out), compiler_params=_params(1), name="attn_sample",
    )(qn_bd, qr, qf_bd, lrep, cckv, nckv, ckr, nkr, cfk, nfk, cfv, nfv, wukn, wuv)


def _merge_kernel(x_ref, oa_ref, za_ref, ob_ref, zb_ref, ga_ref, gb_ref, woa_ref, wob_ref, wout_ref,
                  fg_ref, xn_ref, *maybe_y_ref):
    a = _dot(oa_ref[...] * za_ref[...], woa_ref[0])
    b = _dot(ob_ref[...] * zb_ref[...], wob_ref[0])
    mix = ga_ref[...].astype(F32) * a + gb_ref[...].astype(F32) * b
    xn = x_ref[...] + _dot(mix.astype(BF16), wout_ref[0])
    xn_ref[...] = xn
    if maybe_y_ref:
        maybe_y_ref[0][...] = _rms(xn, fg_ref[...])


def _merge(layer, x, oa, za, ob, zb, ga, gb, woa, wob, wout, fg, tm, final):
    m = x.shape[0]
    row = lambda c: pl.BlockSpec((tm, c), lambda i: (i, 0))
    in_specs = [row(D_MODEL), row(HEADS_W), row(HEADS_W), row(HEADS_W), row(HEADS_W), row(D_MODEL),
                row(D_MODEL), _layer_spec(woa, layer), _layer_spec(wob, layer), _layer_spec(wout, layer),
                _full_spec(fg.shape)]
    n_out = 2 if final else 1
    out_shape = tuple(jax.ShapeDtypeStruct((m, D_MODEL), F32) for _ in range(n_out))
    out_specs = tuple(row(D_MODEL) for _ in range(n_out))
    return pl.pallas_call(
        _merge_kernel, grid=(m // tm,), in_specs=in_specs, out_specs=out_specs, out_shape=out_shape,
        compiler_params=_params(1), name="merge",
    )(x, oa, za, ob, zb, ga, gb, woa, wob, wout, fg)


def _pad_cols(w, width):
    return jnp.pad(w, ((0, 0), (0, width - w.shape[1])))


def _pad_last(w, width):
    return jnp.pad(w, [(0, 0)] * (w.ndim - 1) + [(0, width - w.shape[-1])])


def _stacked_weights(w_in, w_uq, w_ukv, w_oa, w_ob, w_out, b_f, norm_g, g_q, g_kv):
    depth = w_in.shape[0]
    offs = np.cumsum(IN_SIZES)[:-1].tolist()
    w_cq, w_ckv, w_kr, w_za, w_fq, w_fk, w_fv, w_zf, w_zb, w_ga, w_gb = jnp.split(w_in, offs, axis=2)
    w_kr_swapped = jnp.concatenate([w_kr[..., HALF_ROPE:], w_kr[..., :HALF_ROPE]], axis=2)
    w_mla = jnp.concatenate([w_cq, w_ckv, _pad_last(w_kr, LANES), _pad_last(w_kr_swapped, LANES),
                             w_za, w_ga], axis=2).astype(BF16)
    w_fox = jnp.concatenate([w_fk, w_fv, _pad_last(w_zf, LANES), w_zb, w_gb], axis=2).astype(BF16)
    uq = w_uq.reshape(depth, Q_LORA, N_HEADS, NOPE_DIM + ROPE_DIM)
    uq_rows = jnp.concatenate([
        uq[..., :NOPE_DIM].reshape(depth, Q_LORA, -1),
        uq[..., NOPE_DIM:NOPE_DIM + HALF_ROPE].reshape(depth, Q_LORA, -1),
        uq[..., NOPE_DIM + HALF_ROPE:].reshape(depth, Q_LORA, -1)], axis=2)
    ukv = w_ukv.reshape(depth, KV_LORA, N_HEADS, NOPE_DIM + HEAD_DV)
    w_ukn = ukv[..., :NOPE_DIM].reshape(depth, KV_LORA, -1).astype(BF16)
    w_uv = ukv[..., NOPE_DIM:].reshape(depth, KV_LORA, -1).astype(BF16)
    tr = lambda a: jnp.swapaxes(a, 1, 2)
    return dict(
        w_mla=w_mla, w_fox=w_fox, w_uqT=tr(uq_rows).astype(BF16), w_ukn=w_ukn, w_uv=w_uv, w_uvT=tr(w_uv),
        w_fqT=tr(w_fq).astype(BF16),
        b_f=_pad_last(b_f[:, None, :], LANES),
        w_oa=w_oa.astype(BF16), w_ob=w_ob.astype(BF16), w_out=w_out.astype(BF16),
        norm_g=norm_g[:, None, :], g_q=g_q[:, None, :], g_kv=g_kv[:, None, :])


def _rope_tables(pos):
    inv = jnp.exp(-math.log(ROPE_THETA) * jnp.arange(HALF_ROPE, dtype=F32) / HALF_ROPE)
    ang = pos.astype(F32)[:, None] * inv[None, :]
    cos = jnp.cos(ang)
    sin = jnp.sin(ang)
    cosT = jnp.tile(cos.T, (N_HEADS, 1))
    sinT = jnp.tile(sin.T, (N_HEADS, 1))
    ctok = _pad_cols(jnp.concatenate([cos, cos], axis=1), LANES)
    stok = _pad_cols(jnp.concatenate([-sin, sin], axis=1), LANES)
    return cosT, sinT, ctok, stok


def _bias_placement():
    pk = np.zeros((3 * LANES, LANES), np.float32)
    pqt = np.zeros((N_HEADS * AUG_PER_HEAD, 3 * LANES), np.float32)
    for c in range(3):
        for h in range(N_HEADS):
            pk[c * LANES + h, h * AUG_PER_HEAD + 3 + c] = -1.0
            pqt[h * AUG_PER_HEAD + c, c * LANES + h] = 1.0
    return jnp.asarray(pk, BF16), jnp.asarray(pqt, BF16)


def _head_groups():
    grp = np.zeros((HEADS_W, LANES), np.float32)
    for h in range(N_HEADS):
        grp[h * HEAD_DV:(h + 1) * HEAD_DV, h] = 1.0
    return jnp.asarray(grp, BF16)


def _skip_tables(base, qn2, kn2):
    b = base[:, 0, :N_HEADS]
    qmax = jnp.sqrt(qn2[:, :, 0]) * NORM_SLACK
    kmax = lax.cummax(jnp.sqrt(kn2[:, 0, :N_HEADS]) * NORM_SLACK, axis=0)
    return jnp.stack([b, qmax, kmax]).transpose(0, 2, 1)


def _tri(n):
    return jnp.asarray(np.tril(np.ones((n, n), np.float32)), BF16)


def _block_diag_queries(qT, nbatch, t_new):
    q = qT.T.reshape(nbatch, t_new, N_HEADS, HEAD_DV)
    eye = jnp.eye(N_HEADS, dtype=q.dtype)
    return jnp.einsum('bthj,hg->bhtgj', q, eye).reshape(nbatch, N_HEADS * t_new, HEADS_W)


def _project(layer, x, sw, tables, tri, pk, pqt, grp, tm, stack_mla=None, stack_fox=None):
    cosT, sinT, ctok, stok = tables
    mla = _proj_mla(layer, x, sw['norm_g'], sw['w_mla'], sw['g_q'], sw['g_kv'], sw['w_uqT'], sw['w_ukn'],
                    sw['w_uvT'], cosT, sinT, ctok, stok, grp, tm, stack_mla)
    fox = _proj_fox(layer, x, sw['norm_g'], sw['w_fox'], sw['w_fqT'], sw['b_f'], tri, pk, pqt, grp,
                    tm, stack_fox)
    return mla, fox


def kernel(x_prompt, x_sample, cache_mla_ckv, cache_mla_krope, cache_fox_k, cache_fox_v, cache_fox_logf,
           norm_g, w_in, g_q, w_uq, g_kv, w_ukv, b_f, w_oa, w_ob, w_out, final_g):
    depth = w_in.shape[0]
    _, seq, _ = x_prompt.shape
    nbatch, t_new, _ = x_sample.shape
    past = cache_mla_ckv.shape[2]
    m_s = nbatch * t_new
    key_pad = -(past + t_new) % LANES

    tabs_p = _rope_tables(jnp.arange(seq, dtype=jnp.int32))
    tabs_s = _rope_tables(past + jnp.arange(m_s, dtype=jnp.int32) % t_new)
    pk, pqt = _bias_placement()
    grp = _head_groups()
    tri_p = _tri(SEQ_BLOCK)
    tri_s = _tri(m_s)
    fg = final_g[None, :]

    xp = x_prompt.reshape(seq, D_MODEL)
    xs = x_sample.reshape(m_s, D_MODEL)
    outs = {k: [] for k in ('p_kr', 'p_lf', 's_ckv', 's_kr', 's_fk', 's_fv', 's_lf')}
    p_ckv_stack = p_kv_stack = None
    yp = ys = None
    sw = _stacked_weights(w_in, w_uq, w_ukv, w_oa, w_ob, w_out, b_f, norm_g, g_q, g_kv)
    for l in range(depth):
        final = l == depth - 1

        mla, fox = _project(l, xp, sw, tabs_p, tri_p, pk, pqt, grp, SEQ_BLOCK,
                            (l, depth, p_ckv_stack), (l, depth, p_kv_stack))
        ckv, kr, za, ga, qnT, qrT, knb, krb, vT, qn2_a, kn2_a = mla
        fk, fv, logf, zb, gb, fqT, fkb, fvT, augk, augqT, base, qn2, kn2 = fox
        tab = _skip_tables(base, qn2, kn2)
        tab_a = _skip_tables(jnp.zeros_like(base), qn2_a, kn2_a)
        o_a = _attention(tab_a, qnT, qrT, knb, krb, vT, fox=False)
        o_b = _attention(tab, fqT, augqT, fkb, augk, fvT, fox=True)
        res = _merge(l, xp, o_a, za, o_b, zb, ga, gb, sw['w_oa'], sw['w_ob'], sw['w_out'], fg, SEQ_BLOCK, final)
        xp = res[0]
        if final:
            yp = res[1]
        p_ckv_stack, p_kv_stack = (ckv,), (fk, fv)
        outs['p_kr'].append(kr[:, :ROPE_DIM]); outs['p_lf'].append(logf)

        mla, fox = _project(l, xs, sw, tabs_s, tri_s, pk, pqt, grp, m_s)
        ckv, kr, za, ga, qnT, qrT = mla[:6]
        fk, fv, logf, zb, gb, fqT = fox[:6]
        qn_bd = _block_diag_queries(qnT, nbatch, t_new)
        qf_bd = _block_diag_queries(fqT, nbatch, t_new)
        qr = qrT.reshape(N_HEADS, ROPE_DIM, nbatch, t_new).transpose(2, 0, 3, 1).reshape(
            nbatch, N_HEADS * t_new, ROPE_DIM)
        lf_all = jnp.concatenate([cache_fox_logf[l], logf.reshape(nbatch, t_new, N_HEADS),
                                  jnp.zeros((nbatch, key_pad, N_HEADS), F32)], axis=1)
        lrep = jnp.repeat(lf_all.transpose(0, 2, 1), t_new, axis=1)
        o_a, o_b = _sample_attention(
            l, qn_bd, qr, qf_bd, lrep, cache_mla_ckv, ckv, cache_mla_krope, kr,
            cache_fox_k.reshape(depth, nbatch, past, HEADS_W), fk,
            cache_fox_v.reshape(depth, nbatch, past, HEADS_W), fv, sw['w_ukn'], sw['w_uv'])
        res = _merge(l, xs, o_a, za, o_b, zb, ga, gb, sw['w_oa'], sw['w_ob'], sw['w_out'], fg, m_s, final)
        xs = res[0]
        if final:
            ys = res[1]
        outs['s_ckv'].append(ckv); outs['s_kr'].append(kr[:, :ROPE_DIM]); outs['s_fk'].append(fk)
        outs['s_fv'].append(fv); outs['s_lf'].append(logf)

    st = lambda name, shape: jnp.stack(outs[name]).reshape((depth,) + shape)
    return (yp.reshape(1, seq, D_MODEL), ys.reshape(nbatch, t_new, D_MODEL),
            p_ckv_stack[0].reshape(depth, 1, seq, KV_LORA), st('p_kr', (1, seq, ROPE_DIM)),
            p_kv_stack[0].reshape(depth, 1, seq, N_HEADS, HEAD_DV),
            p_kv_stack[1].reshape(depth, 1, seq, N_HEADS, HEAD_DV),
            st('p_lf', (1, seq, N_HEADS)),
            st('s_ckv', (nbatch, t_new, KV_LORA)), st('s_kr', (nbatch, t_new, ROPE_DIM)),
            st('s_fk', (nbatch, t_new, N_HEADS, HEAD_DV)), st('s_fv', (nbatch, t_new, N_HEADS, HEAD_DV)),
            st('s_lf', (nbatch, t_new, N_HEADS)))
```

```python
import functools
import math

import numpy as np
import jax
import jax.numpy as jnp
from jax import lax
from jax.experimental import pallas as pl
from jax.experimental.pallas import tpu as pltpu

D_MODEL = 1024
N_HEADS = 8
NOPE_DIM = 64
ROPE_DIM = 32
HALF_ROPE = ROPE_DIM // 2
HEAD_DV = 64
Q_LORA = 384
KV_LORA = 256
HEADS_W = N_HEADS * HEAD_DV
CHUNK = 64
ROPE_THETA = 10000.0
LOG2E = math.log2(math.e)
MLA_QSCALE = LOG2E / math.sqrt(NOPE_DIM + ROPE_DIM)
FOX_QSCALE = LOG2E / math.sqrt(HEAD_DV)
EPS = 1e-6
NEG = -1e30
IN_SIZES = (Q_LORA, KV_LORA, ROPE_DIM, HEADS_W, HEADS_W, HEADS_W, HEADS_W, N_HEADS, HEADS_W, D_MODEL, D_MODEL)

LANES = 128
SEQ_BLOCK = 512
KEY_SUB = 256
DEN_ROWS = 16
HEADS_PER_STEP = 8
LOOKAHEAD = 2
SAFE_LOG2 = 100.0
SKIP_LOG2 = 64.0
NORM_SLACK = 1.01
AUG_PER_HEAD = 8
VMEM_LIMIT_BYTES = 56 * 1024 * 1024

F32 = jnp.float32
BF16 = jnp.bfloat16


def _dot(a, b):
    return jnp.dot(a, b, preferred_element_type=F32)


def _dot_nt(a, b):
    return lax.dot_general(a, b, (((1,), (1,)), ((), ())), preferred_element_type=F32)


def _rms(x, g):
    return x * lax.rsqrt(jnp.mean(x * x, axis=-1, keepdims=True) + EPS) * g


def _sigmoid(x):
    return 1.0 / (1.0 + jnp.exp(-x))


def _silu(x):
    return x * _sigmoid(x)


def _split3(x):
    hi = x.astype(BF16)
    r1 = x - hi.astype(F32)
    mid = r1.astype(BF16)
    lo = (r1 - mid.astype(F32)).astype(BF16)
    return hi, mid, lo


def _full_spec(shape):
    nd = len(shape)
    return pl.BlockSpec(shape, lambda *_: (0,) * nd)


def _layer_spec(arr, layer):
    nd = arr.ndim
    return pl.BlockSpec((1,) + arr.shape[1:], lambda *_: (layer,) + (0,) * (nd - 1))


def _put(ref, val):
    ref[...] = val.reshape(ref.shape)


def _skip_refs(body, n_in, n_skip, *refs):
    return body(*refs[:n_in], *refs[n_in + n_skip:])


def _stacked(stack, m, tm, cols, heads=None):
    if stack is None:
        return jax.ShapeDtypeStruct((m, cols), F32), pl.BlockSpec((tm, cols), lambda i: (i, 0))
    layer, depth, _ = stack
    if heads is None:
        return (jax.ShapeDtypeStruct((depth, m, cols), F32),
                pl.BlockSpec((1, tm, cols), lambda i: (layer, i, 0)))
    return (jax.ShapeDtypeStruct((depth, m, heads, cols // heads), F32),
            pl.BlockSpec((1, tm, heads, cols // heads), lambda i: (layer, i, 0, 0)))


def _alias_args(stack, n_in, out_indices):
    if stack is None or stack[2] is None:
        return [], [], {}
    prev = list(stack[2])
    specs = [pl.BlockSpec(memory_space=pl.ANY) for _ in prev]
    return prev, specs, {n_in + k: out for k, out in enumerate(out_indices)}


def _params(n_axes):
    return pltpu.CompilerParams(dimension_semantics=("arbitrary",) * n_axes,
                                vmem_limit_bytes=VMEM_LIMIT_BYTES)


MLA_IN_COLS = Q_LORA + KV_LORA + 2 * LANES + HEADS_W + D_MODEL


def _proj_mla_kernel(x_ref, g_ref, w_ref, gq_ref, gkv_ref, wuq_ref, wukn_ref, wuvt_ref,
                     cosT_ref, sinT_ref, ctok_ref, stok_ref, grp_ref,
                     ckv_ref, kr_ref, za_ref, ga_ref, qnT_ref, qrT_ref, knb_ref, krb_ref, vT_ref,
                     qn2_ref, kn2_ref):
    tm = x_ref.shape[0]
    hb = _rms(x_ref[...], g_ref[0]).astype(BF16)
    o = 0
    zcq = _dot(hb, w_ref[0, :, o:o + Q_LORA]); o += Q_LORA
    zckv = _dot(hb, w_ref[0, :, o:o + KV_LORA]); o += KV_LORA
    zka = _dot(hb, w_ref[0, :, o:o + LANES]); o += LANES
    zkb = _dot(hb, w_ref[0, :, o:o + LANES]); o += LANES
    za_ref[...] = _silu(_dot(hb, w_ref[0, :, o:o + HEADS_W])).astype(BF16); o += HEADS_W
    ga_ref[...] = _sigmoid(_dot(hb, w_ref[0, :, o:o + D_MODEL])).astype(BF16)

    cqb = _rms(zcq, gq_ref[0]).astype(BF16)
    qT = _dot_nt(wuq_ref[0], cqb)
    qn = qT[0:HEADS_W] * MLA_QSCALE
    qnT_ref[...] = qn.astype(BF16)
    x1 = qT[HEADS_W:HEADS_W + LANES]
    x2 = qT[HEADS_W + LANES:HEADS_W + 2 * LANES]
    c = cosT_ref[...]
    s = sinT_ref[...]
    r1 = (x1 * c - x2 * s) * MLA_QSCALE
    r2 = (x1 * s + x2 * c) * MLA_QSCALE
    n1 = r1.astype(BF16)
    n2 = r2.astype(BF16)
    for h in range(N_HEADS):
        qrT_ref[h * ROPE_DIM:h * ROPE_DIM + HALF_ROPE, :] = n1[h * HALF_ROPE:(h + 1) * HALF_ROPE]
        qrT_ref[h * ROPE_DIM + HALF_ROPE:(h + 1) * ROPE_DIM, :] = n2[h * HALF_ROPE:(h + 1) * HALF_ROPE]
    qn2 = (jnp.sum((qn * qn).reshape(N_HEADS, NOPE_DIM, tm), axis=1)
           + jnp.sum((r1 * r1 + r2 * r2).reshape(N_HEADS, HALF_ROPE, tm), axis=1))
    qn2_ref[0] = jnp.broadcast_to(jnp.max(qn2, axis=1, keepdims=True), (N_HEADS, LANES))

    ckv = _rms(zckv, gkv_ref[0])
    _put(ckv_ref, ckv)
    cb = ckv.astype(BF16)
    kn = _dot(cb, wukn_ref[0])
    knb_ref[0] = kn.astype(BF16)
    vT_ref[0] = _dot_nt(wuvt_ref[0], cb).astype(BF16)

    kr = zka * ctok_ref[...] + zkb * stok_ref[...]
    kr_ref[...] = kr
    krb_ref[0] = kr.astype(BF16)
    kn2 = _dot((kn * kn).astype(BF16), grp_ref[...]) + jnp.sum(kr * kr, axis=1, keepdims=True)
    kn2_ref[0] = jnp.max(kn2, axis=0, keepdims=True)


def _proj_mla(layer, x, g, w, gq, gkv, wuq, wukn, wuvt, cosT, sinT, ctok, stok, grp, tm, stack=None):
    m = x.shape[0]
    nb = m // tm
    row = lambda c: pl.BlockSpec((tm, c), lambda i: (i, 0))
    colT = lambda r: pl.BlockSpec((r, tm), lambda i: (0, i))
    blk3 = lambda a, b: pl.BlockSpec((1, a, b), lambda i: (i, 0, 0))
    ckv_shape, ckv_spec = _stacked(stack, m, tm, KV_LORA)
    out_shape = (
        ckv_shape,
        jax.ShapeDtypeStruct((m, LANES), F32),
        jax.ShapeDtypeStruct((m, HEADS_W), BF16),
        jax.ShapeDtypeStruct((m, D_MODEL), BF16),
        jax.ShapeDtypeStruct((HEADS_W, m), BF16),
        jax.ShapeDtypeStruct((N_HEADS * ROPE_DIM, m), BF16),
        jax.ShapeDtypeStruct((nb, tm, HEADS_W), BF16),
        jax.ShapeDtypeStruct((nb, tm, LANES), BF16),
        jax.ShapeDtypeStruct((nb, HEADS_W, tm), BF16),
        jax.ShapeDtypeStruct((nb, N_HEADS, LANES), F32),
        jax.ShapeDtypeStruct((nb, 1, LANES), F32),
    )
    out_specs = (ckv_spec, row(LANES), row(HEADS_W), row(D_MODEL), colT(HEADS_W),
                 colT(N_HEADS * ROPE_DIM), blk3(tm, HEADS_W), blk3(tm, LANES), blk3(HEADS_W, tm),
                 blk3(N_HEADS, LANES), blk3(1, LANES))
    in_specs = [row(D_MODEL)] + [_layer_spec(a, layer) for a in (g, w, gq, gkv, wuq, wukn, wuvt)] + [
        colT(LANES), colT(LANES), row(LANES), row(LANES), _full_spec(grp.shape)]
    prev, prev_specs, aliases = _alias_args(stack, len(in_specs), (0,))
    return pl.pallas_call(
        functools.partial(_skip_refs, _proj_mla_kernel, len(in_specs), len(prev)),
        grid=(nb,), in_specs=in_specs + prev_specs, out_specs=out_specs, out_shape=out_shape,
        input_output_aliases=aliases, compiler_params=_params(1), name="proj_mla",
    )(x, g, w, gq, gkv, wuq, wukn, wuvt, cosT, sinT, ctok, stok, grp, *prev)


FOX_IN_COLS = 2 * HEADS_W + LANES + HEADS_W + D_MODEL


def _proj_fox_kernel(x_ref, g_ref, w_ref, wfqt_ref, bf_ref, tri_ref, pk_ref, pqt_ref, grp_ref,
                     fk_ref, fv_ref, logf_ref, zb_ref, gb_ref, fqT_ref, fkb_ref, fvT_ref,
                     augk_ref, augqT_ref, base_ref, qn2_ref, kn2_ref, carry_ref):
    i = pl.program_id(0)
    tm = x_ref.shape[0]
    hb = _rms(x_ref[...], g_ref[0]).astype(BF16)
    o = 0
    fk = _dot(hb, w_ref[0, :, o:o + HEADS_W]); o += HEADS_W
    _put(fk_ref, fk)
    fkb_ref[0] = fk.astype(BF16)
    kn2_ref[0] = jnp.max(_dot((fk * fk).astype(BF16), grp_ref[...]), axis=0, keepdims=True)
    fv = _dot(hb, w_ref[0, :, o:o + HEADS_W]); o += HEADS_W
    _put(fv_ref, fv)
    fvT_ref[0] = fv.T.astype(BF16)
    zf = _dot(hb, w_ref[0, :, o:o + LANES]) + bf_ref[0]; o += LANES
    zb_ref[...] = _silu(_dot(hb, w_ref[0, :, o:o + HEADS_W])).astype(BF16); o += HEADS_W
    gb_ref[...] = _sigmoid(_dot(hb, w_ref[0, :, o:o + D_MODEL])).astype(BF16)
    fqT = _dot_nt(wfqt_ref[0], hb) * FOX_QSCALE
    fqT_ref[...] = fqT.astype(BF16)
    qn2 = jnp.sum((fqT * fqT).reshape(N_HEADS, HEAD_DV, tm), axis=1)
    qn2_ref[0] = jnp.broadcast_to(jnp.max(qn2, axis=1, keepdims=True), (N_HEADS, LANES))

    lane = lax.broadcasted_iota(jnp.int32, (tm, LANES), 1)
    logf = jnp.minimum(zf, 0.0) - jnp.log(1.0 + jnp.exp(-jnp.abs(zf)))
    logf = jnp.where(lane < N_HEADS, logf, 0.0)
    logf_ref[...] = logf[:, 0:N_HEADS]

    hi, mid, lo = _split3(logf)
    packed = (hi.astype(F32) + pltpu.roll(mid.astype(F32), N_HEADS, 1)
              + pltpu.roll(lo.astype(F32), 2 * N_HEADS, 1)).astype(BF16)
    cum3 = _dot(tri_ref[...], packed)
    r = cum3 + pltpu.roll(cum3, LANES - N_HEADS, 1) + pltpu.roll(cum3, LANES - 2 * N_HEADS, 1)
    r = jnp.where(lane < N_HEADS, r, 0.0)

    @pl.when(i == 0)
    def _():
        carry_ref[...] = jnp.zeros_like(carry_ref)

    base_ref[0] = carry_ref[...] * LOG2E
    carry_ref[...] = carry_ref[...] + r[tm - 1:tm, :]

    rh, rm, rl = _split3(r * LOG2E)
    rcat = jnp.concatenate([rh, rm, rl], axis=1)
    slot = lane % AUG_PER_HEAD
    ones_k = jnp.where((lane < N_HEADS * AUG_PER_HEAD) & (slot < 3), 1.0, 0.0)
    augk_ref[0] = (_dot(rcat, pk_ref[...]) + ones_k).astype(BF16)
    rowq = lax.broadcasted_iota(jnp.int32, (N_HEADS * AUG_PER_HEAD, tm), 0) % AUG_PER_HEAD
    ones_q = jnp.where((rowq >= 3) & (rowq < 6), 1.0, 0.0)
    augqT_ref[...] = (_dot_nt(pqt_ref[...], rcat) + ones_q).astype(BF16)


def _proj_fox(layer, x, g, w, wfqt, bf, tri, pk, pqt, grp, tm, stack=None):
    m = x.shape[0]
    nb = m // tm
    row = lambda c: pl.BlockSpec((tm, c), lambda i: (i, 0))
    colT = lambda r: pl.BlockSpec((r, tm), lambda i: (0, i))
    blk3 = lambda a, b: pl.BlockSpec((1, a, b), lambda i: (i, 0, 0))
    kv_shape, kv_spec = _stacked(stack, m, tm, HEADS_W, N_HEADS)
    out_shape = (
        kv_shape,
        kv_shape,
        jax.ShapeDtypeStruct((m, N_HEADS), F32),
        jax.ShapeDtypeStruct((m, HEADS_W), BF16),
        jax.ShapeDtypeStruct((m, D_MODEL), BF16),
        jax.ShapeDtypeStruct((HEADS_W, m), BF16),
        jax.ShapeDtypeStruct((nb, tm, HEADS_W), BF16),
        jax.ShapeDtypeStruct((nb, HEADS_W, tm), BF16),
        jax.ShapeDtypeStruct((nb, tm, LANES), BF16),
        jax.ShapeDtypeStruct((N_HEADS * AUG_PER_HEAD, m), BF16),
        jax.ShapeDtypeStruct((nb, 1, LANES), F32),
        jax.ShapeDtypeStruct((nb, N_HEADS, LANES), F32),
        jax.ShapeDtypeStruct((nb, 1, LANES), F32),
    )
    out_specs = (kv_spec, kv_spec, row(N_HEADS), row(HEADS_W), row(D_MODEL), colT(HEADS_W),
                 blk3(tm, HEADS_W), blk3(HEADS_W, tm), blk3(tm, LANES), colT(N_HEADS * AUG_PER_HEAD),
                 blk3(1, LANES), blk3(N_HEADS, LANES), blk3(1, LANES))
    in_specs = ([row(D_MODEL)] + [_layer_spec(a, layer) for a in (g, w, wfqt, bf)]
                + [_full_spec(a.shape) for a in (tri, pk, pqt, grp)])
    prev, prev_specs, aliases = _alias_args(stack, len(in_specs), (0, 1))
    return pl.pallas_call(
        functools.partial(_skip_refs, _proj_fox_kernel, len(in_specs), len(prev)),
        grid=(nb,), in_specs=in_specs + prev_specs, out_specs=out_specs, out_shape=out_shape,
        scratch_shapes=[pltpu.VMEM((1, LANES), F32)],
        input_output_aliases=aliases, compiler_params=_params(1), name="proj_fox",
    )(x, g, w, wfqt, bf, tri, pk, pqt, grp, *prev)


def _attn_kernel(tab_ref, qp_ref, qe_ref, ka_ref, ke_ref, vT_ref, o_ref, acc_ref, s_ref, *, fox):
    g = pl.program_id(0)
    i = pl.program_id(1)
    tq = qp_ref.shape[1]
    tk = ka_ref.shape[1]
    rows_p = lax.broadcasted_iota(jnp.int32, (LANES, tq), 0)
    rows_e = lax.broadcasted_iota(jnp.int32, qe_ref.shape, 0)
    kpos = lax.broadcasted_iota(jnp.int32, (tk, tq), 0)
    qpos = lax.broadcasted_iota(jnp.int32, (tk, tq), 1)
    if fox:
        visible = kpos <= qpos
    else:
        visible = (kpos // CHUNK) <= (qpos // CHUNK)
    qe = qe_ref[...]

    ws = []
    for hq in range(HEADS_PER_STEP):
        pair, hh = divmod(hq, 2)
        qp = qp_ref[pair * LANES:(pair + 1) * LANES, :]
        keep_p = (rows_p >= hh * HEAD_DV) & (rows_p < (hh + 1) * HEAD_DV)
        if fox:
            h = HEADS_PER_STEP * g + hq
            keep_e = (rows_e >= h * AUG_PER_HEAD) & (rows_e < (h + 1) * AUG_PER_HEAD)
            extra = jnp.where(keep_e, qe, jnp.zeros_like(qe))
        else:
            extra = qe[hq * ROPE_DIM:(hq + 1) * ROPE_DIM]
        zero_rows = jnp.zeros((LANES - extra.shape[0], tq), BF16)
        ws.append(jnp.concatenate([jnp.where(keep_p, qp, jnp.zeros_like(qp)), extra, zero_rows], axis=0))

    steps = [(u, hq) for u in range(tk // KEY_SUB) for hq in range(HEADS_PER_STEP)]
    ones_rows = jnp.ones((DEN_ROWS, KEY_SUB), BF16)

    def scores(j, step):
        u, hq = step
        pair = hq // 2
        rows = slice(u * KEY_SUB, (u + 1) * KEY_SUB)
        lhs = jnp.concatenate([ka_ref[j, rows, pair * LANES:(pair + 1) * LANES], ke_ref[j, rows, :]], axis=1)
        return _dot(lhs, ws[hq])

    def block(j, stats, masked, j_next):
        stats = list(stats)
        n_steps = len(steps)
        tiles = {}
        for n, (u, hq) in enumerate(steps):
            ahead = n + LOOKAHEAD
            if ahead < n_steps:
                tiles[ahead] = scores(j, steps[ahead])
            elif j_next is not None:
                tiles[ahead] = scores(j_next, steps[ahead - n_steps])
            sT = tiles.pop(n) if n in tiles else s_ref[n]
            rows = slice(u * KEY_SUB, (u + 1) * KEY_SUB)
            m_old = stats[hq]
            if masked:
                sT = jnp.where(visible[rows], sT, NEG)
            if fox:
                h = HEADS_PER_STEP * g + hq
                d = tab_ref[0, h, i] - tab_ref[0, h, j]
            else:
                d = 0.0
            m_new = jnp.maximum(m_old, jnp.max(sT, axis=0, keepdims=True) + d)
            alpha = jnp.exp2(m_old - m_new)
            pT = jnp.exp2((sT - (m_new - d)).astype(BF16))
            stats[hq] = m_new
            v = jnp.concatenate([vT_ref[j, hq * HEAD_DV:(hq + 1) * HEAD_DV, rows], ones_rows], axis=0)
            acc_ref[hq] = alpha * acc_ref[hq] + _dot(v, pT)
        for n, tile in tiles.items():
            s_ref[n - n_steps] = tile
        return tuple(stats)

    def more(j_top, stats):
        alive = j_top >= 0
        if fox:
            jt = jnp.maximum(j_top, 0)
            slack = None
            for hq in range(HEADS_PER_STEP):
                h = HEADS_PER_STEP * g + hq
                bound = tab_ref[1, h, i] * tab_ref[2, h, jt] + (tab_ref[0, h, i] - tab_ref[0, h, jt + 1])
                room = stats[hq] - bound
                slack = room if slack is None else jnp.minimum(slack, room)
            alive = alive & (jnp.min(slack) < SKIP_LOG2)
        return alive.astype(jnp.int32)

    def block_unshifted(j, masked, j_next):
        n_steps = len(steps)
        tiles = {}
        for n, (u, hq) in enumerate(steps):
            ahead = n + LOOKAHEAD
            if ahead < n_steps:
                tiles[ahead] = scores(j, steps[ahead])
            else:
                tiles[ahead] = scores(j_next, steps[ahead - n_steps])
            sT = tiles.pop(n) if n in tiles else s_ref[n]
            rows = slice(u * KEY_SUB, (u + 1) * KEY_SUB)
            if masked:
                sT = jnp.where(visible[rows], sT, NEG)
            pT = jnp.exp2(sT).astype(BF16)
            v = jnp.concatenate([vT_ref[j, hq * HEAD_DV:(hq + 1) * HEAD_DV, rows], ones_rows], axis=0)
            acc_ref[hq] = acc_ref[hq] + _dot(v, pT)
        for n, tile in tiles.items():
            s_ref[n - n_steps] = tile

    def walk_online():
        m0 = jnp.full((1, tq), NEG, F32)
        stats = block(i, (m0,) * HEADS_PER_STEP, True, jnp.maximum(i - 1, 0))

        def visit(c):
            j = c[0]
            st = block(j, c[2:], False, jnp.maximum(j - 1, 0))
            return (j - 1, more(j - 1, st)) + tuple(st)

        lax.while_loop(lambda c: c[1] > 0, visit, (i - 1, more(i - 1, stats)) + tuple(stats))

    def walk_unshifted():
        block_unshifted(i, True, jnp.maximum(i - 1, 0))

        def visit(k, c):
            j = i - 1 - k
            block_unshifted(j, False, jnp.maximum(j - 1, 0))
            return c

        lax.fori_loop(0, i, visit, 0)

    acc_ref[...] = jnp.zeros_like(acc_ref)
    for n in range(LOOKAHEAD):
        s_ref[n] = scores(i, steps[n])
    if fox:
        walk_online()
    else:
        bounded = None
        for hq in range(HEADS_PER_STEP):
            h = HEADS_PER_STEP * g + hq
            ok = tab_ref[1, h, i] * tab_ref[2, h, i] <= SAFE_LOG2
            bounded = ok if bounded is None else bounded & ok
        pl.when(bounded)(walk_unshifted)
        pl.when(jnp.logical_not(bounded))(walk_online)
    outs = [acc_ref[hq, 0:HEAD_DV, :] / acc_ref[hq, HEAD_DV:HEAD_DV + 1, :] for hq in range(HEADS_PER_STEP)]
    o_ref[...] = jnp.concatenate(outs, axis=0).T.astype(BF16)


def _attention(tab, qpT, qeT, ka, ke, vT, fox):
    nb, tk, _ = ka.shape
    s = qpT.shape[1]
    tq = tk
    width = HEADS_PER_STEP * HEAD_DV
    e_rows = qeT.shape[0] if fox else HEADS_PER_STEP * ROPE_DIM
    e_map = (lambda g, i, b: (0, i)) if fox else (lambda g, i, b: (g, i))
    once = dict(pipeline_mode=pl.Buffered(1))
    grid_spec = pltpu.PrefetchScalarGridSpec(
        num_scalar_prefetch=1,
        grid=(N_HEADS // HEADS_PER_STEP, nb),
        in_specs=[
            pl.BlockSpec((width, tq), lambda g, i, b: (g, i)),
            pl.BlockSpec((e_rows, tq), e_map),
            pl.BlockSpec((nb, tk, width), lambda g, i, b: (0, 0, g), **once),
            pl.BlockSpec((nb, tk, LANES), lambda g, i, b: (0, 0, 0), **once),
            pl.BlockSpec((nb, width, tk), lambda g, i, b: (0, g, 0), **once),
        ],
        out_specs=pl.BlockSpec((tq, width), lambda g, i, b: (i, g)),
        scratch_shapes=[pltpu.VMEM((HEADS_PER_STEP, HEAD_DV + DEN_ROWS, tq), F32),
                        pltpu.VMEM((LOOKAHEAD, KEY_SUB, tq), F32)],
    )
    return pl.pallas_call(
        functools.partial(_attn_kernel, fox=fox), grid_spec=grid_spec,
        out_shape=jax.ShapeDtypeStruct((s, HEADS_W), BF16),
        compiler_params=_params(2), name="attn_fox" if fox else "attn_mla",
    )(tab, qpT, qeT, ka, ke, vT)


def _softmax_pv(s, v):
    m = jnp.max(s, axis=-1, keepdims=True)
    p = jnp.exp2(s - m)
    l = jnp.sum(p, axis=-1, keepdims=True)
    return _dot(p.astype(BF16), v) / l


def _diag_heads(o_big, t_new):
    rows = lax.broadcasted_iota(jnp.int32, o_big.shape, 0) // t_new
    cols = lax.broadcasted_iota(jnp.int32, o_big.shape, 1) // HEAD_DV
    kept = jnp.where(rows == cols, o_big, 0.0)
    return jnp.sum(kept.reshape(N_HEADS, t_new, o_big.shape[1]), axis=0)


def _sample_attn_kernel(qn_ref, qr_ref, qf_ref, lrep_ref, cckv_ref, nckv_ref, ckr_ref, nkr_ref,
                        cfk_ref, nfk_ref, cfv_ref, nfv_ref, wukn_ref, wuv_ref, oa_ref, ob_ref):
    past = cckv_ref.shape[2]
    t_new = nckv_ref.shape[0]
    rows = N_HEADS * t_new
    pad = lrep_ref.shape[2] - past - t_new
    keys = past + t_new + pad

    def with_new(cache, new):
        return jnp.concatenate([cache, new, jnp.zeros((pad, new.shape[1]), new.dtype)], axis=0)

    kidx = lax.broadcasted_iota(jnp.int32, (rows, keys), 1)
    qpos = past + lax.broadcasted_iota(jnp.int32, (rows, keys), 0) % t_new
    real = kidx < past + t_new

    ckv_all = with_new(cckv_ref[0, 0], nckv_ref[...]).astype(BF16)
    kn = _dot(ckv_all, wukn_ref[0]).astype(BF16)
    va = _dot(ckv_all, wuv_ref[0]).astype(BF16)
    kr = with_new(ckr_ref[0, 0], nkr_ref[:, 0:ROPE_DIM]).astype(BF16)
    s = _dot_nt(qn_ref[0], kn) + _dot_nt(qr_ref[0], kr)
    s = jnp.where(real & ((kidx // CHUNK) <= (qpos // CHUNK)), s, NEG)
    oa_ref[...] = _diag_heads(_softmax_pv(s, va), t_new).astype(BF16)

    fk = with_new(cfk_ref[0, 0], nfk_ref[...]).astype(BF16)
    fv = with_new(cfv_ref[0, 0], nfv_ref[...]).astype(BF16)
    c = lrep_ref[0]
    shift = 1
    while shift < keys:
        c = c + jnp.where(kidx >= shift, pltpu.roll(c, shift, 1), 0.0)
        shift *= 2
    cq = jnp.sum(jnp.where(kidx == qpos, c, 0.0), axis=-1, keepdims=True)
    s = _dot_nt(qf_ref[0], fk) + (cq - c) * LOG2E
    s = jnp.where(real & (kidx <= qpos), s, NEG)
    ob_ref[...] = _diag_heads(_softmax_pv(s, fv), t_new).astype(BF16)


def _sample_attention(layer, qn_bd, qr, qf_bd, lrep, cckv, nckv, ckr, nkr, cfk, nfk, cfv, nfv, wukn, wuv):
    nbatch = cckv.shape[1]
    t_new = nckv.shape[0] // nbatch
    b3 = lambda a: pl.BlockSpec((1,) + a.shape[1:], lambda b: (b, 0, 0))
    cache = lambda a: pl.BlockSpec((1, 1) + a.shape[2:], lambda b: (layer, b, 0, 0))
    new = lambda a: pl.BlockSpec((t_new, a.shape[1]), lambda b: (b, 0))
    in_specs = [b3(qn_bd), b3(qr), b3(qf_bd), b3(lrep), cache(cckv), new(nckv), cache(ckr), new(nkr),
                cache(cfk), new(nfk), cache(cfv), new(nfv), _layer_spec(wukn, layer), _layer_spec(wuv, layer)]
    out = jax.ShapeDtypeStruct((nbatch * t_new, HEADS_W), BF16)
    o_spec = pl.BlockSpec((t_new, HEADS_W), lambda b: (b, 0))
    return pl.pallas_call(
        _sample_attn_kernel, grid=(nbatch,), in_specs=in_specs, out_specs=(o_spec, o_spec),
        out_shape=(out, out), compiler_params=_params(1), name="attn_sample",
    )(qn_bd, qr, qf_bd, lrep, cckv, nckv, ckr, nkr, cfk, nfk, cfv, nfv, wukn, wuv)


def _merge_kernel(x_ref, oa_ref, za_ref, ob_ref, zb_ref, ga_ref, gb_ref, woa_ref, wob_ref, wout_ref,
                  fg_ref, xn_ref, *maybe_y_ref):
    a = _dot(oa_ref[...] * za_ref[...], woa_ref[0])
    b = _dot(ob_ref[...] * zb_ref[...], wob_ref[0])
    mix = ga_ref[...].astype(F32) * a + gb_ref[...].astype(F32) * b
    xn = x_ref[...] + _dot(mix.astype(BF16), wout_ref[0])
    xn_ref[...] = xn
    if maybe_y_ref:
        maybe_y_ref[0][...] = _rms(xn, fg_ref[...])


def _merge(layer, x, oa, za, ob, zb, ga, gb, woa, wob, wout, fg, tm, final):
    m = x.shape[0]
    row = lambda c: pl.BlockSpec((tm, c), lambda i: (i, 0))
    in_specs = [row(D_MODEL), row(HEADS_W), row(HEADS_W), row(HEADS_W), row(HEADS_W), row(D_MODEL),
                row(D_MODEL), _layer_spec(woa, layer), _layer_spec(wob, layer), _layer_spec(wout, layer),
                _full_spec(fg.shape)]
    n_out = 2 if final else 1
    out_shape = tuple(jax.ShapeDtypeStruct((m, D_MODEL), F32) for _ in range(n_out))
    out_specs = tuple(row(D_MODEL) for _ in range(n_out))
    return pl.pallas_call(
        _merge_kernel, grid=(m // tm,), in_specs=in_specs, out_specs=out_specs, out_shape=out_shape,
        compiler_params=_params(1), name="merge",
    )(x, oa, za, ob, zb, ga, gb, woa, wob, wout, fg)


def _pad_cols(w, width):
    return jnp.pad(w, ((0, 0), (0, width - w.shape[1])))


def _pad_last(w, width):
    return jnp.pad(w, [(0, 0)] * (w.ndim - 1) + [(0, width - w.shape[-1])])


def _stacked_weights(w_in, w_uq, w_ukv, w_oa, w_ob, w_out, b_f, norm_g, g_q, g_kv):
    depth = w_in.shape[0]
    offs = np.cumsum(IN_SIZES)[:-1].tolist()
    w_cq, w_ckv, w_kr, w_za, w_fq, w_fk, w_fv, w_zf, w_zb, w_ga, w_gb = jnp.split(w_in, offs, axis=2)
    w_kr_swapped = jnp.concatenate([w_kr[..., HALF_ROPE:], w_kr[..., :HALF_ROPE]], axis=2)
    w_mla = jnp.concatenate([w_cq, w_ckv, _pad_last(w_kr, LANES), _pad_last(w_kr_swapped, LANES),
                             w_za, w_ga], axis=2).astype(BF16)
    w_fox = jnp.concatenate([w_fk, w_fv, _pad_last(w_zf, LANES), w_zb, w_gb], axis=2).astype(BF16)
    uq = w_uq.reshape(depth, Q_LORA, N_HEADS, NOPE_DIM + ROPE_DIM)
    uq_rows = jnp.concatenate([
        uq[..., :NOPE_DIM].reshape(depth, Q_LORA, -1),
        uq[..., NOPE_DIM:NOPE_DIM + HALF_ROPE].reshape(depth, Q_LORA, -1),
        uq[..., NOPE_DIM + HALF_ROPE:].reshape(depth, Q_LORA, -1)], axis=2)
    ukv = w_ukv.reshape(depth, KV_LORA, N_HEADS, NOPE_DIM + HEAD_DV)
    w_ukn = ukv[..., :NOPE_DIM].reshape(depth, KV_LORA, -1).astype(BF16)
    w_uv = ukv[..., NOPE_DIM:].reshape(depth, KV_LORA, -1).astype(BF16)
    tr = lambda a: jnp.swapaxes(a, 1, 2)
    return dict(
        w_mla=w_mla, w_fox=w_fox, w_uqT=tr(uq_rows).astype(BF16), w_ukn=w_ukn, w_uv=w_uv, w_uvT=tr(w_uv),
        w_fqT=tr(w_fq).astype(BF16),
        b_f=_pad_last(b_f[:, None, :], LANES),
        w_oa=w_oa.astype(BF16), w_ob=w_ob.astype(BF16), w_out=w_out.astype(BF16),
        norm_g=norm_g[:, None, :], g_q=g_q[:, None, :], g_kv=g_kv[:, None, :])


def _rope_tables(pos):
    inv = jnp.exp(-math.log(ROPE_THETA) * jnp.arange(HALF_ROPE, dtype=F32) / HALF_ROPE)
    ang = pos.astype(F32)[:, None] * inv[None, :]
    cos = jnp.cos(ang)
    sin = jnp.sin(ang)
    cosT = jnp.tile(cos.T, (N_HEADS, 1))
    sinT = jnp.tile(sin.T, (N_HEADS, 1))
    ctok = _pad_cols(jnp.concatenate([cos, cos], axis=1), LANES)
    stok = _pad_cols(jnp.concatenate([-sin, sin], axis=1), LANES)
    return cosT, sinT, ctok, stok


def _bias_placement():
    pk = np.zeros((3 * LANES, LANES), np.float32)
    pqt = np.zeros((N_HEADS * AUG_PER_HEAD, 3 * LANES), np.float32)
    for c in range(3):
        for h in range(N_HEADS):
            pk[c * LANES + h, h * AUG_PER_HEAD + 3 + c] = -1.0
            pqt[h * AUG_PER_HEAD + c, c * LANES + h] = 1.0
    return jnp.asarray(pk, BF16), jnp.asarray(pqt, BF16)


def _head_groups():
    grp = np.zeros((HEADS_W, LANES), np.float32)
    for h in range(N_HEADS):
        grp[h * HEAD_DV:(h + 1) * HEAD_DV, h] = 1.0
    return jnp.asarray(grp, BF16)


def _skip_tables(base, qn2, kn2):
    b = base[:, 0, :N_HEADS]
    qmax = jnp.sqrt(qn2[:, :, 0]) * NORM_SLACK
    kmax = lax.cummax(jnp.sqrt(kn2[:, 0, :N_HEADS]) * NORM_SLACK, axis=0)
    return jnp.stack([b, qmax, kmax]).transpose(0, 2, 1)


def _tri(n):
    return jnp.asarray(np.tril(np.ones((n, n), np.float32)), BF16)


def _block_diag_queries(qT, nbatch, t_new):
    q = qT.T.reshape(nbatch, t_new, N_HEADS, HEAD_DV)
    eye = jnp.eye(N_HEADS, dtype=q.dtype)
    return jnp.einsum('bthj,hg->bhtgj', q, eye).reshape(nbatch, N_HEADS * t_new, HEADS_W)


def _project(layer, x, sw, tables, tri, pk, pqt, grp, tm, stack_mla=None, stack_fox=None):
    cosT, sinT, ctok, stok = tables
    mla = _proj_mla(layer, x, sw['norm_g'], sw['w_mla'], sw['g_q'], sw['g_kv'], sw['w_uqT'], sw['w_ukn'],
                    sw['w_uvT'], cosT, sinT, ctok, stok, grp, tm, stack_mla)
    fox = _proj_fox(layer, x, sw['norm_g'], sw['w_fox'], sw['w_fqT'], sw['b_f'], tri, pk, pqt, grp,
                    tm, stack_fox)
    return mla, fox


def kernel(x_prompt, x_sample, cache_mla_ckv, cache_mla_krope, cache_fox_k, cache_fox_v, cache_fox_logf,
           norm_g, w_in, g_q, w_uq, g_kv, w_ukv, b_f, w_oa, w_ob, w_out, final_g):
    depth = w_in.shape[0]
    _, seq, _ = x_prompt.shape
    nbatch, t_new, _ = x_sample.shape
    past = cache_mla_ckv.shape[2]
    m_s = nbatch * t_new
    key_pad = -(past + t_new) % LANES

    tabs_p = _rope_tables(jnp.arange(seq, dtype=jnp.int32))
    tabs_s = _rope_tables(past + jnp.arange(m_s, dtype=jnp.int32) % t_new)
    pk, pqt = _bias_placement()
    grp = _head_groups()
    tri_p = _tri(SEQ_BLOCK)
    tri_s = _tri(m_s)
    fg = final_g[None, :]

    xp = x_prompt.reshape(seq, D_MODEL)
    xs = x_sample.reshape(m_s, D_MODEL)
    outs = {k: [] for k in ('p_kr', 'p_lf', 's_ckv', 's_kr', 's_fk', 's_fv', 's_lf')}
    p_ckv_stack = p_kv_stack = None
    yp = ys = None
    sw = _stacked_weights(w_in, w_uq, w_ukv, w_oa, w_ob, w_out, b_f, norm_g, g_q, g_kv)
    for l in range(depth):
        final = l == depth - 1

        mla, fox = _project(l, xp, sw, tabs_p, tri_p, pk, pqt, grp, SEQ_BLOCK,
                            (l, depth, p_ckv_stack), (l, depth, p_kv_stack))
        ckv, kr, za, ga, qnT, qrT, knb, krb, vT, qn2_a, kn2_a = mla
        fk, fv, logf, zb, gb, fqT, fkb, fvT, augk, augqT, base, qn2, kn2 = fox
        tab = _skip_tables(base, qn2, kn2)
        tab_a = _skip_tables(jnp.zeros_like(base), qn2_a, kn2_a)
        o_a = _attention(tab_a, qnT, qrT, knb, krb, vT, fox=False)
        o_b = _attention(tab, fqT, augqT, fkb, augk, fvT, fox=True)
        res = _merge(l, xp, o_a, za, o_b, zb, ga, gb, sw['w_oa'], sw['w_ob'], sw['w_out'], fg, SEQ_BLOCK, final)
        xp = res[0]
        if final:
            yp = res[1]
        p_ckv_stack, p_kv_stack = (ckv,), (fk, fv)
        outs['p_kr'].append(kr[:, :ROPE_DIM]); outs['p_lf'].append(logf)

        mla, fox = _project(l, xs, sw, tabs_s, tri_s, pk, pqt, grp, m_s)
        ckv, kr, za, ga, qnT, qrT = mla[:6]
        fk, fv, logf, zb, gb, fqT = fox[:6]
        qn_bd = _block_diag_queries(qnT, nbatch, t_new)
        qf_bd = _block_diag_queries(fqT, nbatch, t_new)
        qr = qrT.reshape(N_HEADS, ROPE_DIM, nbatch, t_new).transpose(2, 0, 3, 1).reshape(
            nbatch, N_HEADS * t_new, ROPE_DIM)
        lf_all = jnp.concatenate([cache_fox_logf[l], logf.reshape(nbatch, t_new, N_HEADS),
                                  jnp.zeros((nbatch, key_pad, N_HEADS), F32)], axis=1)
        lrep = jnp.repeat(lf_all.transpose(0, 2, 1), t_new, axis=1)
        o_a, o_b = _sample_attention(
            l, qn_bd, qr, qf_bd, lrep, cache_mla_ckv, ckv, cache_mla_krope, kr,
            cache_fox_k.reshape(depth, nbatch, past, HEADS_W), fk,
            cache_fox_v.reshape(depth, nbatch, past, HEADS_W), fv, sw['w_ukn'], sw['w_uv'])
        res = _merge(l, xs, o_a, za, o_b, zb, ga, gb, sw['w_oa'], sw['w_ob'], sw['w_out'], fg, m_s, final)
        xs = res[0]
        if final:
            ys = res[1]
        outs['s_ckv'].append(ckv); outs['s_kr'].append(kr[:, :ROPE_DIM]); outs['s_fk'].append(fk)
        outs['s_fv'].append(fv); outs['s_lf'].append(logf)

    st = lambda name, shape: jnp.stack(outs[name]).reshape((depth,) + shape)
    return (yp.reshape(1, seq, D_MODEL), ys.reshape(nbatch, t_new, D_MODEL),
            p_ckv_stack[0].reshape(depth, 1, seq, KV_LORA), st('p_kr', (1, seq, ROPE_DIM)),
            p_kv_stack[0].reshape(depth, 1, seq, N_HEADS, HEAD_DV),
            p_kv_stack[1].reshape(depth, 1, seq, N_HEADS, HEAD_DV),
            st('p_lf', (1, seq, N_HEADS)),
            st('s_ckv', (nbatch, t_new, KV_LORA)), st('s_kr', (nbatch, t_new, ROPE_DIM)),
            st('s_fk', (nbatch, t_new, N_HEADS, HEAD_DV)), st('s_fv', (nbatch, t_new, N_HEADS, HEAD_DV)),
            st('s_lf', (nbatch, t_new, N_HEADS)))
```

```python
import functools
import math

import numpy as np
import jax
import jax.numpy as jnp
from jax import lax
from jax.experimental import pallas as pl
from jax.experimental.pallas import tpu as pltpu

D_MODEL = 1024
N_HEADS = 8
NOPE_DIM = 64
ROPE_DIM = 32
HALF_ROPE = ROPE_DIM // 2
HEAD_DV = 64
Q_LORA = 384
KV_LORA = 256
HEADS_W = N_HEADS * HEAD_DV
CHUNK = 64
ROPE_THETA = 10000.0
LOG2E = math.log2(math.e)
MLA_QSCALE = LOG2E / math.sqrt(NOPE_DIM + ROPE_DIM)
FOX_QSCALE = LOG2E / math.sqrt(HEAD_DV)
EPS = 1e-6
NEG = -1e30
IN_SIZES = (Q_LORA, KV_LORA, ROPE_DIM, HEADS_W, HEADS_W, HEADS_W, HEADS_W, N_HEADS, HEADS_W, D_MODEL, D_MODEL)

LANES = 128
SEQ_BLOCK = 512
KEY_SUB = 256
DEN_ROWS = 16
HEADS_PER_STEP = 8
LOOKAHEAD = 2
SAFE_LOG2 = 100.0
SKIP_LOG2 = 64.0
NORM_SLACK = 1.01
AUG_PER_HEAD = 8
VMEM_LIMIT_BYTES = 56 * 1024 * 1024

F32 = jnp.float32
BF16 = jnp.bfloat16


def _dot(a, b):
    return jnp.dot(a, b, preferred_element_type=F32)


def _dot_nt(a, b):
    return lax.dot_general(a, b, (((1,), (1,)), ((), ())), preferred_element_type=F32)


def _rms(x, g):
    return x * lax.rsqrt(jnp.mean(x * x, axis=-1, keepdims=True) + EPS) * g


def _sigmoid(x):
    return 1.0 / (1.0 + jnp.exp(-x))


def _silu(x):
    return x * _sigmoid(x)


def _split3(x):
    hi = x.astype(BF16)
    r1 = x - hi.astype(F32)
    mid = r1.astype(BF16)
    lo = (r1 - mid.astype(F32)).astype(BF16)
    return hi, mid, lo


def _full_spec(shape):
    nd = len(shape)
    return pl.BlockSpec(shape, lambda *_: (0,) * nd)


def _layer_spec(arr, layer):
    nd = arr.ndim
    return pl.BlockSpec((1,) + arr.shape[1:], lambda *_: (layer,) + (0,) * (nd - 1))


def _put(ref, val, earlier=()):
    if len(ref.shape) == 2:
        ref[...] = val
    else:
        for k, prev_ref in enumerate(earlier):
            ref[k] = prev_ref[...]
        ref[len(earlier)] = val


def _with_earlier(body, n_in, n_earlier, *refs):
    return body(*refs[:n_in], *refs[n_in + n_earlier:], earlier=refs[n_in:n_in + n_earlier])


def _stacked(earlier, depth, m, tm, cols):
    if earlier is None or len(earlier) < depth - 1:
        return jax.ShapeDtypeStruct((m, cols), F32), pl.BlockSpec((tm, cols), lambda i: (i, 0))
    return (jax.ShapeDtypeStruct((depth, m, cols), F32),
            pl.BlockSpec((depth, tm, cols), lambda i: (0, i, 0)))


def _earlier_args(earlier, depth, tm, groups=1):
    if earlier is None or len(earlier) < depth - 1:
        return [], []
    flat = [a for layer in zip(*earlier) for a in layer] if groups > 1 else list(earlier)
    return flat, [pl.BlockSpec((tm, a.shape[1]), lambda i: (i, 0)) for a in flat]


def _params(n_axes):
    return pltpu.CompilerParams(dimension_semantics=("arbitrary",) * n_axes,
                                vmem_limit_bytes=VMEM_LIMIT_BYTES)


MLA_IN_COLS = Q_LORA + KV_LORA + 2 * LANES + HEADS_W + D_MODEL


def _proj_mla_kernel(x_ref, g_ref, w_ref, gq_ref, gkv_ref, wuq_ref, wukn_ref, wuvt_ref,
                     cosT_ref, sinT_ref, ctok_ref, stok_ref, grp_ref,
                     ckv_ref, kr_ref, za_ref, ga_ref, qnT_ref, qrT_ref, knb_ref, krb_ref, vT_ref,
                     qn2_ref, kn2_ref, earlier=()):
    tm = x_ref.shape[0]
    hb = _rms(x_ref[...], g_ref[0]).astype(BF16)
    o = 0
    zcq = _dot(hb, w_ref[0, :, o:o + Q_LORA]); o += Q_LORA
    zckv = _dot(hb, w_ref[0, :, o:o + KV_LORA]); o += KV_LORA
    zka = _dot(hb, w_ref[0, :, o:o + LANES]); o += LANES
    zkb = _dot(hb, w_ref[0, :, o:o + LANES]); o += LANES
    za_ref[...] = _silu(_dot(hb, w_ref[0, :, o:o + HEADS_W])).astype(BF16); o += HEADS_W
    ga_ref[...] = _sigmoid(_dot(hb, w_ref[0, :, o:o + D_MODEL])).astype(BF16)

    cqb = _rms(zcq, gq_ref[0]).astype(BF16)
    qT = _dot_nt(wuq_ref[0], cqb)
    qn = qT[0:HEADS_W] * MLA_QSCALE
    qnT_ref[...] = qn.astype(BF16)
    x1 = qT[HEADS_W:HEADS_W + LANES]
    x2 = qT[HEADS_W + LANES:HEADS_W + 2 * LANES]
    c = cosT_ref[...]
    s = sinT_ref[...]
    r1 = (x1 * c - x2 * s) * MLA_QSCALE
    r2 = (x1 * s + x2 * c) * MLA_QSCALE
    n1 = r1.astype(BF16)
    n2 = r2.astype(BF16)
    for h in range(N_HEADS):
        qrT_ref[h * ROPE_DIM:h * ROPE_DIM + HALF_ROPE, :] = n1[h * HALF_ROPE:(h + 1) * HALF_ROPE]
        qrT_ref[h * ROPE_DIM + HALF_ROPE:(h + 1) * ROPE_DIM, :] = n2[h * HALF_ROPE:(h + 1) * HALF_ROPE]
    qn2 = (jnp.sum((qn * qn).reshape(N_HEADS, NOPE_DIM, tm), axis=1)
           + jnp.sum((r1 * r1 + r2 * r2).reshape(N_HEADS, HALF_ROPE, tm), axis=1))
    qn2_ref[0] = jnp.broadcast_to(jnp.max(qn2, axis=1, keepdims=True), (N_HEADS, LANES))

    ckv = _rms(zckv, gkv_ref[0])
    _put(ckv_ref, ckv, earlier)
    cb = ckv.astype(BF16)
    kn = _dot(cb, wukn_ref[0])
    knb_ref[0] = kn.astype(BF16)
    vT_ref[0] = _dot_nt(wuvt_ref[0], cb).astype(BF16)

    kr = zka * ctok_ref[...] + zkb * stok_ref[...]
    kr_ref[...] = kr
    krb_ref[0] = kr.astype(BF16)
    kn2 = _dot((kn * kn).astype(BF16), grp_ref[...]) + jnp.sum(kr * kr, axis=1, keepdims=True)
    kn2_ref[0] = jnp.max(kn2, axis=0, keepdims=True)


def _proj_mla(layer, x, g, w, gq, gkv, wuq, wukn, wuvt, cosT, sinT, ctok, stok, grp, tm, depth, earlier=None):
    m = x.shape[0]
    nb = m // tm
    row = lambda c: pl.BlockSpec((tm, c), lambda i: (i, 0))
    colT = lambda r: pl.BlockSpec((r, tm), lambda i: (0, i))
    blk3 = lambda a, b: pl.BlockSpec((1, a, b), lambda i: (i, 0, 0))
    ckv_shape, ckv_spec = _stacked(earlier, depth, m, tm, KV_LORA)
    out_shape = (
        ckv_shape,
        jax.ShapeDtypeStruct((m, LANES), F32),
        jax.ShapeDtypeStruct((m, HEADS_W), BF16),
        jax.ShapeDtypeStruct((m, D_MODEL), BF16),
        jax.ShapeDtypeStruct((HEADS_W, m), BF16),
        jax.ShapeDtypeStruct((N_HEADS * ROPE_DIM, m), BF16),
        jax.ShapeDtypeStruct((nb, tm, HEADS_W), BF16),
        jax.ShapeDtypeStruct((nb, tm, LANES), BF16),
        jax.ShapeDtypeStruct((nb, HEADS_W, tm), BF16),
        jax.ShapeDtypeStruct((nb, N_HEADS, LANES), F32),
        jax.ShapeDtypeStruct((nb, 1, LANES), F32),
    )
    out_specs = (ckv_spec, row(LANES), row(HEADS_W), row(D_MODEL), colT(HEADS_W),
                 colT(N_HEADS * ROPE_DIM), blk3(tm, HEADS_W), blk3(tm, LANES), blk3(HEADS_W, tm),
                 blk3(N_HEADS, LANES), blk3(1, LANES))
    in_specs = [row(D_MODEL)] + [_layer_spec(a, layer) for a in (g, w, gq, gkv, wuq, wukn, wuvt)] + [
        colT(LANES), colT(LANES), row(LANES), row(LANES), _full_spec(grp.shape)]
    prev, prev_specs = _earlier_args(earlier, depth, tm)
    return pl.pallas_call(
        functools.partial(_with_earlier, _proj_mla_kernel, len(in_specs), len(prev)),
        grid=(nb,), in_specs=in_specs + prev_specs, out_specs=out_specs, out_shape=out_shape,
        compiler_params=_params(1), name="proj_mla",
    )(x, g, w, gq, gkv, wuq, wukn, wuvt, cosT, sinT, ctok, stok, grp, *prev)


FOX_IN_COLS = 2 * HEADS_W + LANES + HEADS_W + D_MODEL


def _proj_fox_kernel(x_ref, g_ref, w_ref, wfqt_ref, bf_ref, tri_ref, pk_ref, pqt_ref, grp_ref,
                     fk_ref, fv_ref, logf_ref, zb_ref, gb_ref, fqT_ref, fkb_ref, fvT_ref,
                     augk_ref, augqT_ref, base_ref, qn2_ref, kn2_ref, carry_ref, earlier=()):
    i = pl.program_id(0)
    tm = x_ref.shape[0]
    hb = _rms(x_ref[...], g_ref[0]).astype(BF16)
    o = 0
    fk = _dot(hb, w_ref[0, :, o:o + HEADS_W]); o += HEADS_W
    _put(fk_ref, fk, earlier[:len(earlier) // 2])
    fkb_ref[0] = fk.astype(BF16)
    kn2_ref[0] = jnp.max(_dot((fk * fk).astype(BF16), grp_ref[...]), axis=0, keepdims=True)
    fv = _dot(hb, w_ref[0, :, o:o + HEADS_W]); o += HEADS_W
    _put(fv_ref, fv, earlier[len(earlier) // 2:])
    fvT_ref[0] = fv.T.astype(BF16)
    zf = _dot(hb, w_ref[0, :, o:o + LANES]) + bf_ref[0]; o += LANES
    zb_ref[...] = _silu(_dot(hb, w_ref[0, :, o:o + HEADS_W])).astype(BF16); o += HEADS_W
    gb_ref[...] = _sigmoid(_dot(hb, w_ref[0, :, o:o + D_MODEL])).astype(BF16)
    fqT = _dot_nt(wfqt_ref[0], hb) * FOX_QSCALE
    fqT_ref[...] = fqT.astype(BF16)
    qn2 = jnp.sum((fqT * fqT).reshape(N_HEADS, HEAD_DV, tm), axis=1)
    qn2_ref[0] = jnp.broadcast_to(jnp.max(qn2, axis=1, keepdims=True), (N_HEADS, LANES))

    lane = lax.broadcasted_iota(jnp.int32, (tm, LANES), 1)
    logf = jnp.minimum(zf, 0.0) - jnp.log(1.0 + jnp.exp(-jnp.abs(zf)))
    logf = jnp.where(lane < N_HEADS, logf, 0.0)
    logf_ref[...] = logf[:, 0:N_HEADS]

    hi, mid, lo = _split3(logf)
    packed = (hi.astype(F32) + pltpu.roll(mid.astype(F32), N_HEADS, 1)
              + pltpu.roll(lo.astype(F32), 2 * N_HEADS, 1)).astype(BF16)
    cum3 = _dot(tri_ref[...], packed)
    r = cum3 + pltpu.roll(cum3, LANES - N_HEADS, 1) + pltpu.roll(cum3, LANES - 2 * N_HEADS, 1)
    r = jnp.where(lane < N_HEADS, r, 0.0)

    @pl.when(i == 0)
    def _():
        carry_ref[...] = jnp.zeros_like(carry_ref)

    base_ref[0] = carry_ref[...] * LOG2E
    carry_ref[...] = carry_ref[...] + r[tm - 1:tm, :]

    rh, rm, rl = _split3(r * LOG2E)
    rcat = jnp.concatenate([rh, rm, rl], axis=1)
    slot = lane % AUG_PER_HEAD
    ones_k = jnp.where((lane < N_HEADS * AUG_PER_HEAD) & (slot < 3), 1.0, 0.0)
    augk_ref[0] = (_dot(rcat, pk_ref[...]) + ones_k).astype(BF16)
    rowq = lax.broadcasted_iota(jnp.int32, (N_HEADS * AUG_PER_HEAD, tm), 0) % AUG_PER_HEAD
    ones_q = jnp.where((rowq >= 3) & (rowq < 6), 1.0, 0.0)
    augqT_ref[...] = (_dot_nt(pqt_ref[...], rcat) + ones_q).astype(BF16)


def _proj_fox(layer, x, g, w, wfqt, bf, tri, pk, pqt, grp, tm, depth, earlier=None):
    m = x.shape[0]
    nb = m // tm
    row = lambda c: pl.BlockSpec((tm, c), lambda i: (i, 0))
    colT = lambda r: pl.BlockSpec((r, tm), lambda i: (0, i))
    blk3 = lambda a, b: pl.BlockSpec((1, a, b), lambda i: (i, 0, 0))
    kv_shape, kv_spec = _stacked(earlier, depth, m, tm, HEADS_W)
    out_shape = (
        kv_shape,
        kv_shape,
        jax.ShapeDtypeStruct((m, N_HEADS), F32),
        jax.ShapeDtypeStruct((m, HEADS_W), BF16),
        jax.ShapeDtypeStruct((m, D_MODEL), BF16),
        jax.ShapeDtypeStruct((HEADS_W, m), BF16),
        jax.ShapeDtypeStruct((nb, tm, HEADS_W), BF16),
        jax.ShapeDtypeStruct((nb, HEADS_W, tm), BF16),
        jax.ShapeDtypeStruct((nb, tm, LANES), BF16),
        jax.ShapeDtypeStruct((N_HEADS * AUG_PER_HEAD, m), BF16),
        jax.ShapeDtypeStruct((nb, 1, LANES), F32),
        jax.ShapeDtypeStruct((nb, N_HEADS, LANES), F32),
        jax.ShapeDtypeStruct((nb, 1, LANES), F32),
    )
    out_specs = (kv_spec, kv_spec, row(N_HEADS), row(HEADS_W), row(D_MODEL), colT(HEADS_W),
                 blk3(tm, HEADS_W), blk3(HEADS_W, tm), blk3(tm, LANES), colT(N_HEADS * AUG_PER_HEAD),
                 blk3(1, LANES), blk3(N_HEADS, LANES), blk3(1, LANES))
    in_specs = ([row(D_MODEL)] + [_layer_spec(a, layer) for a in (g, w, wfqt, bf)]
                + [_full_spec(a.shape) for a in (tri, pk, pqt, grp)])
    prev, prev_specs = _earlier_args(earlier, depth, tm, groups=2)
    return pl.pallas_call(
        functools.partial(_with_earlier, _proj_fox_kernel, len(in_specs), len(prev)),
        grid=(nb,), in_specs=in_specs + prev_specs, out_specs=out_specs, out_shape=out_shape,
        scratch_shapes=[pltpu.VMEM((1, LANES), F32)],
        compiler_params=_params(1), name="proj_fox",
    )(x, g, w, wfqt, bf, tri, pk, pqt, grp, *prev)


def _attn_kernel(tab_ref, qp_ref, qe_ref, ka_ref, ke_ref, vT_ref, o_ref, acc_ref, s_ref, *, fox):
    g = pl.program_id(0)
    i = pl.program_id(1)
    tq = qp_ref.shape[1]
    tk = ka_ref.shape[1]
    rows_p = lax.broadcasted_iota(jnp.int32, (LANES, tq), 0)
    rows_e = lax.broadcasted_iota(jnp.int32, qe_ref.shape, 0)
    kpos = lax.broadcasted_iota(jnp.int32, (tk, tq), 0)
    qpos = lax.broadcasted_iota(jnp.int32, (tk, tq), 1)
    if fox:
        visible = kpos <= qpos
    else:
        visible = (kpos // CHUNK) <= (qpos // CHUNK)
    qe = qe_ref[...]

    ws = []
    for hq in range(HEADS_PER_STEP):
        pair, hh = divmod(hq, 2)
        qp = qp_ref[pair * LANES:(pair + 1) * LANES, :]
        keep_p = (rows_p >= hh * HEAD_DV) & (rows_p < (hh + 1) * HEAD_DV)
        if fox:
            h = HEADS_PER_STEP * g + hq
            keep_e = (rows_e >= h * AUG_PER_HEAD) & (rows_e < (h + 1) * AUG_PER_HEAD)
            extra = jnp.where(keep_e, qe, jnp.zeros_like(qe))
        else:
            extra = qe[hq * ROPE_DIM:(hq + 1) * ROPE_DIM]
        zero_rows = jnp.zeros((LANES - extra.shape[0], tq), BF16)
        ws.append(jnp.concatenate([jnp.where(keep_p, qp, jnp.zeros_like(qp)), extra, zero_rows], axis=0))

    steps = [(u, hq) for u in range(tk // KEY_SUB) for hq in range(HEADS_PER_STEP)]
    ones_rows = jnp.ones((DEN_ROWS, KEY_SUB), BF16)

    def scores(j, step):
        u, hq = step
        pair = hq // 2
        rows = slice(u * KEY_SUB, (u + 1) * KEY_SUB)
        lhs = jnp.concatenate([ka_ref[j, rows, pair * LANES:(pair + 1) * LANES], ke_ref[j, rows, :]], axis=1)
        return _dot(lhs, ws[hq])

    def block(j, stats, masked, j_next):
        stats = list(stats)
        n_steps = len(steps)
        tiles = {}
        for n, (u, hq) in enumerate(steps):
            ahead = n + LOOKAHEAD
            if ahead < n_steps:
                tiles[ahead] = scores(j, steps[ahead])
            elif j_next is not None:
                tiles[ahead] = scores(j_next, steps[ahead - n_steps])
            sT = tiles.pop(n) if n in tiles else s_ref[n]
            rows = slice(u * KEY_SUB, (u + 1) * KEY_SUB)
            m_old = stats[hq]
            if masked:
                sT = jnp.where(visible[rows], sT, NEG)
            if fox:
                h = HEADS_PER_STEP * g + hq
                d = tab_ref[0, h, i] - tab_ref[0, h, j]
            else:
                d = 0.0
            m_new = jnp.maximum(m_old, jnp.max(sT, axis=0, keepdims=True) + d)
            alpha = jnp.exp2(m_old - m_new)
            pT = jnp.exp2((sT - (m_new - d)).astype(BF16))
            stats[hq] = m_new
            v = jnp.concatenate([vT_ref[j, hq * HEAD_DV:(hq + 1) * HEAD_DV, rows], ones_rows], axis=0)
            acc_ref[hq] = alpha * acc_ref[hq] + _dot(v, pT)
        for n, tile in tiles.items():
            s_ref[n - n_steps] = tile
        return tuple(stats)

    def more(j_top, stats):
        alive = j_top >= 0
        if fox:
            jt = jnp.maximum(j_top, 0)
            slack = None
            for hq in range(HEADS_PER_STEP):
                h = HEADS_PER_STEP * g + hq
                bound = tab_ref[1, h, i] * tab_ref[2, h, jt] + (tab_ref[0, h, i] - tab_ref[0, h, jt + 1])
                room = stats[hq] - bound
                slack = room if slack is None else jnp.minimum(slack, room)
            alive = alive & (jnp.min(slack) < SKIP_LOG2)
        return alive.astype(jnp.int32)

    def block_unshifted(j, masked, j_next):
        n_steps = len(steps)
        tiles = {}
        for n, (u, hq) in enumerate(steps):
            ahead = n + LOOKAHEAD
            if ahead < n_steps:
                tiles[ahead] = scores(j, steps[ahead])
            else:
                tiles[ahead] = scores(j_next, steps[ahead - n_steps])
            sT = tiles.pop(n) if n in tiles else s_ref[n]
            rows = slice(u * KEY_SUB, (u + 1) * KEY_SUB)
            if masked:
                sT = jnp.where(visible[rows], sT, NEG)
            pT = jnp.exp2(sT).astype(BF16)
            v = jnp.concatenate([vT_ref[j, hq * HEAD_DV:(hq + 1) * HEAD_DV, rows], ones_rows], axis=0)
            acc_ref[hq] = acc_ref[hq] + _dot(v, pT)
        for n, tile in tiles.items():
            s_ref[n - n_steps] = tile

    def walk_online():
        m0 = jnp.full((1, tq), NEG, F32)
        stats = block(i, (m0,) * HEADS_PER_STEP, True, jnp.maximum(i - 1, 0))

        def visit(c):
            j = c[0]
            st = block(j, c[2:], False, jnp.maximum(j - 1, 0))
            return (j - 1, more(j - 1, st)) + tuple(st)

        lax.while_loop(lambda c: c[1] > 0, visit, (i - 1, more(i - 1, stats)) + tuple(stats))

    def walk_unshifted():
        block_unshifted(i, True, jnp.maximum(i - 1, 0))

        def visit(k, c):
            j = i - 1 - k
            block_unshifted(j, False, jnp.maximum(j - 1, 0))
            return c

        lax.fori_loop(0, i, visit, 0)

    acc_ref[...] = jnp.zeros_like(acc_ref)
    for n in range(LOOKAHEAD):
        s_ref[n] = scores(i, steps[n])
    if fox:
        walk_online()
    else:
        bounded = None
        for hq in range(HEADS_PER_STEP):
            h = HEADS_PER_STEP * g + hq
            ok = tab_ref[1, h, i] * tab_ref[2, h, i] <= SAFE_LOG2
            bounded = ok if bounded is None else bounded & ok
        pl.when(bounded)(walk_unshifted)
        pl.when(jnp.logical_not(bounded))(walk_online)
    outs = [acc_ref[hq, 0:HEAD_DV, :] / acc_ref[hq, HEAD_DV:HEAD_DV + 1, :] for hq in range(HEADS_PER_STEP)]
    o_ref[...] = jnp.concatenate(outs, axis=0).T.astype(BF16)


def _attention(tab, qpT, qeT, ka, ke, vT, fox):
    nb, tk, _ = ka.shape
    s = qpT.shape[1]
    tq = tk
    width = HEADS_PER_STEP * HEAD_DV
    e_rows = qeT.shape[0] if fox else HEADS_PER_STEP * ROPE_DIM
    e_map = (lambda g, i, b: (0, i)) if fox else (lambda g, i, b: (g, i))
    once = dict(pipeline_mode=pl.Buffered(1))
    grid_spec = pltpu.PrefetchScalarGridSpec(
        num_scalar_prefetch=1,
        grid=(N_HEADS // HEADS_PER_STEP, nb),
        in_specs=[
            pl.BlockSpec((width, tq), lambda g, i, b: (g, i)),
            pl.BlockSpec((e_rows, tq), e_map),
            pl.BlockSpec((nb, tk, width), lambda g, i, b: (0, 0, g), **once),
            pl.BlockSpec((nb, tk, LANES), lambda g, i, b: (0, 0, 0), **once),
            pl.BlockSpec((nb, width, tk), lambda g, i, b: (0, g, 0), **once),
        ],
        out_specs=pl.BlockSpec((tq, width), lambda g, i, b: (i, g)),
        scratch_shapes=[pltpu.VMEM((HEADS_PER_STEP, HEAD_DV + DEN_ROWS, tq), F32),
                        pltpu.VMEM((LOOKAHEAD, KEY_SUB, tq), F32)],
    )
    return pl.pallas_call(
        functools.partial(_attn_kernel, fox=fox), grid_spec=grid_spec,
        out_shape=jax.ShapeDtypeStruct((s, HEADS_W), BF16),
        compiler_params=_params(2), name="attn_fox" if fox else "attn_mla",
    )(tab, qpT, qeT, ka, ke, vT)


def _softmax_pv(s, v):
    m = jnp.max(s, axis=-1, keepdims=True)
    p = jnp.exp2(s - m)
    l = jnp.sum(p, axis=-1, keepdims=True)
    return _dot(p.astype(BF16), v) / l


def _diag_heads(o_big, t_new):
    rows = lax.broadcasted_iota(jnp.int32, o_big.shape, 0) // t_new
    cols = lax.broadcasted_iota(jnp.int32, o_big.shape, 1) // HEAD_DV
    kept = jnp.where(rows == cols, o_big, 0.0)
    return jnp.sum(kept.reshape(N_HEADS, t_new, o_big.shape[1]), axis=0)


def _sample_attn_kernel(qn_ref, qr_ref, qf_ref, lrep_ref, cckv_ref, nckv_ref, ckr_ref, nkr_ref,
                        cfk_ref, nfk_ref, cfv_ref, nfv_ref, wukn_ref, wuv_ref, oa_ref, ob_ref):
    past = cckv_ref.shape[2]
    t_new = nckv_ref.shape[0]
    rows = N_HEADS * t_new
    pad = lrep_ref.shape[2] - past - t_new
    keys = past + t_new + pad

    def with_new(cache, new):
        return jnp.concatenate([cache, new, jnp.zeros((pad, new.shape[1]), new.dtype)], axis=0)

    kidx = lax.broadcasted_iota(jnp.int32, (rows, keys), 1)
    qpos = past + lax.broadcasted_iota(jnp.int32, (rows, keys), 0) % t_new
    real = kidx < past + t_new

    ckv_all = with_new(cckv_ref[0, 0], nckv_ref[...]).astype(BF16)
    kn = _dot(ckv_all, wukn_ref[0]).astype(BF16)
    va = _dot(ckv_all, wuv_ref[0]).astype(BF16)
    kr = with_new(ckr_ref[0, 0], nkr_ref[:, 0:ROPE_DIM]).astype(BF16)
    s = _dot_nt(qn_ref[0], kn) + _dot_nt(qr_ref[0], kr)
    s = jnp.where(real & ((kidx // CHUNK) <= (qpos // CHUNK)), s, NEG)
    oa_ref[...] = _diag_heads(_softmax_pv(s, va), t_new).astype(BF16)

    fk = with_new(cfk_ref[0, 0], nfk_ref[...]).astype(BF16)
    fv = with_new(cfv_ref[0, 0], nfv_ref[...]).astype(BF16)
    c = lrep_ref[0]
    shift = 1
    while shift < keys:
        c = c + jnp.where(kidx >= shift, pltpu.roll(c, shift, 1), 0.0)
        shift *= 2
    cq = jnp.sum(jnp.where(kidx == qpos, c, 0.0), axis=-1, keepdims=True)
    s = _dot_nt(qf_ref[0], fk) + (cq - c) * LOG2E
    s = jnp.where(real & (kidx <= qpos), s, NEG)
    ob_ref[...] = _diag_heads(_softmax_pv(s, fv), t_new).astype(BF16)


def _sample_attention(layer, qn_bd, qr, qf_bd, lrep, cckv, nckv, ckr, nkr, cfk, nfk, cfv, nfv, wukn, wuv):
    nbatch = cckv.shape[1]
    t_new = nckv.shape[0] // nbatch
    b3 = lambda a: pl.BlockSpec((1,) + a.shape[1:], lambda b: (b, 0, 0))
    cache = lambda a: pl.BlockSpec((1, 1) + a.shape[2:], lambda b: (layer, b, 0, 0))
    new = lambda a: pl.BlockSpec((t_new, a.shape[1]), lambda b: (b, 0))
    in_specs = [b3(qn_bd), b3(qr), b3(qf_bd), b3(lrep), cache(cckv), new(nckv), cache(ckr), new(nkr),
                cache(cfk), new(nfk), cache(cfv), new(nfv), _layer_spec(wukn, layer), _layer_spec(wuv, layer)]
    out = jax.ShapeDtypeStruct((nbatch * t_new, HEADS_W), BF16)
    o_spec = pl.BlockSpec((t_new, HEADS_W), lambda b: (b, 0))
    return pl.pallas_call(
        _sample_attn_kernel, grid=(nbatch,), in_specs=in_specs, out_specs=(o_spec, o_spec),
        out_shape=(out, out), compiler_params=_params(1), name="attn_sample",
    )(qn_bd, qr, qf_bd, lrep, cckv, nckv, ckr, nkr, cfk, nfk, cfv, nfv, wukn, wuv)


def _merge_kernel(x_ref, oa_ref, za_ref, ob_ref, zb_ref, ga_ref, gb_ref, woa_ref, wob_ref, wout_ref,
                  fg_ref, xn_ref, *maybe_y_ref):
    a = _dot(oa_ref[...] * za_ref[...], woa_ref[0])
    b = _dot(ob_ref[...] * zb_ref[...], wob_ref[0])
    mix = ga_ref[...].astype(F32) * a + gb_ref[...].astype(F32) * b
    xn = x_ref[...] + _dot(mix.astype(BF16), wout_ref[0])
    xn_ref[...] = xn
    if maybe_y_ref:
        maybe_y_ref[0][...] = _rms(xn, fg_ref[...])


def _merge(layer, x, oa, za, ob, zb, ga, gb, woa, wob, wout, fg, tm, final):
    m = x.shape[0]
    row = lambda c: pl.BlockSpec((tm, c), lambda i: (i, 0))
    in_specs = [row(D_MODEL), row(HEADS_W), row(HEADS_W), row(HEADS_W), row(HEADS_W), row(D_MODEL),
                row(D_MODEL), _layer_spec(woa, layer), _layer_spec(wob, layer), _layer_spec(wout, layer),
                _full_spec(fg.shape)]
    n_out = 2 if final else 1
    out_shape = tuple(jax.ShapeDtypeStruct((m, D_MODEL), F32) for _ in range(n_out))
    out_specs = tuple(row(D_MODEL) for _ in range(n_out))
    return pl.pallas_call(
        _merge_kernel, grid=(m // tm,), in_specs=in_specs, out_specs=out_specs, out_shape=out_shape,
        compiler_params=_params(1), name="merge",
    )(x, oa, za, ob, zb, ga, gb, woa, wob, wout, fg)


def _pad_cols(w, width):
    return jnp.pad(w, ((0, 0), (0, width - w.shape[1])))


def _pad_last(w, width):
    return jnp.pad(w, [(0, 0)] * (w.ndim - 1) + [(0, width - w.shape[-1])])


def _stacked_weights(w_in, w_uq, w_ukv, w_oa, w_ob, w_out, b_f, norm_g, g_q, g_kv):
    depth = w_in.shape[0]
    offs = np.cumsum(IN_SIZES)[:-1].tolist()
    w_cq, w_ckv, w_kr, w_za, w_fq, w_fk, w_fv, w_zf, w_zb, w_ga, w_gb = jnp.split(w_in, offs, axis=2)
    w_kr_swapped = jnp.concatenate([w_kr[..., HALF_ROPE:], w_kr[..., :HALF_ROPE]], axis=2)
    w_mla = jnp.concatenate([w_cq, w_ckv, _pad_last(w_kr, LANES), _pad_last(w_kr_swapped, LANES),
                             w_za, w_ga], axis=2).astype(BF16)
    w_fox = jnp.concatenate([w_fk, w_fv, _pad_last(w_zf, LANES), w_zb, w_gb], axis=2).astype(BF16)
    uq = w_uq.reshape(depth, Q_LORA, N_HEADS, NOPE_DIM + ROPE_DIM)
    uq_rows = jnp.concatenate([
        uq[..., :NOPE_DIM].reshape(depth, Q_LORA, -1),
        uq[..., NOPE_DIM:NOPE_DIM + HALF_ROPE].reshape(depth, Q_LORA, -1),
        uq[..., NOPE_DIM + HALF_ROPE:].reshape(depth, Q_LORA, -1)], axis=2)
    ukv = w_ukv.reshape(depth, KV_LORA, N_HEADS, NOPE_DIM + HEAD_DV)
    w_ukn = ukv[..., :NOPE_DIM].reshape(depth, KV_LORA, -1).astype(BF16)
    w_uv = ukv[..., NOPE_DIM:].reshape(depth, KV_LORA, -1).astype(BF16)
    tr = lambda a: jnp.swapaxes(a, 1, 2)
    return dict(
        w_mla=w_mla, w_fox=w_fox, w_uqT=tr(uq_rows).astype(BF16), w_ukn=w_ukn, w_uv=w_uv, w_uvT=tr(w_uv),
        w_fqT=tr(w_fq).astype(BF16),
        b_f=_pad_last(b_f[:, None, :], LANES),
        w_oa=w_oa.astype(BF16), w_ob=w_ob.astype(BF16), w_out=w_out.astype(BF16),
        norm_g=norm_g[:, None, :], g_q=g_q[:, None, :], g_kv=g_kv[:, None, :])


def _rope_tables(pos):
    inv = jnp.exp(-math.log(ROPE_THETA) * jnp.arange(HALF_ROPE, dtype=F32) / HALF_ROPE)
    ang = pos.astype(F32)[:, None] * inv[None, :]
    cos = jnp.cos(ang)
    sin = jnp.sin(ang)
    cosT = jnp.tile(cos.T, (N_HEADS, 1))
    sinT = jnp.tile(sin.T, (N_HEADS, 1))
    ctok = _pad_cols(jnp.concatenate([cos, cos], axis=1), LANES)
    stok = _pad_cols(jnp.concatenate([-sin, sin], axis=1), LANES)
    return cosT, sinT, ctok, stok


def _bias_placement():
    pk = np.zeros((3 * LANES, LANES), np.float32)
    pqt = np.zeros((N_HEADS * AUG_PER_HEAD, 3 * LANES), np.float32)
    for c in range(3):
        for h in range(N_HEADS):
            pk[c * LANES + h, h * AUG_PER_HEAD + 3 + c] = -1.0
            pqt[h * AUG_PER_HEAD + c, c * LANES + h] = 1.0
    return jnp.asarray(pk, BF16), jnp.asarray(pqt, BF16)


def _head_groups():
    grp = np.zeros((HEADS_W, LANES), np.float32)
    for h in range(N_HEADS):
        grp[h * HEAD_DV:(h + 1) * HEAD_DV, h] = 1.0
    return jnp.asarray(grp, BF16)


def _skip_tables(base, qn2, kn2):
    b = base[:, 0, :N_HEADS]
    qmax = jnp.sqrt(qn2[:, :, 0]) * NORM_SLACK
    kmax = lax.cummax(jnp.sqrt(kn2[:, 0, :N_HEADS]) * NORM_SLACK, axis=0)
    return jnp.stack([b, qmax, kmax]).transpose(0, 2, 1)


def _tri(n):
    return jnp.asarray(np.tril(np.ones((n, n), np.float32)), BF16)


def _block_diag_queries(qT, nbatch, t_new):
    q = qT.T.reshape(nbatch, t_new, N_HEADS, HEAD_DV)
    eye = jnp.eye(N_HEADS, dtype=q.dtype)
    return jnp.einsum('bthj,hg->bhtgj', q, eye).reshape(nbatch, N_HEADS * t_new, HEADS_W)


def _project(layer, depth, x, sw, tables, tri, pk, pqt, grp, tm, earlier_mla=None, earlier_fox=None):
    cosT, sinT, ctok, stok = tables
    mla = _proj_mla(layer, x, sw['norm_g'], sw['w_mla'], sw['g_q'], sw['g_kv'], sw['w_uqT'], sw['w_ukn'],
                    sw['w_uvT'], cosT, sinT, ctok, stok, grp, tm, depth, earlier_mla)
    fox = _proj_fox(layer, x, sw['norm_g'], sw['w_fox'], sw['w_fqT'], sw['b_f'], tri, pk, pqt, grp,
                    tm, depth, earlier_fox)
    return mla, fox


def kernel(x_prompt, x_sample, cache_mla_ckv, cache_mla_krope, cache_fox_k, cache_fox_v, cache_fox_logf,
           norm_g, w_in, g_q, w_uq, g_kv, w_ukv, b_f, w_oa, w_ob, w_out, final_g):
    depth = w_in.shape[0]
    _, seq, _ = x_prompt.shape
    nbatch, t_new, _ = x_sample.shape
    past = cache_mla_ckv.shape[2]
    m_s = nbatch * t_new
    key_pad = -(past + t_new) % LANES

    tabs_p = _rope_tables(jnp.arange(seq, dtype=jnp.int32))
    tabs_s = _rope_tables(past + jnp.arange(m_s, dtype=jnp.int32) % t_new)
    pk, pqt = _bias_placement()
    grp = _head_groups()
    tri_p = _tri(SEQ_BLOCK)
    tri_s = _tri(m_s)
    fg = final_g[None, :]

    xp = x_prompt.reshape(seq, D_MODEL)
    xs = x_sample.reshape(m_s, D_MODEL)
    outs = {k: [] for k in ('p_kr', 'p_lf', 's_ckv', 's_kr', 's_fk', 's_fv', 's_lf')}
    p_ckv, p_kv = [], []
    yp = ys = None
    sw = _stacked_weights(w_in, w_uq, w_ukv, w_oa, w_ob, w_out, b_f, norm_g, g_q, g_kv)
    for l in range(depth):
        final = l == depth - 1

        mla, fox = _project(l, depth, xp, sw, tabs_p, tri_p, pk, pqt, grp, SEQ_BLOCK, tuple(p_ckv), tuple(p_kv))
        ckv, kr, za, ga, qnT, qrT, knb, krb, vT, qn2_a, kn2_a = mla
        fk, fv, logf, zb, gb, fqT, fkb, fvT, augk, augqT, base, qn2, kn2 = fox
        tab = _skip_tables(base, qn2, kn2)
        tab_a = _skip_tables(jnp.zeros_like(base), qn2_a, kn2_a)
        o_a = _attention(tab_a, qnT, qrT, knb, krb, vT, fox=False)
        o_b = _attention(tab, fqT, augqT, fkb, augk, fvT, fox=True)
        res = _merge(l, xp, o_a, za, o_b, zb, ga, gb, sw['w_oa'], sw['w_ob'], sw['w_out'], fg, SEQ_BLOCK, final)
        xp = res[0]
        if final:
            yp = res[1]
        p_ckv.append(ckv); p_kv.append((fk, fv))
        outs['p_kr'].append(kr[:, :ROPE_DIM]); outs['p_lf'].append(logf)

        mla, fox = _project(l, depth, xs, sw, tabs_s, tri_s, pk, pqt, grp, m_s)
        ckv, kr, za, ga, qnT, qrT = mla[:6]
        fk, fv, logf, zb, gb, fqT = fox[:6]
        qn_bd = _block_diag_queries(qnT, nbatch, t_new)
        qf_bd = _block_diag_queries(fqT, nbatch, t_new)
        qr = qrT.reshape(N_HEADS, ROPE_DIM, nbatch, t_new).transpose(2, 0, 3, 1).reshape(
            nbatch, N_HEADS * t_new, ROPE_DIM)
        lf_all = jnp.concatenate([cache_fox_logf[l], logf.reshape(nbatch, t_new, N_HEADS),
                                  jnp.zeros((nbatch, key_pad, N_HEADS), F32)], axis=1)
        lrep = jnp.repeat(lf_all.transpose(0, 2, 1), t_new, axis=1)
        o_a, o_b = _sample_attention(
            l, qn_bd, qr, qf_bd, lrep, cache_mla_ckv, ckv, cache_mla_krope, kr,
            cache_fox_k.reshape(depth, nbatch, past, HEADS_W), fk,
            cache_fox_v.reshape(depth, nbatch, past, HEADS_W), fv, sw['w_ukn'], sw['w_uv'])
        res = _merge(l, xs, o_a, za, o_b, zb, ga, gb, sw['w_oa'], sw['w_ob'], sw['w_out'], fg, m_s, final)
        xs = res[0]
        if final:
            ys = res[1]
        outs['s_ckv'].append(ckv); outs['s_kr'].append(kr[:, :ROPE_DIM]); outs['s_fk'].append(fk)
        outs['s_fv'].append(fv); outs['s_lf'].append(logf)

    st = lambda name, shape: jnp.stack(outs[name]).reshape((depth,) + shape)
    return (yp.reshape(1, seq, D_MODEL), ys.reshape(nbatch, t_new, D_MODEL),
            p_ckv[-1].reshape(depth, 1, seq, KV_LORA), st('p_kr', (1, seq, ROPE_DIM)),
            p_kv[-1][0].reshape(depth, 1, seq, N_HEADS, HEAD_DV),
            p_kv[-1][1].reshape(depth, 1, seq, N_HEADS, HEAD_DV),
            st('p_lf', (1, seq, N_HEADS)),
            st('s_ckv', (nbatch, t_new, KV_LORA)), st('s_kr', (nbatch, t_new, ROPE_DIM)),
            st('s_fk', (nbatch, t_new, N_HEADS, HEAD_DV)), st('s_fv', (nbatch, t_new, N_HEADS, HEAD_DV)),
            st('s_lf', (nbatch, t_new, N_HEADS)))
```

```python
import functools
import math

import numpy as np
import jax
import jax.numpy as jnp
from jax import lax
from jax.experimental import pallas as pl
from jax.experimental.pallas import tpu as pltpu

D_MODEL = 1024
N_HEADS = 8
NOPE_DIM = 64
ROPE_DIM = 32
HALF_ROPE = ROPE_DIM // 2
HEAD_DV = 64
Q_LORA = 384
KV_LORA = 256
HEADS_W = N_HEADS * HEAD_DV
CHUNK = 64
ROPE_THETA = 10000.0
LOG2E = math.log2(math.e)
MLA_QSCALE = LOG2E / math.sqrt(NOPE_DIM + ROPE_DIM)
FOX_QSCALE = LOG2E / math.sqrt(HEAD_DV)
EPS = 1e-6
NEG = -1e30
IN_SIZES = (Q_LORA, KV_LORA, ROPE_DIM, HEADS_W, HEADS_W, HEADS_W, HEADS_W, N_HEADS, HEADS_W, D_MODEL, D_MODEL)

LANES = 128
SEQ_BLOCK = 512
KEY_SUB = 256
DEN_ROWS = 16
HEADS_PER_STEP = 8
LOOKAHEAD = 2
SAFE_LOG2 = 100.0
SKIP_LOG2 = 64.0
NORM_SLACK = 1.01
AUG_PER_HEAD = 8
VMEM_LIMIT_BYTES = 56 * 1024 * 1024

F32 = jnp.float32
BF16 = jnp.bfloat16


def _dot(a, b):
    return jnp.dot(a, b, preferred_element_type=F32)


def _dot_nt(a, b):
    return lax.dot_general(a, b, (((1,), (1,)), ((), ())), preferred_element_type=F32)


def _rms(x, g):
    return x * lax.rsqrt(jnp.mean(x * x, axis=-1, keepdims=True) + EPS) * g


def _sigmoid(x):
    return 1.0 / (1.0 + jnp.exp(-x))


def _silu(x):
    return x * _sigmoid(x)


def _split3(x):
    hi = x.astype(BF16)
    r1 = x - hi.astype(F32)
    mid = r1.astype(BF16)
    lo = (r1 - mid.astype(F32)).astype(BF16)
    return hi, mid, lo


def _full_spec(shape):
    nd = len(shape)
    return pl.BlockSpec(shape, lambda *_: (0,) * nd)


def _layer_spec(arr, layer):
    nd = arr.ndim
    return pl.BlockSpec((1,) + arr.shape[1:], lambda *_: (layer,) + (0,) * (nd - 1))


def _put(ref, val):
    ref[...] = val.reshape(ref.shape)


def _skip_refs(body, n_in, n_skip, *refs):
    return body(*refs[:n_in], *refs[n_in + n_skip:])


def _stacked(stack, m, tm, cols):
    if stack is None:
        return jax.ShapeDtypeStruct((m, cols), F32), pl.BlockSpec((tm, cols), lambda i: (i, 0))
    layer, depth, _ = stack
    return (jax.ShapeDtypeStruct((depth, m, cols), F32),
            pl.BlockSpec((1, tm, cols), lambda i: (layer, i, 0)))


def _alias_args(stack, n_in, out_indices):
    if stack is None or stack[2] is None:
        return [], [], {}
    prev = list(stack[2])
    specs = [pl.BlockSpec(memory_space=pl.ANY) for _ in prev]
    return prev, specs, {n_in + k: out for k, out in enumerate(out_indices)}


def _params(n_axes):
    return pltpu.CompilerParams(dimension_semantics=("arbitrary",) * n_axes,
                                vmem_limit_bytes=VMEM_LIMIT_BYTES)


MLA_IN_COLS = Q_LORA + KV_LORA + 2 * LANES + HEADS_W + D_MODEL


def _proj_mla_kernel(x_ref, g_ref, w_ref, gq_ref, gkv_ref, wuq_ref, wukn_ref, wuvt_ref,
                     cosT_ref, sinT_ref, ctok_ref, stok_ref, grp_ref,
                     ckv_ref, kr_ref, za_ref, ga_ref, qnT_ref, qrT_ref, knb_ref, krb_ref, vT_ref,
                     qn2_ref, kn2_ref):
    tm = x_ref.shape[0]
    hb = _rms(x_ref[...], g_ref[0]).astype(BF16)
    o = 0
    zcq = _dot(hb, w_ref[0, :, o:o + Q_LORA]); o += Q_LORA
    zckv = _dot(hb, w_ref[0, :, o:o + KV_LORA]); o += KV_LORA
    zka = _dot(hb, w_ref[0, :, o:o + LANES]); o += LANES
    zkb = _dot(hb, w_ref[0, :, o:o + LANES]); o += LANES
    za_ref[...] = _silu(_dot(hb, w_ref[0, :, o:o + HEADS_W])).astype(BF16); o += HEADS_W
    ga_ref[...] = _sigmoid(_dot(hb, w_ref[0, :, o:o + D_MODEL])).astype(BF16)

    cqb = _rms(zcq, gq_ref[0]).astype(BF16)
    qT = _dot_nt(wuq_ref[0], cqb)
    qn = qT[0:HEADS_W] * MLA_QSCALE
    qnT_ref[...] = qn.astype(BF16)
    x1 = qT[HEADS_W:HEADS_W + LANES]
    x2 = qT[HEADS_W + LANES:HEADS_W + 2 * LANES]
    c = cosT_ref[...]
    s = sinT_ref[...]
    r1 = (x1 * c - x2 * s) * MLA_QSCALE
    r2 = (x1 * s + x2 * c) * MLA_QSCALE
    n1 = r1.astype(BF16)
    n2 = r2.astype(BF16)
    for h in range(N_HEADS):
        qrT_ref[h * ROPE_DIM:h * ROPE_DIM + HALF_ROPE, :] = n1[h * HALF_ROPE:(h + 1) * HALF_ROPE]
        qrT_ref[h * ROPE_DIM + HALF_ROPE:(h + 1) * ROPE_DIM, :] = n2[h * HALF_ROPE:(h + 1) * HALF_ROPE]
    qn2 = (jnp.sum((qn * qn).reshape(N_HEADS, NOPE_DIM, tm), axis=1)
           + jnp.sum((r1 * r1 + r2 * r2).reshape(N_HEADS, HALF_ROPE, tm), axis=1))
    qn2_ref[0] = jnp.broadcast_to(jnp.max(qn2, axis=1, keepdims=True), (N_HEADS, LANES))

    ckv = _rms(zckv, gkv_ref[0])
    _put(ckv_ref, ckv)
    cb = ckv.astype(BF16)
    kn = _dot(cb, wukn_ref[0])
    knb_ref[0] = kn.astype(BF16)
    vT_ref[0] = _dot_nt(wuvt_ref[0], cb).astype(BF16)

    kr = zka * ctok_ref[...] + zkb * stok_ref[...]
    kr_ref[...] = kr
    krb_ref[0] = kr.astype(BF16)
    kn2 = _dot((kn * kn).astype(BF16), grp_ref[...]) + jnp.sum(kr * kr, axis=1, keepdims=True)
    kn2_ref[0] = jnp.max(kn2, axis=0, keepdims=True)


def _proj_mla(layer, x, g, w, gq, gkv, wuq, wukn, wuvt, cosT, sinT, ctok, stok, grp, tm, stack=None):
    m = x.shape[0]
    nb = m // tm
    row = lambda c: pl.BlockSpec((tm, c), lambda i: (i, 0))
    colT = lambda r: pl.BlockSpec((r, tm), lambda i: (0, i))
    blk3 = lambda a, b: pl.BlockSpec((1, a, b), lambda i: (i, 0, 0))
    ckv_shape, ckv_spec = _stacked(stack, m, tm, KV_LORA)
    out_shape = (
        ckv_shape,
        jax.ShapeDtypeStruct((m, LANES), F32),
        jax.ShapeDtypeStruct((m, HEADS_W), BF16),
        jax.ShapeDtypeStruct((m, D_MODEL), BF16),
        jax.ShapeDtypeStruct((HEADS_W, m), BF16),
        jax.ShapeDtypeStruct((N_HEADS * ROPE_DIM, m), BF16),
        jax.ShapeDtypeStruct((nb, tm, HEADS_W), BF16),
        jax.ShapeDtypeStruct((nb, tm, LANES), BF16),
        jax.ShapeDtypeStruct((nb, HEADS_W, tm), BF16),
        jax.ShapeDtypeStruct((nb, N_HEADS, LANES), F32),
        jax.ShapeDtypeStruct((nb, 1, LANES), F32),
    )
    out_specs = (ckv_spec, row(LANES), row(HEADS_W), row(D_MODEL), colT(HEADS_W),
                 colT(N_HEADS * ROPE_DIM), blk3(tm, HEADS_W), blk3(tm, LANES), blk3(HEADS_W, tm),
                 blk3(N_HEADS, LANES), blk3(1, LANES))
    in_specs = [row(D_MODEL)] + [_layer_spec(a, layer) for a in (g, w, gq, gkv, wuq, wukn, wuvt)] + [
        colT(LANES), colT(LANES), row(LANES), row(LANES), _full_spec(grp.shape)]
    prev, prev_specs, aliases = _alias_args(stack, len(in_specs), (0,))
    return pl.pallas_call(
        functools.partial(_skip_refs, _proj_mla_kernel, len(in_specs), len(prev)),
        grid=(nb,), in_specs=in_specs + prev_specs, out_specs=out_specs, out_shape=out_shape,
        input_output_aliases=aliases, compiler_params=_params(1), name="proj_mla",
    )(x, g, w, gq, gkv, wuq, wukn, wuvt, cosT, sinT, ctok, stok, grp, *prev)


FOX_IN_COLS = 2 * HEADS_W + LANES + HEADS_W + D_MODEL


def _proj_fox_kernel(x_ref, g_ref, w_ref, wfqt_ref, bf_ref, tri_ref, pk_ref, pqt_ref, grp_ref,
                     fk_ref, fv_ref, logf_ref, zb_ref, gb_ref, fqT_ref, fkb_ref, fvT_ref,
                     augk_ref, augqT_ref, base_ref, qn2_ref, kn2_ref, carry_ref):
    i = pl.program_id(0)
    tm = x_ref.shape[0]
    hb = _rms(x_ref[...], g_ref[0]).astype(BF16)
    o = 0
    fk = _dot(hb, w_ref[0, :, o:o + HEADS_W]); o += HEADS_W
    _put(fk_ref, fk)
    fkb_ref[0] = fk.astype(BF16)
    kn2_ref[0] = jnp.max(_dot((fk * fk).astype(BF16), grp_ref[...]), axis=0, keepdims=True)
    fv = _dot(hb, w_ref[0, :, o:o + HEADS_W]); o += HEADS_W
    _put(fv_ref, fv)
    fvT_ref[0] = fv.T.astype(BF16)
    zf = _dot(hb, w_ref[0, :, o:o + LANES]) + bf_ref[0]; o += LANES
    zb_ref[...] = _silu(_dot(hb, w_ref[0, :, o:o + HEADS_W])).astype(BF16); o += HEADS_W
    gb_ref[...] = _sigmoid(_dot(hb, w_ref[0, :, o:o + D_MODEL])).astype(BF16)
    fqT = _dot_nt(wfqt_ref[0], hb) * FOX_QSCALE
    fqT_ref[...] = fqT.astype(BF16)
    qn2 = jnp.sum((fqT * fqT).reshape(N_HEADS, HEAD_DV, tm), axis=1)
    qn2_ref[0] = jnp.broadcast_to(jnp.max(qn2, axis=1, keepdims=True), (N_HEADS, LANES))

    lane = lax.broadcasted_iota(jnp.int32, (tm, LANES), 1)
    logf = jnp.minimum(zf, 0.0) - jnp.log(1.0 + jnp.exp(-jnp.abs(zf)))
    logf = jnp.where(lane < N_HEADS, logf, 0.0)
    logf_ref[...] = logf[:, 0:N_HEADS]

    hi, mid, lo = _split3(logf)
    packed = (hi.astype(F32) + pltpu.roll(mid.astype(F32), N_HEADS, 1)
              + pltpu.roll(lo.astype(F32), 2 * N_HEADS, 1)).astype(BF16)
    cum3 = _dot(tri_ref[...], packed)
    r = cum3 + pltpu.roll(cum3, LANES - N_HEADS, 1) + pltpu.roll(cum3, LANES - 2 * N_HEADS, 1)
    r = jnp.where(lane < N_HEADS, r, 0.0)

    @pl.when(i == 0)
    def _():
        carry_ref[...] = jnp.zeros_like(carry_ref)

    base_ref[0] = carry_ref[...] * LOG2E
    carry_ref[...] = carry_ref[...] + r[tm - 1:tm, :]

    rh, rm, rl = _split3(r * LOG2E)
    rcat = jnp.concatenate([rh, rm, rl], axis=1)
    slot = lane % AUG_PER_HEAD
    ones_k = jnp.where((lane < N_HEADS * AUG_PER_HEAD) & (slot < 3), 1.0, 0.0)
    augk_ref[0] = (_dot(rcat, pk_ref[...]) + ones_k).astype(BF16)
    rowq = lax.broadcasted_iota(jnp.int32, (N_HEADS * AUG_PER_HEAD, tm), 0) % AUG_PER_HEAD
    ones_q = jnp.where((rowq >= 3) & (rowq < 6), 1.0, 0.0)
    augqT_ref[...] = (_dot_nt(pqt_ref[...], rcat) + ones_q).astype(BF16)


def _proj_fox(layer, x, g, w, wfqt, bf, tri, pk, pqt, grp, tm, stack=None):
    m = x.shape[0]
    nb = m // tm
    row = lambda c: pl.BlockSpec((tm, c), lambda i: (i, 0))
    colT = lambda r: pl.BlockSpec((r, tm), lambda i: (0, i))
    blk3 = lambda a, b: pl.BlockSpec((1, a, b), lambda i: (i, 0, 0))
    kv_shape, kv_spec = _stacked(stack, m, tm, HEADS_W)
    out_shape = (
        kv_shape,
        kv_shape,
        jax.ShapeDtypeStruct((m, N_HEADS), F32),
        jax.ShapeDtypeStruct((m, HEADS_W), BF16),
        jax.ShapeDtypeStruct((m, D_MODEL), BF16),
        jax.ShapeDtypeStruct((HEADS_W, m), BF16),
        jax.ShapeDtypeStruct((nb, tm, HEADS_W), BF16),
        jax.ShapeDtypeStruct((nb, HEADS_W, tm), BF16),
        jax.ShapeDtypeStruct((nb, tm, LANES), BF16),
        jax.ShapeDtypeStruct((N_HEADS * AUG_PER_HEAD, m), BF16),
        jax.ShapeDtypeStruct((nb, 1, LANES), F32),
        jax.ShapeDtypeStruct((nb, N_HEADS, LANES), F32),
        jax.ShapeDtypeStruct((nb, 1, LANES), F32),
    )
    out_specs = (kv_spec, kv_spec, row(N_HEADS), row(HEADS_W), row(D_MODEL), colT(HEADS_W),
                 blk3(tm, HEADS_W), blk3(HEADS_W, tm), blk3(tm, LANES), colT(N_HEADS * AUG_PER_HEAD),
                 blk3(1, LANES), blk3(N_HEADS, LANES), blk3(1, LANES))
    in_specs = ([row(D_MODEL)] + [_layer_spec(a, layer) for a in (g, w, wfqt, bf)]
                + [_full_spec(a.shape) for a in (tri, pk, pqt, grp)])
    prev, prev_specs, aliases = _alias_args(stack, len(in_specs), (0, 1))
    return pl.pallas_call(
        functools.partial(_skip_refs, _proj_fox_kernel, len(in_specs), len(prev)),
        grid=(nb,), in_specs=in_specs + prev_specs, out_specs=out_specs, out_shape=out_shape,
        scratch_shapes=[pltpu.VMEM((1, LANES), F32)],
        input_output_aliases=aliases, compiler_params=_params(1), name="proj_fox",
    )(x, g, w, wfqt, bf, tri, pk, pqt, grp, *prev)


def _attn_kernel(tab_ref, qp_ref, qe_ref, ka_ref, ke_ref, vT_ref, o_ref, acc_ref, s_ref, *, fox):
    g = pl.program_id(0)
    i = pl.program_id(1)
    tq = qp_ref.shape[1]
    tk = ka_ref.shape[1]
    rows_p = lax.broadcasted_iota(jnp.int32, (LANES, tq), 0)
    rows_e = lax.broadcasted_iota(jnp.int32, qe_ref.shape, 0)
    kpos = lax.broadcasted_iota(jnp.int32, (tk, tq), 0)
    qpos = lax.broadcasted_iota(jnp.int32, (tk, tq), 1)
    if fox:
        visible = kpos <= qpos
    else:
        visible = (kpos // CHUNK) <= (qpos // CHUNK)
    qe = qe_ref[...]

    ws = []
    for hq in range(HEADS_PER_STEP):
        pair, hh = divmod(hq, 2)
        qp = qp_ref[pair * LANES:(pair + 1) * LANES, :]
        keep_p = (rows_p >= hh * HEAD_DV) & (rows_p < (hh + 1) * HEAD_DV)
        if fox:
            h = HEADS_PER_STEP * g + hq
            keep_e = (rows_e >= h * AUG_PER_HEAD) & (rows_e < (h + 1) * AUG_PER_HEAD)
            extra = jnp.where(keep_e, qe, jnp.zeros_like(qe))
        else:
            extra = qe[hq * ROPE_DIM:(hq + 1) * ROPE_DIM]
        zero_rows = jnp.zeros((LANES - extra.shape[0], tq), BF16)
        ws.append(jnp.concatenate([jnp.where(keep_p, qp, jnp.zeros_like(qp)), extra, zero_rows], axis=0))

    steps = [(u, hq) for u in range(tk // KEY_SUB) for hq in range(HEADS_PER_STEP)]
    ones_rows = jnp.ones((DEN_ROWS, KEY_SUB), BF16)

    def scores(j, step):
        u, hq = step
        pair = hq // 2
        rows = slice(u * KEY_SUB, (u + 1) * KEY_SUB)
        lhs = jnp.concatenate([ka_ref[j, rows, pair * LANES:(pair + 1) * LANES], ke_ref[j, rows, :]], axis=1)
        return _dot(lhs, ws[hq])

    def block(j, stats, masked, j_next):
        stats = list(stats)
        n_steps = len(steps)
        tiles = {}
        for n, (u, hq) in enumerate(steps):
            ahead = n + LOOKAHEAD
            if ahead < n_steps:
                tiles[ahead] = scores(j, steps[ahead])
            elif j_next is not None:
                tiles[ahead] = scores(j_next, steps[ahead - n_steps])
            sT = tiles.pop(n) if n in tiles else s_ref[n]
            rows = slice(u * KEY_SUB, (u + 1) * KEY_SUB)
            m_old = stats[hq]
            if masked:
                sT = jnp.where(visible[rows], sT, NEG)
            if fox:
                h = HEADS_PER_STEP * g + hq
                d = tab_ref[0, h, i] - tab_ref[0, h, j]
            else:
                d = 0.0
            m_new = jnp.maximum(m_old, jnp.max(sT, axis=0, keepdims=True) + d)
            alpha = jnp.exp2(m_old - m_new)
            pT = jnp.exp2((sT - (m_new - d)).astype(BF16))
            stats[hq] = m_new
            v = jnp.concatenate([vT_ref[j, hq * HEAD_DV:(hq + 1) * HEAD_DV, rows], ones_rows], axis=0)
            acc_ref[hq] = alpha * acc_ref[hq] + _dot(v, pT)
        for n, tile in tiles.items():
            s_ref[n - n_steps] = tile
        return tuple(stats)

    def more(j_top, stats):
        alive = j_top >= 0
        if fox:
            jt = jnp.maximum(j_top, 0)
            slack = None
            for hq in range(HEADS_PER_STEP):
                h = HEADS_PER_STEP * g + hq
                bound = tab_ref[1, h, i] * tab_ref[2, h, jt] + (tab_ref[0, h, i] - tab_ref[0, h, jt + 1])
                room = stats[hq] - bound
                slack = room if slack is None else jnp.minimum(slack, room)
            alive = alive & (jnp.min(slack) < SKIP_LOG2)
        return alive.astype(jnp.int32)

    def block_unshifted(j, masked, j_next):
        n_steps = len(steps)
        tiles = {}
        for n, (u, hq) in enumerate(steps):
            ahead = n + LOOKAHEAD
            if ahead < n_steps:
                tiles[ahead] = scores(j, steps[ahead])
            else:
                tiles[ahead] = scores(j_next, steps[ahead - n_steps])
            sT = tiles.pop(n) if n in tiles else s_ref[n]
            rows = slice(u * KEY_SUB, (u + 1) * KEY_SUB)
            if masked:
                sT = jnp.where(visible[rows], sT, NEG)
            pT = jnp.exp2(sT).astype(BF16)
            v = jnp.concatenate([vT_ref[j, hq * HEAD_DV:(hq + 1) * HEAD_DV, rows], ones_rows], axis=0)
            acc_ref[hq] = acc_ref[hq] + _dot(v, pT)
        for n, tile in tiles.items():
            s_ref[n - n_steps] = tile

    def walk_online():
        m0 = jnp.full((1, tq), NEG, F32)
        stats = block(i, (m0,) * HEADS_PER_STEP, True, jnp.maximum(i - 1, 0))

        def visit(c):
            j = c[0]
            st = block(j, c[2:], False, jnp.maximum(j - 1, 0))
            return (j - 1, more(j - 1, st)) + tuple(st)

        lax.while_loop(lambda c: c[1] > 0, visit, (i - 1, more(i - 1, stats)) + tuple(stats))

    def walk_unshifted():
        block_unshifted(i, True, jnp.maximum(i - 1, 0))

        def visit(k, c):
            j = i - 1 - k
            block_unshifted(j, False, jnp.maximum(j - 1, 0))
            return c

        lax.fori_loop(0, i, visit, 0)

    acc_ref[...] = jnp.zeros_like(acc_ref)
    for n in range(LOOKAHEAD):
        s_ref[n] = scores(i, steps[n])
    if fox:
        walk_online()
    else:
        bounded = None
        for hq in range(HEADS_PER_STEP):
            h = HEADS_PER_STEP * g + hq
            ok = tab_ref[1, h, i] * tab_ref[2, h, i] <= SAFE_LOG2
            bounded = ok if bounded is None else bounded & ok
        pl.when(bounded)(walk_unshifted)
        pl.when(jnp.logical_not(bounded))(walk_online)
    outs = [acc_ref[hq, 0:HEAD_DV, :] / acc_ref[hq, HEAD_DV:HEAD_DV + 1, :] for hq in range(HEADS_PER_STEP)]
    o_ref[...] = jnp.concatenate(outs, axis=0).T.astype(BF16)


def _attention(tab, qpT, qeT, ka, ke, vT, fox):
    nb, tk, _ = ka.shape
    s = qpT.shape[1]
    tq = tk
    width = HEADS_PER_STEP * HEAD_DV
    e_rows = qeT.shape[0] if fox else HEADS_PER_STEP * ROPE_DIM
    e_map = (lambda g, i, b: (0, i)) if fox else (lambda g, i, b: (g, i))
    once = dict(pipeline_mode=pl.Buffered(1))
    grid_spec = pltpu.PrefetchScalarGridSpec(
        num_scalar_prefetch=1,
        grid=(N_HEADS // HEADS_PER_STEP, nb),
        in_specs=[
            pl.BlockSpec((width, tq), lambda g, i, b: (g, i)),
            pl.BlockSpec((e_rows, tq), e_map),
            pl.BlockSpec((nb, tk, width), lambda g, i, b: (0, 0, g), **once),
            pl.BlockSpec((nb, tk, LANES), lambda g, i, b: (0, 0, 0), **once),
            pl.BlockSpec((nb, width, tk), lambda g, i, b: (0, g, 0), **once),
        ],
        out_specs=pl.BlockSpec((tq, width), lambda g, i, b: (i, g)),
        scratch_shapes=[pltpu.VMEM((HEADS_PER_STEP, HEAD_DV + DEN_ROWS, tq), F32),
                        pltpu.VMEM((LOOKAHEAD, KEY_SUB, tq), F32)],
    )
    return pl.pallas_call(
        functools.partial(_attn_kernel, fox=fox), grid_spec=grid_spec,
        out_shape=jax.ShapeDtypeStruct((s, HEADS_W), BF16),
        compiler_params=_params(2), name="attn_fox" if fox else "attn_mla",
    )(tab, qpT, qeT, ka, ke, vT)


def _softmax_pv(s, v):
    m = jnp.max(s, axis=-1, keepdims=True)
    p = jnp.exp2(s - m)
    l = jnp.sum(p, axis=-1, keepdims=True)
    return _dot(p.astype(BF16), v) / l


def _diag_heads(o_big, t_new):
    rows = lax.broadcasted_iota(jnp.int32, o_big.shape, 0) // t_new
    cols = lax.broadcasted_iota(jnp.int32, o_big.shape, 1) // HEAD_DV
    kept = jnp.where(rows == cols, o_big, 0.0)
    return jnp.sum(kept.reshape(N_HEADS, t_new, o_big.shape[1]), axis=0)


def _sample_attn_kernel(qn_ref, qr_ref, qf_ref, lrep_ref, cckv_ref, nckv_ref, ckr_ref, nkr_ref,
                        cfk_ref, nfk_ref, cfv_ref, nfv_ref, wukn_ref, wuv_ref, oa_ref, ob_ref):
    past = cckv_ref.shape[2]
    t_new = nckv_ref.shape[0]
    rows = N_HEADS * t_new
    pad = lrep_ref.shape[2] - past - t_new
    keys = past + t_new + pad

    def with_new(cache, new):
        return jnp.concatenate([cache, new, jnp.zeros((pad, new.shape[1]), new.dtype)], axis=0)

    kidx = lax.broadcasted_iota(jnp.int32, (rows, keys), 1)
    qpos = past + lax.broadcasted_iota(jnp.int32, (rows, keys), 0) % t_new
    real = kidx < past + t_new

    ckv_all = with_new(cckv_ref[0, 0], nckv_ref[...]).astype(BF16)
    kn = _dot(ckv_all, wukn_ref[0]).astype(BF16)
    va = _dot(ckv_all, wuv_ref[0]).astype(BF16)
    kr = with_new(ckr_ref[0, 0], nkr_ref[:, 0:ROPE_DIM]).astype(BF16)
    s = _dot_nt(qn_ref[0], kn) + _dot_nt(qr_ref[0], kr)
    s = jnp.where(real & ((kidx // CHUNK) <= (qpos // CHUNK)), s, NEG)
    oa_ref[...] = _diag_heads(_softmax_pv(s, va), t_new).astype(BF16)

    fk = with_new(cfk_ref[0, 0], nfk_ref[...]).astype(BF16)
    fv = with_new(cfv_ref[0, 0], nfv_ref[...]).astype(BF16)
    c = lrep_ref[0]
    shift = 1
    while shift < keys:
        c = c + jnp.where(kidx >= shift, pltpu.roll(c, shift, 1), 0.0)
        shift *= 2
    cq = jnp.sum(jnp.where(kidx == qpos, c, 0.0), axis=-1, keepdims=True)
    s = _dot_nt(qf_ref[0], fk) + (cq - c) * LOG2E
    s = jnp.where(real & (kidx <= qpos), s, NEG)
    ob_ref[...] = _diag_heads(_softmax_pv(s, fv), t_new).astype(BF16)


def _sample_attention(layer, qn_bd, qr, qf_bd, lrep, cckv, nckv, ckr, nkr, cfk, nfk, cfv, nfv, wukn, wuv):
    nbatch = cckv.shape[1]
    t_new = nckv.shape[0] // nbatch
    b3 = lambda a: pl.BlockSpec((1,) + a.shape[1:], lambda b: (b, 0, 0))
    cache = lambda a: pl.BlockSpec((1, 1) + a.shape[2:], lambda b: (layer, b, 0, 0))
    new = lambda a: pl.BlockSpec((t_new, a.shape[1]), lambda b: (b, 0))
    in_specs = [b3(qn_bd), b3(qr), b3(qf_bd), b3(lrep), cache(cckv), new(nckv), cache(ckr), new(nkr),
                cache(cfk), new(nfk), cache(cfv), new(nfv), _layer_spec(wukn, layer), _layer_spec(wuv, layer)]
    out = jax.ShapeDtypeStruct((nbatch * t_new, HEADS_W), BF16)
    o_spec = pl.BlockSpec((t_new, HEADS_W), lambda b: (b, 0))
    return pl.pallas_call(
        _sample_attn_kernel, grid=(nbatch,), in_specs=in_specs, out_specs=(o_spec, o_spec),
        out_shape=(out, out), compiler_params=_params(1), name="attn_sample",
    )(qn_bd, qr, qf_bd, lrep, cckv, nckv, ckr, nkr, cfk, nfk, cfv, nfv, wukn, wuv)


def _merge_kernel(x_ref, oa_ref, za_ref, ob_ref, zb_ref, ga_ref, gb_ref, woa_ref, wob_ref, wout_ref,
                  fg_ref, xn_ref, *maybe_y_ref):
    a = _dot(oa_ref[...] * za_ref[...], woa_ref[0])
    b = _dot(ob_ref[...] * zb_ref[...], wob_ref[0])
    mix = ga_ref[...].astype(F32) * a + gb_ref[...].astype(F32) * b
    xn = x_ref[...] + _dot(mix.astype(BF16), wout_ref[0])
    xn_ref[...] = xn
    if maybe_y_ref:
        maybe_y_ref[0][...] = _rms(xn, fg_ref[...])


def _merge(layer, x, oa, za, ob, zb, ga, gb, woa, wob, wout, fg, tm, final):
    m = x.shape[0]
    row = lambda c: pl.BlockSpec((tm, c), lambda i: (i, 0))
    in_specs = [row(D_MODEL), row(HEADS_W), row(HEADS_W), row(HEADS_W), row(HEADS_W), row(D_MODEL),
                row(D_MODEL), _layer_spec(woa, layer), _layer_spec(wob, layer), _layer_spec(wout, layer),
                _full_spec(fg.shape)]
    n_out = 2 if final else 1
    out_shape = tuple(jax.ShapeDtypeStruct((m, D_MODEL), F32) for _ in range(n_out))
    out_specs = tuple(row(D_MODEL) for _ in range(n_out))
    return pl.pallas_call(
        _merge_kernel, grid=(m // tm,), in_specs=in_specs, out_specs=out_specs, out_shape=out_shape,
        compiler_params=_params(1), name="merge",
    )(x, oa, za, ob, zb, ga, gb, woa, wob, wout, fg)


def _pad_cols(w, width):
    return jnp.pad(w, ((0, 0), (0, width - w.shape[1])))


def _pad_last(w, width):
    return jnp.pad(w, [(0, 0)] * (w.ndim - 1) + [(0, width - w.shape[-1])])


def _stacked_weights(w_in, w_uq, w_ukv, w_oa, w_ob, w_out, b_f, norm_g, g_q, g_kv):
    depth = w_in.shape[0]
    offs = np.cumsum(IN_SIZES)[:-1].tolist()
    w_cq, w_ckv, w_kr, w_za, w_fq, w_fk, w_fv, w_zf, w_zb, w_ga, w_gb = jnp.split(w_in, offs, axis=2)
    w_kr_swapped = jnp.concatenate([w_kr[..., HALF_ROPE:], w_kr[..., :HALF_ROPE]], axis=2)
    w_mla = jnp.concatenate([w_cq, w_ckv, _pad_last(w_kr, LANES), _pad_last(w_kr_swapped, LANES),
                             w_za, w_ga], axis=2).astype(BF16)
    w_fox = jnp.concatenate([w_fk, w_fv, _pad_last(w_zf, LANES), w_zb, w_gb], axis=2).astype(BF16)
    uq = w_uq.reshape(depth, Q_LORA, N_HEADS, NOPE_DIM + ROPE_DIM)
    uq_rows = jnp.concatenate([
        uq[..., :NOPE_DIM].reshape(depth, Q_LORA, -1),
        uq[..., NOPE_DIM:NOPE_DIM + HALF_ROPE].reshape(depth, Q_LORA, -1),
        uq[..., NOPE_DIM + HALF_ROPE:].reshape(depth, Q_LORA, -1)], axis=2)
    ukv = w_ukv.reshape(depth, KV_LORA, N_HEADS, NOPE_DIM + HEAD_DV)
    w_ukn = ukv[..., :NOPE_DIM].reshape(depth, KV_LORA, -1).astype(BF16)
    w_uv = ukv[..., NOPE_DIM:].reshape(depth, KV_LORA, -1).astype(BF16)
    tr = lambda a: jnp.swapaxes(a, 1, 2)
    return dict(
        w_mla=w_mla, w_fox=w_fox, w_uqT=tr(uq_rows).astype(BF16), w_ukn=w_ukn, w_uv=w_uv, w_uvT=tr(w_uv),
        w_fqT=tr(w_fq).astype(BF16),
        b_f=_pad_last(b_f[:, None, :], LANES),
        w_oa=w_oa.astype(BF16), w_ob=w_ob.astype(BF16), w_out=w_out.astype(BF16),
        norm_g=norm_g[:, None, :], g_q=g_q[:, None, :], g_kv=g_kv[:, None, :])


def _rope_tables(pos):
    inv = jnp.exp(-math.log(ROPE_THETA) * jnp.arange(HALF_ROPE, dtype=F32) / HALF_ROPE)
    ang = pos.astype(F32)[:, None] * inv[None, :]
    cos = jnp.cos(ang)
    sin = jnp.sin(ang)
    cosT = jnp.tile(cos.T, (N_HEADS, 1))
    sinT = jnp.tile(sin.T, (N_HEADS, 1))
    ctok = _pad_cols(jnp.concatenate([cos, cos], axis=1), LANES)
    stok = _pad_cols(jnp.concatenate([-sin, sin], axis=1), LANES)
    return cosT, sinT, ctok, stok


def _bias_placement():
    pk = np.zeros((3 * LANES, LANES), np.float32)
    pqt = np.zeros((N_HEADS * AUG_PER_HEAD, 3 * LANES), np.float32)
    for c in range(3):
        for h in range(N_HEADS):
            pk[c * LANES + h, h * AUG_PER_HEAD + 3 + c] = -1.0
            pqt[h * AUG_PER_HEAD + c, c * LANES + h] = 1.0
    return jnp.asarray(pk, BF16), jnp.asarray(pqt, BF16)


def _head_groups():
    grp = np.zeros((HEADS_W, LANES), np.float32)
    for h in range(N_HEADS):
        grp[h * HEAD_DV:(h + 1) * HEAD_DV, h] = 1.0
    return jnp.asarray(grp, BF16)


def _skip_tables(base, qn2, kn2):
    b = base[:, 0, :N_HEADS]
    qmax = jnp.sqrt(qn2[:, :, 0]) * NORM_SLACK
    kmax = lax.cummax(jnp.sqrt(kn2[:, 0, :N_HEADS]) * NORM_SLACK, axis=0)
    return jnp.stack([b, qmax, kmax]).transpose(0, 2, 1)


def _tri(n):
    return jnp.asarray(np.tril(np.ones((n, n), np.float32)), BF16)


def _block_diag_queries(qT, nbatch, t_new):
    q = qT.T.reshape(nbatch, t_new, N_HEADS, HEAD_DV)
    eye = jnp.eye(N_HEADS, dtype=q.dtype)
    return jnp.einsum('bthj,hg->bhtgj', q, eye).reshape(nbatch, N_HEADS * t_new, HEADS_W)


N_MLA_OUT = 11
N_FOX_OUT = 13


def _proj_both_kernel(n_mla_in, n_fox_in, n_prev, *refs):
    x_ref, g_ref = refs[0], refs[1]
    mla_in = refs[2:2 + n_mla_in]
    grp_ref = refs[2 + n_mla_in]
    fox_in = refs[3 + n_mla_in:3 + n_mla_in + n_fox_in]
    outs = refs[3 + n_mla_in + n_fox_in + n_prev:]
    _proj_mla_kernel(x_ref, g_ref, *mla_in, grp_ref, *outs[:N_MLA_OUT])
    _proj_fox_kernel(x_ref, g_ref, *fox_in, grp_ref, *outs[N_MLA_OUT:N_MLA_OUT + N_FOX_OUT],
                     outs[N_MLA_OUT + N_FOX_OUT])


def _project(layer, x, sw, tables, tri, pk, pqt, grp, tm, stack_mla=None, stack_fox=None):
    cosT, sinT, ctok, stok = tables
    m = x.shape[0]
    nb = m // tm
    row = lambda c: pl.BlockSpec((tm, c), lambda i: (i, 0))
    colT = lambda r: pl.BlockSpec((r, tm), lambda i: (0, i))
    blk3 = lambda a, b: pl.BlockSpec((1, a, b), lambda i: (i, 0, 0))
    sds = jax.ShapeDtypeStruct
    once = lambda a: pl.BlockSpec((1,) + a.shape[1:], lambda i: (layer, 0, 0), pipeline_mode=pl.Buffered(1))
    mla_in = (sw['w_mla'], sw['g_q'], sw['g_kv'], sw['w_uqT'], sw['w_ukn'], sw['w_uvT'], cosT, sinT, ctok, stok)
    mla_specs = [once(sw['w_mla'])] + [_layer_spec(a, layer) for a in mla_in[1:6]] + [
        colT(LANES), colT(LANES), row(LANES), row(LANES)]
    fox_in = (sw['w_fox'], sw['w_fqT'], sw['b_f'], tri, pk, pqt)
    fox_specs = [once(sw['w_fox'])] + [_layer_spec(a, layer) for a in fox_in[1:3]] + [
        _full_spec(a.shape) for a in fox_in[3:]]
    ckv_shape, ckv_spec = _stacked(stack_mla, m, tm, KV_LORA)
    kv_shape, kv_spec = _stacked(stack_fox, m, tm, HEADS_W)
    out_shape = (
        ckv_shape, sds((m, LANES), F32), sds((m, HEADS_W), BF16), sds((m, D_MODEL), BF16),
        sds((HEADS_W, m), BF16), sds((N_HEADS * ROPE_DIM, m), BF16), sds((nb, tm, HEADS_W), BF16),
        sds((nb, tm, LANES), BF16), sds((nb, HEADS_W, tm), BF16), sds((nb, N_HEADS, LANES), F32),
        sds((nb, 1, LANES), F32),
        kv_shape, kv_shape, sds((m, N_HEADS), F32), sds((m, HEADS_W), BF16), sds((m, D_MODEL), BF16),
        sds((HEADS_W, m), BF16), sds((nb, tm, HEADS_W), BF16), sds((nb, HEADS_W, tm), BF16),
        sds((nb, tm, LANES), BF16), sds((N_HEADS * AUG_PER_HEAD, m), BF16), sds((nb, 1, LANES), F32),
        sds((nb, N_HEADS, LANES), F32), sds((nb, 1, LANES), F32))
    out_specs = (
        ckv_spec, row(LANES), row(HEADS_W), row(D_MODEL), colT(HEADS_W), colT(N_HEADS * ROPE_DIM),
        blk3(tm, HEADS_W), blk3(tm, LANES), blk3(HEADS_W, tm), blk3(N_HEADS, LANES), blk3(1, LANES),
        kv_spec, kv_spec, row(N_HEADS), row(HEADS_W), row(D_MODEL), colT(HEADS_W),
        blk3(tm, HEADS_W), blk3(HEADS_W, tm), blk3(tm, LANES), colT(N_HEADS * AUG_PER_HEAD),
        blk3(1, LANES), blk3(N_HEADS, LANES), blk3(1, LANES))
    in_specs = ([row(D_MODEL), _layer_spec(sw['norm_g'], layer)] + mla_specs + [_full_spec(grp.shape)]
                + fox_specs)
    prev_mla, specs_mla, alias_mla = _alias_args(stack_mla, len(in_specs), (0,))
    prev_fox, specs_fox, alias_fox = _alias_args(stack_fox, len(in_specs) + len(prev_mla),
                                                 (N_MLA_OUT, N_MLA_OUT + 1))
    prev = prev_mla + prev_fox
    res = pl.pallas_call(
        functools.partial(_proj_both_kernel, len(mla_in), len(fox_in), len(prev)),
        grid=(nb,), in_specs=in_specs + specs_mla + specs_fox, out_specs=out_specs, out_shape=out_shape,
        scratch_shapes=[pltpu.VMEM((1, LANES), F32)],
        input_output_aliases={**alias_mla, **alias_fox}, compiler_params=_params(1), name="proj_both",
    )(x, sw['norm_g'], *mla_in, grp, *fox_in, *prev)
    return res[:N_MLA_OUT], res[N_MLA_OUT:]


def kernel(x_prompt, x_sample, cache_mla_ckv, cache_mla_krope, cache_fox_k, cache_fox_v, cache_fox_logf,
           norm_g, w_in, g_q, w_uq, g_kv, w_ukv, b_f, w_oa, w_ob, w_out, final_g):
    depth = w_in.shape[0]
    _, seq, _ = x_prompt.shape
    nbatch, t_new, _ = x_sample.shape
    past = cache_mla_ckv.shape[2]
    m_s = nbatch * t_new
    key_pad = -(past + t_new) % LANES

    tabs_p = _rope_tables(jnp.arange(seq, dtype=jnp.int32))
    tabs_s = _rope_tables(past + jnp.arange(m_s, dtype=jnp.int32) % t_new)
    pk, pqt = _bias_placement()
    grp = _head_groups()
    tri_p = _tri(SEQ_BLOCK)
    tri_s = _tri(m_s)
    fg = final_g[None, :]

    xp = x_prompt.reshape(seq, D_MODEL)
    xs = x_sample.reshape(m_s, D_MODEL)
    outs = {k: [] for k in ('p_kr', 'p_lf', 's_ckv', 's_kr', 's_fk', 's_fv', 's_lf')}
    p_ckv_stack = p_kv_stack = None
    yp = ys = None
    sw = _stacked_weights(w_in, w_uq, w_ukv, w_oa, w_ob, w_out, b_f, norm_g, g_q, g_kv)
    for l in range(depth):
        final = l == depth - 1

        mla, fox = _project(l, xp, sw, tabs_p, tri_p, pk, pqt, grp, SEQ_BLOCK,
                            (l, depth, p_ckv_stack), (l, depth, p_kv_stack))
        ckv, kr, za, ga, qnT, qrT, knb, krb, vT, qn2_a, kn2_a = mla
        fk, fv, logf, zb, gb, fqT, fkb, fvT, augk, augqT, base, qn2, kn2 = fox
        tab = _skip_tables(base, qn2, kn2)
        tab_a = _skip_tables(jnp.zeros_like(base), qn2_a, kn2_a)
        o_a = _attention(tab_a, qnT, qrT, knb, krb, vT, fox=False)
        o_b = _attention(tab, fqT, augqT, fkb, augk, fvT, fox=True)
        res = _merge(l, xp, o_a, za, o_b, zb, ga, gb, sw['w_oa'], sw['w_ob'], sw['w_out'], fg, SEQ_BLOCK, final)
        xp = res[0]
        if final:
            yp = res[1]
        p_ckv_stack, p_kv_stack = (ckv,), (fk, fv)
        outs['p_kr'].append(kr[:, :ROPE_DIM]); outs['p_lf'].append(logf)

        mla, fox = _project(l, xs, sw, tabs_s, tri_s, pk, pqt, grp, m_s)
        ckv, kr, za, ga, qnT, qrT = mla[:6]
        fk, fv, logf, zb, gb, fqT = fox[:6]
        qn_bd = _block_diag_queries(qnT, nbatch, t_new)
        qf_bd = _block_diag_queries(fqT, nbatch, t_new)
        qr = qrT.reshape(N_HEADS, ROPE_DIM, nbatch, t_new).transpose(2, 0, 3, 1).reshape(
            nbatch, N_HEADS * t_new, ROPE_DIM)
        lf_all = jnp.concatenate([cache_fox_logf[l], logf.reshape(nbatch, t_new, N_HEADS),
                                  jnp.zeros((nbatch, key_pad, N_HEADS), F32)], axis=1)
        lrep = jnp.repeat(lf_all.transpose(0, 2, 1), t_new, axis=1)
        o_a, o_b = _sample_attention(
            l, qn_bd, qr, qf_bd, lrep, cache_mla_ckv, ckv, cache_mla_krope, kr,
            cache_fox_k.reshape(depth, nbatch, past, HEADS_W), fk,
            cache_fox_v.reshape(depth, nbatch, past, HEADS_W), fv, sw['w_ukn'], sw['w_uv'])
        res = _merge(l, xs, o_a, za, o_b, zb, ga, gb, sw['w_oa'], sw['w_ob'], sw['w_out'], fg, m_s, final)
        xs = res[0]
        if final:
            ys = res[1]
        outs['s_ckv'].append(ckv); outs['s_kr'].append(kr[:, :ROPE_DIM]); outs['s_fk'].append(fk)
        outs['s_fv'].append(fv); outs['s_lf'].append(logf)

    st = lambda name, shape: jnp.stack(outs[name]).reshape((depth,) + shape)
    return (yp.reshape(1, seq, D_MODEL), ys.reshape(nbatch, t_new, D_MODEL),
            p_ckv_stack[0].reshape(depth, 1, seq, KV_LORA), st('p_kr', (1, seq, ROPE_DIM)),
            p_kv_stack[0].reshape(depth, 1, seq, N_HEADS, HEAD_DV),
            p_kv_stack[1].reshape(depth, 1, seq, N_HEADS, HEAD_DV),
            st('p_lf', (1, seq, N_HEADS)),
            st('s_ckv', (nbatch, t_new, KV_LORA)), st('s_kr', (nbatch, t_new, ROPE_DIM)),
            st('s_fk', (nbatch, t_new, N_HEADS, HEAD_DV)), st('s_fv', (nbatch, t_new, N_HEADS, HEAD_DV)),
            st('s_lf', (nbatch, t_new, N_HEADS)))
```

```python
import functools
import math

import numpy as np
import jax
import jax.numpy as jnp
from jax import lax
from jax.experimental import pallas as pl
from jax.experimental.pallas import tpu as pltpu

D_MODEL = 1024
N_HEADS = 8
NOPE_DIM = 64
ROPE_DIM = 32
HALF_ROPE = ROPE_DIM // 2
HEAD_DV = 64
Q_LORA = 384
KV_LORA = 256
HEADS_W = N_HEADS * HEAD_DV
CHUNK = 64
ROPE_THETA = 10000.0
LOG2E = math.log2(math.e)
MLA_QSCALE = LOG2E / math.sqrt(NOPE_DIM + ROPE_DIM)
FOX_QSCALE = LOG2E / math.sqrt(HEAD_DV)
EPS = 1e-6
NEG = -1e30
IN_SIZES = (Q_LORA, KV_LORA, ROPE_DIM, HEADS_W, HEADS_W, HEADS_W, HEADS_W, N_HEADS, HEADS_W, D_MODEL, D_MODEL)

LANES = 128
SEQ_BLOCK = 512
KEY_SUB = 256
DEN_ROWS = 16
HEADS_PER_STEP = 8
FOX_HEADS_PER_STEP = 4
LOOKAHEAD = 2
SAFE_LOG2 = 100.0
SKIP_LOG2 = 64.0
NORM_SLACK = 1.01
AUG_PER_HEAD = 8
VMEM_LIMIT_BYTES = 56 * 1024 * 1024

F32 = jnp.float32
BF16 = jnp.bfloat16


def _dot(a, b):
    return jnp.dot(a, b, preferred_element_type=F32)


def _dot_nt(a, b):
    return lax.dot_general(a, b, (((1,), (1,)), ((), ())), preferred_element_type=F32)


def _rms(x, g):
    return x * lax.rsqrt(jnp.mean(x * x, axis=-1, keepdims=True) + EPS) * g


def _sigmoid(x):
    return 1.0 / (1.0 + jnp.exp(-x))


def _silu(x):
    return x * _sigmoid(x)


def _split3(x):
    hi = x.astype(BF16)
    r1 = x - hi.astype(F32)
    mid = r1.astype(BF16)
    lo = (r1 - mid.astype(F32)).astype(BF16)
    return hi, mid, lo


def _full_spec(shape):
    nd = len(shape)
    return pl.BlockSpec(shape, lambda *_: (0,) * nd)


def _layer_spec(arr, layer):
    nd = arr.ndim
    return pl.BlockSpec((1,) + arr.shape[1:], lambda *_: (layer,) + (0,) * (nd - 1))


def _put(ref, val):
    ref[...] = val.reshape(ref.shape)


def _skip_refs(body, n_in, n_skip, *refs):
    return body(*refs[:n_in], *refs[n_in + n_skip:])


def _stacked(stack, m, tm, cols):
    if stack is None:
        return jax.ShapeDtypeStruct((m, cols), F32), pl.BlockSpec((tm, cols), lambda i: (i, 0))
    layer, depth, _ = stack
    return (jax.ShapeDtypeStruct((depth, m, cols), F32),
            pl.BlockSpec((1, tm, cols), lambda i: (layer, i, 0)))


def _alias_args(stack, n_in, out_indices):
    if stack is None or stack[2] is None:
        return [], [], {}
    prev = list(stack[2])
    specs = [pl.BlockSpec(memory_space=pl.ANY) for _ in prev]
    return prev, specs, {n_in + k: out for k, out in enumerate(out_indices)}


def _params(n_axes):
    return pltpu.CompilerParams(dimension_semantics=("arbitrary",) * n_axes,
                                vmem_limit_bytes=VMEM_LIMIT_BYTES)


MLA_IN_COLS = Q_LORA + KV_LORA + 2 * LANES + HEADS_W + D_MODEL


def _proj_mla_kernel(x_ref, g_ref, w_ref, gq_ref, gkv_ref, wuq_ref, wukn_ref, wuvt_ref,
                     cosT_ref, sinT_ref, ctok_ref, stok_ref, grp_ref,
                     ckv_ref, kr_ref, za_ref, ga_ref, qnT_ref, qrT_ref, knb_ref, krb_ref, vT_ref,
                     qn2_ref, kn2_ref):
    tm = x_ref.shape[0]
    hb = _rms(x_ref[...], g_ref[0]).astype(BF16)
    o = 0
    zcq = _dot(hb, w_ref[0, :, o:o + Q_LORA]); o += Q_LORA
    zckv = _dot(hb, w_ref[0, :, o:o + KV_LORA]); o += KV_LORA
    zka = _dot(hb, w_ref[0, :, o:o + LANES]); o += LANES
    zkb = _dot(hb, w_ref[0, :, o:o + LANES]); o += LANES
    za_ref[...] = _silu(_dot(hb, w_ref[0, :, o:o + HEADS_W])).astype(BF16); o += HEADS_W
    ga_ref[...] = _sigmoid(_dot(hb, w_ref[0, :, o:o + D_MODEL])).astype(BF16)

    cqb = _rms(zcq, gq_ref[0]).astype(BF16)
    qT = _dot_nt(wuq_ref[0], cqb)
    qn = qT[0:HEADS_W] * MLA_QSCALE
    qnT_ref[...] = qn.astype(BF16)
    x1 = qT[HEADS_W:HEADS_W + LANES]
    x2 = qT[HEADS_W + LANES:HEADS_W + 2 * LANES]
    c = cosT_ref[...]
    s = sinT_ref[...]
    r1 = (x1 * c - x2 * s) * MLA_QSCALE
    r2 = (x1 * s + x2 * c) * MLA_QSCALE
    n1 = r1.astype(BF16)
    n2 = r2.astype(BF16)
    for h in range(N_HEADS):
        qrT_ref[h * ROPE_DIM:h * ROPE_DIM + HALF_ROPE, :] = n1[h * HALF_ROPE:(h + 1) * HALF_ROPE]
        qrT_ref[h * ROPE_DIM + HALF_ROPE:(h + 1) * ROPE_DIM, :] = n2[h * HALF_ROPE:(h + 1) * HALF_ROPE]
    qn2 = (jnp.sum((qn * qn).reshape(N_HEADS, NOPE_DIM, tm), axis=1)
           + jnp.sum((r1 * r1 + r2 * r2).reshape(N_HEADS, HALF_ROPE, tm), axis=1))
    qn2_ref[0] = jnp.broadcast_to(jnp.max(qn2, axis=1, keepdims=True), (N_HEADS, LANES))

    ckv = _rms(zckv, gkv_ref[0])
    _put(ckv_ref, ckv)
    cb = ckv.astype(BF16)
    kn = _dot(cb, wukn_ref[0])
    knb_ref[0] = kn.astype(BF16)
    vT_ref[0] = _dot_nt(wuvt_ref[0], cb).astype(BF16)

    kr = zka * ctok_ref[...] + zkb * stok_ref[...]
    kr_ref[...] = kr
    krb_ref[0] = kr.astype(BF16)
    kn2 = _dot((kn * kn).astype(BF16), grp_ref[...]) + jnp.sum(kr * kr, axis=1, keepdims=True)
    kn2_ref[0] = jnp.max(kn2, axis=0, keepdims=True)


def _proj_mla(layer, x, g, w, gq, gkv, wuq, wukn, wuvt, cosT, sinT, ctok, stok, grp, tm, stack=None):
    m = x.shape[0]
    nb = m // tm
    row = lambda c: pl.BlockSpec((tm, c), lambda i: (i, 0))
    colT = lambda r: pl.BlockSpec((r, tm), lambda i: (0, i))
    blk3 = lambda a, b: pl.BlockSpec((1, a, b), lambda i: (i, 0, 0))
    ckv_shape, ckv_spec = _stacked(stack, m, tm, KV_LORA)
    out_shape = (
        ckv_shape,
        jax.ShapeDtypeStruct((m, LANES), F32),
        jax.ShapeDtypeStruct((m, HEADS_W), BF16),
        jax.ShapeDtypeStruct((m, D_MODEL), BF16),
        jax.ShapeDtypeStruct((HEADS_W, m), BF16),
        jax.ShapeDtypeStruct((N_HEADS * ROPE_DIM, m), BF16),
        jax.ShapeDtypeStruct((nb, tm, HEADS_W), BF16),
        jax.ShapeDtypeStruct((nb, tm, LANES), BF16),
        jax.ShapeDtypeStruct((nb, HEADS_W, tm), BF16),
        jax.ShapeDtypeStruct((nb, N_HEADS, LANES), F32),
        jax.ShapeDtypeStruct((nb, 1, LANES), F32),
    )
    out_specs = (ckv_spec, row(LANES), row(HEADS_W), row(D_MODEL), colT(HEADS_W),
                 colT(N_HEADS * ROPE_DIM), blk3(tm, HEADS_W), blk3(tm, LANES), blk3(HEADS_W, tm),
                 blk3(N_HEADS, LANES), blk3(1, LANES))
    in_specs = [row(D_MODEL)] + [_layer_spec(a, layer) for a in (g, w, gq, gkv, wuq, wukn, wuvt)] + [
        colT(LANES), colT(LANES), row(LANES), row(LANES), _full_spec(grp.shape)]
    prev, prev_specs, aliases = _alias_args(stack, len(in_specs), (0,))
    return pl.pallas_call(
        functools.partial(_skip_refs, _proj_mla_kernel, len(in_specs), len(prev)),
        grid=(nb,), in_specs=in_specs + prev_specs, out_specs=out_specs, out_shape=out_shape,
        input_output_aliases=aliases, compiler_params=_params(1), name="proj_mla",
    )(x, g, w, gq, gkv, wuq, wukn, wuvt, cosT, sinT, ctok, stok, grp, *prev)


FOX_IN_COLS = 2 * HEADS_W + LANES + HEADS_W + D_MODEL


def _proj_fox_kernel(x_ref, g_ref, w_ref, wfqt_ref, bf_ref, tri_ref, pk_ref, pqt_ref, grp_ref,
                     fk_ref, fv_ref, logf_ref, zb_ref, gb_ref, fqT_ref, fkb_ref, fvT_ref,
                     augk_ref, augqT_ref, base_ref, qn2_ref, kn2_ref, carry_ref):
    i = pl.program_id(0)
    tm = x_ref.shape[0]
    hb = _rms(x_ref[...], g_ref[0]).astype(BF16)
    o = 0
    fk = _dot(hb, w_ref[0, :, o:o + HEADS_W]); o += HEADS_W
    _put(fk_ref, fk)
    fkb_ref[0] = fk.astype(BF16)
    kn2_ref[0] = jnp.max(_dot((fk * fk).astype(BF16), grp_ref[...]), axis=0, keepdims=True)
    fv = _dot(hb, w_ref[0, :, o:o + HEADS_W]); o += HEADS_W
    _put(fv_ref, fv)
    fvT_ref[0] = fv.T.astype(BF16)
    zf = _dot(hb, w_ref[0, :, o:o + LANES]) + bf_ref[0]; o += LANES
    zb_ref[...] = _silu(_dot(hb, w_ref[0, :, o:o + HEADS_W])).astype(BF16); o += HEADS_W
    gb_ref[...] = _sigmoid(_dot(hb, w_ref[0, :, o:o + D_MODEL])).astype(BF16)
    fqT = _dot_nt(wfqt_ref[0], hb) * FOX_QSCALE
    fqT_ref[...] = fqT.astype(BF16)
    qn2 = jnp.sum((fqT * fqT).reshape(N_HEADS, HEAD_DV, tm), axis=1)
    qn2_ref[0] = jnp.broadcast_to(jnp.max(qn2, axis=1, keepdims=True), (N_HEADS, LANES))

    lane = lax.broadcasted_iota(jnp.int32, (tm, LANES), 1)
    logf = jnp.minimum(zf, 0.0) - jnp.log(1.0 + jnp.exp(-jnp.abs(zf)))
    logf = jnp.where(lane < N_HEADS, logf, 0.0)
    logf_ref[...] = logf[:, 0:N_HEADS]

    hi, mid, lo = _split3(logf)
    packed = (hi.astype(F32) + pltpu.roll(mid.astype(F32), N_HEADS, 1)
              + pltpu.roll(lo.astype(F32), 2 * N_HEADS, 1)).astype(BF16)
    cum3 = _dot(tri_ref[...], packed)
    r = cum3 + pltpu.roll(cum3, LANES - N_HEADS, 1) + pltpu.roll(cum3, LANES - 2 * N_HEADS, 1)
    r = jnp.where(lane < N_HEADS, r, 0.0)

    @pl.when(i == 0)
    def _():
        carry_ref[...] = jnp.zeros_like(carry_ref)

    base_ref[0] = carry_ref[...] * LOG2E
    carry_ref[...] = carry_ref[...] + r[tm - 1:tm, :]

    rh, rm, rl = _split3(r * LOG2E)
    rcat = jnp.concatenate([rh, rm, rl], axis=1)
    slot = lane % AUG_PER_HEAD
    ones_k = jnp.where((lane < N_HEADS * AUG_PER_HEAD) & (slot < 3), 1.0, 0.0)
    augk_ref[0] = (_dot(rcat, pk_ref[...]) + ones_k).astype(BF16)
    rowq = lax.broadcasted_iota(jnp.int32, (N_HEADS * AUG_PER_HEAD, tm), 0) % AUG_PER_HEAD
    ones_q = jnp.where((rowq >= 3) & (rowq < 6), 1.0, 0.0)
    augqT_ref[...] = (_dot_nt(pqt_ref[...], rcat) + ones_q).astype(BF16)


def _proj_fox(layer, x, g, w, wfqt, bf, tri, pk, pqt, grp, tm, stack=None):
    m = x.shape[0]
    nb = m // tm
    row = lambda c: pl.BlockSpec((tm, c), lambda i: (i, 0))
    colT = lambda r: pl.BlockSpec((r, tm), lambda i: (0, i))
    blk3 = lambda a, b: pl.BlockSpec((1, a, b), lambda i: (i, 0, 0))
    kv_shape, kv_spec = _stacked(stack, m, tm, HEADS_W)
    out_shape = (
        kv_shape,
        kv_shape,
        jax.ShapeDtypeStruct((m, N_HEADS), F32),
        jax.ShapeDtypeStruct((m, HEADS_W), BF16),
        jax.ShapeDtypeStruct((m, D_MODEL), BF16),
        jax.ShapeDtypeStruct((HEADS_W, m), BF16),
        jax.ShapeDtypeStruct((nb, tm, HEADS_W), BF16),
        jax.ShapeDtypeStruct((nb, HEADS_W, tm), BF16),
        jax.ShapeDtypeStruct((nb, tm, LANES), BF16),
        jax.ShapeDtypeStruct((N_HEADS * AUG_PER_HEAD, m), BF16),
        jax.ShapeDtypeStruct((nb, 1, LANES), F32),
        jax.ShapeDtypeStruct((nb, N_HEADS, LANES), F32),
        jax.ShapeDtypeStruct((nb, 1, LANES), F32),
    )
    out_specs = (kv_spec, kv_spec, row(N_HEADS), row(HEADS_W), row(D_MODEL), colT(HEADS_W),
                 blk3(tm, HEADS_W), blk3(HEADS_W, tm), blk3(tm, LANES), colT(N_HEADS * AUG_PER_HEAD),
                 blk3(1, LANES), blk3(N_HEADS, LANES), blk3(1, LANES))
    in_specs = ([row(D_MODEL)] + [_layer_spec(a, layer) for a in (g, w, wfqt, bf)]
                + [_full_spec(a.shape) for a in (tri, pk, pqt, grp)])
    prev, prev_specs, aliases = _alias_args(stack, len(in_specs), (0, 1))
    return pl.pallas_call(
        functools.partial(_skip_refs, _proj_fox_kernel, len(in_specs), len(prev)),
        grid=(nb,), in_specs=in_specs + prev_specs, out_specs=out_specs, out_shape=out_shape,
        scratch_shapes=[pltpu.VMEM((1, LANES), F32)],
        input_output_aliases=aliases, compiler_params=_params(1), name="proj_fox",
    )(x, g, w, wfqt, bf, tri, pk, pqt, grp, *prev)


def _attn_kernel(tab_ref, qp_ref, qe_ref, ka_ref, ke_ref, vT_ref, o_ref, acc_ref, s_ref, *, fox):
    g = pl.program_id(0)
    i = pl.program_id(1)
    hps = acc_ref.shape[0]
    tq = qp_ref.shape[1]
    tk = ka_ref.shape[1]
    rows_p = lax.broadcasted_iota(jnp.int32, (LANES, tq), 0)
    rows_e = lax.broadcasted_iota(jnp.int32, qe_ref.shape, 0)
    kpos = lax.broadcasted_iota(jnp.int32, (tk, tq), 0)
    qpos = lax.broadcasted_iota(jnp.int32, (tk, tq), 1)
    if fox:
        visible = kpos <= qpos
    else:
        visible = (kpos // CHUNK) <= (qpos // CHUNK)
    qe = qe_ref[...]

    ws = []
    for hq in range(hps):
        pair, hh = divmod(hq, 2)
        qp = qp_ref[pair * LANES:(pair + 1) * LANES, :]
        keep_p = (rows_p >= hh * HEAD_DV) & (rows_p < (hh + 1) * HEAD_DV)
        if fox:
            h = hps * g + hq
            keep_e = (rows_e >= h * AUG_PER_HEAD) & (rows_e < (h + 1) * AUG_PER_HEAD)
            extra = jnp.where(keep_e, qe, jnp.zeros_like(qe))
        else:
            extra = qe[hq * ROPE_DIM:(hq + 1) * ROPE_DIM]
        zero_rows = jnp.zeros((LANES - extra.shape[0], tq), BF16)
        ws.append(jnp.concatenate([jnp.where(keep_p, qp, jnp.zeros_like(qp)), extra, zero_rows], axis=0))

    steps = [(u, hq) for u in range(tk // KEY_SUB) for hq in range(hps)]
    ones_rows = jnp.ones((DEN_ROWS, KEY_SUB), BF16)

    def scores(j, step):
        u, hq = step
        pair = hq // 2
        rows = slice(u * KEY_SUB, (u + 1) * KEY_SUB)
        lhs = jnp.concatenate([ka_ref[j, rows, pair * LANES:(pair + 1) * LANES], ke_ref[j, rows, :]], axis=1)
        return _dot(lhs, ws[hq])

    def block(j, stats, masked, j_next):
        stats = list(stats)
        n_steps = len(steps)
        tiles = {}
        for n, (u, hq) in enumerate(steps):
            ahead = n + LOOKAHEAD
            if ahead < n_steps:
                tiles[ahead] = scores(j, steps[ahead])
            elif j_next is not None:
                tiles[ahead] = scores(j_next, steps[ahead - n_steps])
            sT = tiles.pop(n) if n in tiles else s_ref[n]
            rows = slice(u * KEY_SUB, (u + 1) * KEY_SUB)
            m_old = stats[hq]
            if masked:
                sT = jnp.where(visible[rows], sT, NEG)
            if fox:
                h = hps * g + hq
                d = tab_ref[0, h, i] - tab_ref[0, h, j]
            else:
                d = 0.0
            m_new = jnp.maximum(m_old, jnp.max(sT, axis=0, keepdims=True) + d)
            alpha = jnp.exp2(m_old - m_new)
            pT = jnp.exp2((sT - (m_new - d)).astype(BF16))
            stats[hq] = m_new
            v = jnp.concatenate([vT_ref[j, hq * HEAD_DV:(hq + 1) * HEAD_DV, rows], ones_rows], axis=0)
            acc_ref[hq] = alpha * acc_ref[hq] + _dot(v, pT)
        for n, tile in tiles.items():
            s_ref[n - n_steps] = tile
        return tuple(stats)

    def more(j_top, stats):
        alive = j_top >= 0
        if fox:
            jt = jnp.maximum(j_top, 0)
            slack = None
            for hq in range(hps):
                h = hps * g + hq
                bound = tab_ref[1, h, i] * tab_ref[2, h, jt] + (tab_ref[0, h, i] - tab_ref[0, h, jt + 1])
                room = stats[hq] - bound
                slack = room if slack is None else jnp.minimum(slack, room)
            alive = alive & (jnp.min(slack) < SKIP_LOG2)
        return alive.astype(jnp.int32)

    def block_unshifted(j, masked, j_next):
        n_steps = len(steps)
        tiles = {}
        for n, (u, hq) in enumerate(steps):
            ahead = n + LOOKAHEAD
            if ahead < n_steps:
                tiles[ahead] = scores(j, steps[ahead])
            else:
                tiles[ahead] = scores(j_next, steps[ahead - n_steps])
            sT = tiles.pop(n) if n in tiles else s_ref[n]
            rows = slice(u * KEY_SUB, (u + 1) * KEY_SUB)
            if masked:
                sT = jnp.where(visible[rows], sT, NEG)
            pT = jnp.exp2(sT).astype(BF16)
            v = jnp.concatenate([vT_ref[j, hq * HEAD_DV:(hq + 1) * HEAD_DV, rows], ones_rows], axis=0)
            acc_ref[hq] = acc_ref[hq] + _dot(v, pT)
        for n, tile in tiles.items():
            s_ref[n - n_steps] = tile

    def walk_online():
        m0 = jnp.full((1, tq), NEG, F32)
        stats = block(i, (m0,) * hps, True, jnp.maximum(i - 1, 0))

        def visit(c):
            j = c[0]
            st = block(j, c[2:], False, jnp.maximum(j - 1, 0))
            return (j - 1, more(j - 1, st)) + tuple(st)

        lax.while_loop(lambda c: c[1] > 0, visit, (i - 1, more(i - 1, stats)) + tuple(stats))

    def walk_unshifted():
        block_unshifted(i, True, jnp.maximum(i - 1, 0))

        def visit(k, c):
            j = i - 1 - k
            block_unshifted(j, False, jnp.maximum(j - 1, 0))
            return c

        lax.fori_loop(0, i, visit, 0)

    acc_ref[...] = jnp.zeros_like(acc_ref)
    for n in range(LOOKAHEAD):
        s_ref[n] = scores(i, steps[n])
    if fox:
        walk_online()
    else:
        bounded = None
        for hq in range(hps):
            h = hps * g + hq
            ok = tab_ref[1, h, i] * tab_ref[2, h, i] <= SAFE_LOG2
            bounded = ok if bounded is None else bounded & ok
        pl.when(bounded)(walk_unshifted)
        pl.when(jnp.logical_not(bounded))(walk_online)
    outs = [acc_ref[hq, 0:HEAD_DV, :] / acc_ref[hq, HEAD_DV:HEAD_DV + 1, :] for hq in range(hps)]
    o_ref[...] = jnp.concatenate(outs, axis=0).T.astype(BF16)


def _attention(tab, qpT, qeT, ka, ke, vT, fox):
    nb, tk, _ = ka.shape
    s = qpT.shape[1]
    tq = tk
    hps = FOX_HEADS_PER_STEP if fox else HEADS_PER_STEP
    width = hps * HEAD_DV
    e_rows = qeT.shape[0] if fox else hps * ROPE_DIM
    e_map = (lambda g, i, b: (0, i)) if fox else (lambda g, i, b: (g, i))
    once = dict(pipeline_mode=pl.Buffered(1))
    grid_spec = pltpu.PrefetchScalarGridSpec(
        num_scalar_prefetch=1,
        grid=(N_HEADS // hps, nb),
        in_specs=[
            pl.BlockSpec((width, tq), lambda g, i, b: (g, i)),
            pl.BlockSpec((e_rows, tq), e_map),
            pl.BlockSpec((nb, tk, width), lambda g, i, b: (0, 0, g), **once),
            pl.BlockSpec((nb, tk, LANES), lambda g, i, b: (0, 0, 0), **once),
            pl.BlockSpec((nb, width, tk), lambda g, i, b: (0, g, 0), **once),
        ],
        out_specs=pl.BlockSpec((tq, width), lambda g, i, b: (i, g)),
        scratch_shapes=[pltpu.VMEM((hps, HEAD_DV + DEN_ROWS, tq), F32),
                        pltpu.VMEM((LOOKAHEAD, KEY_SUB, tq), F32)],
    )
    return pl.pallas_call(
        functools.partial(_attn_kernel, fox=fox), grid_spec=grid_spec,
        out_shape=jax.ShapeDtypeStruct((s, HEADS_W), BF16),
        compiler_params=_params(2), name="attn_fox" if fox else "attn_mla",
    )(tab, qpT, qeT, ka, ke, vT)


def _softmax_pv(s, v):
    m = jnp.max(s, axis=-1, keepdims=True)
    p = jnp.exp2(s - m)
    l = jnp.sum(p, axis=-1, keepdims=True)
    return _dot(p.astype(BF16), v) / l


def _diag_heads(o_big, t_new):
    rows = lax.broadcasted_iota(jnp.int32, o_big.shape, 0) // t_new
    cols = lax.broadcasted_iota(jnp.int32, o_big.shape, 1) // HEAD_DV
    kept = jnp.where(rows == cols, o_big, 0.0)
    return jnp.sum(kept.reshape(N_HEADS, t_new, o_big.shape[1]), axis=0)


def _sample_attn_kernel(qn_ref, qr_ref, qf_ref, lrep_ref, cckv_ref, nckv_ref, ckr_ref, nkr_ref,
                        cfk_ref, nfk_ref, cfv_ref, nfv_ref, wukn_ref, wuv_ref, oa_ref, ob_ref):
    past = cckv_ref.shape[2]
    t_new = nckv_ref.shape[0]
    rows = N_HEADS * t_new
    pad = lrep_ref.shape[2] - past - t_new
    keys = past + t_new + pad

    def with_new(cache, new):
        return jnp.concatenate([cache, new, jnp.zeros((pad, new.shape[1]), new.dtype)], axis=0)

    kidx = lax.broadcasted_iota(jnp.int32, (rows, keys), 1)
    qpos = past + lax.broadcasted_iota(jnp.int32, (rows, keys), 0) % t_new
    real = kidx < past + t_new

    ckv_all = with_new(cckv_ref[0, 0], nckv_ref[...]).astype(BF16)
    kn = _dot(ckv_all, wukn_ref[0]).astype(BF16)
    va = _dot(ckv_all, wuv_ref[0]).astype(BF16)
    kr = with_new(ckr_ref[0, 0], nkr_ref[:, 0:ROPE_DIM]).astype(BF16)
    s = _dot_nt(qn_ref[0], kn) + _dot_nt(qr_ref[0], kr)
    s = jnp.where(real & ((kidx // CHUNK) <= (qpos // CHUNK)), s, NEG)
    oa_ref[...] = _diag_heads(_softmax_pv(s, va), t_new).astype(BF16)

    fk = with_new(cfk_ref[0, 0], nfk_ref[...]).astype(BF16)
    fv = with_new(cfv_ref[0, 0], nfv_ref[...]).astype(BF16)
    c = lrep_ref[0]
    shift = 1
    while shift < keys:
        c = c + jnp.where(kidx >= shift, pltpu.roll(c, shift, 1), 0.0)
        shift *= 2
    cq = jnp.sum(jnp.where(kidx == qpos, c, 0.0), axis=-1, keepdims=True)
    s = _dot_nt(qf_ref[0], fk) + (cq - c) * LOG2E
    s = jnp.where(real & (kidx <= qpos), s, NEG)
    ob_ref[...] = _diag_heads(_softmax_pv(s, fv), t_new).astype(BF16)


def _sample_attention(layer, qn_bd, qr, qf_bd, lrep, cckv, nckv, ckr, nkr, cfk, nfk, cfv, nfv, wukn, wuv):
    nbatch = cckv.shape[1]
    t_new = nckv.shape[0] // nbatch
    b3 = lambda a: pl.BlockSpec((1,) + a.shape[1:], lambda b: (b, 0, 0))
    cache = lambda a: pl.BlockSpec((1, 1) + a.shape[2:], lambda b: (layer, b, 0, 0))
    new = lambda a: pl.BlockSpec((t_new, a.shape[1]), lambda b: (b, 0))
    in_specs = [b3(qn_bd), b3(qr), b3(qf_bd), b3(lrep), cache(cckv), new(nckv), cache(ckr), new(nkr),
                cache(cfk), new(nfk), cache(cfv), new(nfv), _layer_spec(wukn, layer), _layer_spec(wuv, layer)]
    out = jax.ShapeDtypeStruct((nbatch * t_new, HEADS_W), BF16)
    o_spec = pl.BlockSpec((t_new, HEADS_W), lambda b: (b, 0))
    return pl.pallas_call(
        _sample_attn_kernel, grid=(nbatch,), in_specs=in_specs, out_specs=(o_spec, o_spec),
        out_shape=(out, out), compiler_params=_params(1), name="attn_sample",
    )(qn_bd, qr, qf_bd, lrep, cckv, nckv, ckr, nkr, cfk, nfk, cfv, nfv, wukn, wuv)


def _merge_kernel(x_ref, oa_ref, za_ref, ob_ref, zb_ref, ga_ref, gb_ref, woa_ref, wob_ref, wout_ref,
                  fg_ref, xn_ref, *maybe_y_ref):
    a = _dot(oa_ref[...] * za_ref[...], woa_ref[0])
    b = _dot(ob_ref[...] * zb_ref[...], wob_ref[0])
    mix = ga_ref[...].astype(F32) * a + gb_ref[...].astype(F32) * b
    xn = x_ref[...] + _dot(mix.astype(BF16), wout_ref[0])
    xn_ref[...] = xn
    if maybe_y_ref:
        maybe_y_ref[0][...] = _rms(xn, fg_ref[...])


def _merge(layer, x, oa, za, ob, zb, ga, gb, woa, wob, wout, fg, tm, final):
    m = x.shape[0]
    row = lambda c: pl.BlockSpec((tm, c), lambda i: (i, 0))
    in_specs = [row(D_MODEL), row(HEADS_W), row(HEADS_W), row(HEADS_W), row(HEADS_W), row(D_MODEL),
                row(D_MODEL), _layer_spec(woa, layer), _layer_spec(wob, layer), _layer_spec(wout, layer),
                _full_spec(fg.shape)]
    n_out = 2 if final else 1
    out_shape = tuple(jax.ShapeDtypeStruct((m, D_MODEL), F32) for _ in range(n_out))
    out_specs = tuple(row(D_MODEL) for _ in range(n_out))
    return pl.pallas_call(
        _merge_kernel, grid=(m // tm,), in_specs=in_specs, out_specs=out_specs, out_shape=out_shape,
        compiler_params=_params(1), name="merge",
    )(x, oa, za, ob, zb, ga, gb, woa, wob, wout, fg)


def _pad_cols(w, width):
    return jnp.pad(w, ((0, 0), (0, width - w.shape[1])))


def _pad_last(w, width):
    return jnp.pad(w, [(0, 0)] * (w.ndim - 1) + [(0, width - w.shape[-1])])


def _stacked_weights(w_in, w_uq, w_ukv, w_oa, w_ob, w_out, b_f, norm_g, g_q, g_kv):
    depth = w_in.shape[0]
    offs = np.cumsum(IN_SIZES)[:-1].tolist()
    w_cq, w_ckv, w_kr, w_za, w_fq, w_fk, w_fv, w_zf, w_zb, w_ga, w_gb = jnp.split(w_in, offs, axis=2)
    w_kr_swapped = jnp.concatenate([w_kr[..., HALF_ROPE:], w_kr[..., :HALF_ROPE]], axis=2)
    w_mla = jnp.concatenate([w_cq, w_ckv, _pad_last(w_kr, LANES), _pad_last(w_kr_swapped, LANES),
                             w_za, w_ga], axis=2).astype(BF16)
    w_fox = jnp.concatenate([w_fk, w_fv, _pad_last(w_zf, LANES), w_zb, w_gb], axis=2).astype(BF16)
    uq = w_uq.reshape(depth, Q_LORA, N_HEADS, NOPE_DIM + ROPE_DIM)
    uq_rows = jnp.concatenate([
        uq[..., :NOPE_DIM].reshape(depth, Q_LORA, -1),
        uq[..., NOPE_DIM:NOPE_DIM + HALF_ROPE].reshape(depth, Q_LORA, -1),
        uq[..., NOPE_DIM + HALF_ROPE:].reshape(depth, Q_LORA, -1)], axis=2)
    ukv = w_ukv.reshape(depth, KV_LORA, N_HEADS, NOPE_DIM + HEAD_DV)
    w_ukn = ukv[..., :NOPE_DIM].reshape(depth, KV_LORA, -1).astype(BF16)
    w_uv = ukv[..., NOPE_DIM:].reshape(depth, KV_LORA, -1).astype(BF16)
    tr = lambda a: jnp.swapaxes(a, 1, 2)
    return dict(
        w_mla=w_mla, w_fox=w_fox, w_uqT=tr(uq_rows).astype(BF16), w_ukn=w_ukn, w_uv=w_uv, w_uvT=tr(w_uv),
        w_fqT=tr(w_fq).astype(BF16),
        b_f=_pad_last(b_f[:, None, :], LANES),
        w_oa=w_oa.astype(BF16), w_ob=w_ob.astype(BF16), w_out=w_out.astype(BF16),
        norm_g=norm_g[:, None, :], g_q=g_q[:, None, :], g_kv=g_kv[:, None, :])


def _rope_tables(pos):
    inv = jnp.exp(-math.log(ROPE_THETA) * jnp.arange(HALF_ROPE, dtype=F32) / HALF_ROPE)
    ang = pos.astype(F32)[:, None] * inv[None, :]
    cos = jnp.cos(ang)
    sin = jnp.sin(ang)
    cosT = jnp.tile(cos.T, (N_HEADS, 1))
    sinT = jnp.tile(sin.T, (N_HEADS, 1))
    ctok = _pad_cols(jnp.concatenate([cos, cos], axis=1), LANES)
    stok = _pad_cols(jnp.concatenate([-sin, sin], axis=1), LANES)
    return cosT, sinT, ctok, stok


def _bias_placement():
    pk = np.zeros((3 * LANES, LANES), np.float32)
    pqt = np.zeros((N_HEADS * AUG_PER_HEAD, 3 * LANES), np.float32)
    for c in range(3):
        for h in range(N_HEADS):
            pk[c * LANES + h, h * AUG_PER_HEAD + 3 + c] = -1.0
            pqt[h * AUG_PER_HEAD + c, c * LANES + h] = 1.0
    return jnp.asarray(pk, BF16), jnp.asarray(pqt, BF16)


def _head_groups():
    grp = np.zeros((HEADS_W, LANES), np.float32)
    for h in range(N_HEADS):
        grp[h * HEAD_DV:(h + 1) * HEAD_DV, h] = 1.0
    return jnp.asarray(grp, BF16)


def _skip_tables(base, qn2, kn2):
    b = base[:, 0, :N_HEADS]
    qmax = jnp.sqrt(qn2[:, :, 0]) * NORM_SLACK
    kmax = lax.cummax(jnp.sqrt(kn2[:, 0, :N_HEADS]) * NORM_SLACK, axis=0)
    return jnp.stack([b, qmax, kmax]).transpose(0, 2, 1)


def _tri(n):
    return jnp.asarray(np.tril(np.ones((n, n), np.float32)), BF16)


def _block_diag_queries(qT, nbatch, t_new):
    q = qT.T.reshape(nbatch, t_new, N_HEADS, HEAD_DV)
    eye = jnp.eye(N_HEADS, dtype=q.dtype)
    return jnp.einsum('bthj,hg->bhtgj', q, eye).reshape(nbatch, N_HEADS * t_new, HEADS_W)


def _project(layer, x, sw, tables, tri, pk, pqt, grp, tm, stack_mla=None, stack_fox=None):
    cosT, sinT, ctok, stok = tables
    mla = _proj_mla(layer, x, sw['norm_g'], sw['w_mla'], sw['g_q'], sw['g_kv'], sw['w_uqT'], sw['w_ukn'],
                    sw['w_uvT'], cosT, sinT, ctok, stok, grp, tm, stack_mla)
    fox = _proj_fox(layer, x, sw['norm_g'], sw['w_fox'], sw['w_fqT'], sw['b_f'], tri, pk, pqt, grp,
                    tm, stack_fox)
    return mla, fox


def kernel(x_prompt, x_sample, cache_mla_ckv, cache_mla_krope, cache_fox_k, cache_fox_v, cache_fox_logf,
           norm_g, w_in, g_q, w_uq, g_kv, w_ukv, b_f, w_oa, w_ob, w_out, final_g):
    depth = w_in.shape[0]
    _, seq, _ = x_prompt.shape
    nbatch, t_new, _ = x_sample.shape
    past = cache_mla_ckv.shape[2]
    m_s = nbatch * t_new
    key_pad = -(past + t_new) % LANES

    tabs_p = _rope_tables(jnp.arange(seq, dtype=jnp.int32))
    tabs_s = _rope_tables(past + jnp.arange(m_s, dtype=jnp.int32) % t_new)
    pk, pqt = _bias_placement()
    grp = _head_groups()
    tri_p = _tri(SEQ_BLOCK)
    tri_s = _tri(m_s)
    fg = final_g[None, :]

    xp = x_prompt.reshape(seq, D_MODEL)
    xs = x_sample.reshape(m_s, D_MODEL)
    outs = {k: [] for k in ('p_kr', 'p_lf', 's_ckv', 's_kr', 's_fk', 's_fv', 's_lf')}
    p_ckv_stack = p_kv_stack = None
    yp = ys = None
    sw = _stacked_weights(w_in, w_uq, w_ukv, w_oa, w_ob, w_out, b_f, norm_g, g_q, g_kv)
    for l in range(depth):
        final = l == depth - 1

        mla, fox = _project(l, xp, sw, tabs_p, tri_p, pk, pqt, grp, SEQ_BLOCK,
                            (l, depth, p_ckv_stack), (l, depth, p_kv_stack))
        ckv, kr, za, ga, qnT, qrT, knb, krb, vT, qn2_a, kn2_a = mla
        fk, fv, logf, zb, gb, fqT, fkb, fvT, augk, augqT, base, qn2, kn2 = fox
        tab = _skip_tables(base, qn2, kn2)
        tab_a = _skip_tables(jnp.zeros_like(base), qn2_a, kn2_a)
        o_a = _attention(tab_a, qnT, qrT, knb, krb, vT, fox=False)
        o_b = _attention(tab, fqT, augqT, fkb, augk, fvT, fox=True)
        res = _merge(l, xp, o_a, za, o_b, zb, ga, gb, sw['w_oa'], sw['w_ob'], sw['w_out'], fg, SEQ_BLOCK, final)
        xp = res[0]
        if final:
            yp = res[1]
        p_ckv_stack, p_kv_stack = (ckv,), (fk, fv)
        outs['p_kr'].append(kr[:, :ROPE_DIM]); outs['p_lf'].append(logf)

        mla, fox = _project(l, xs, sw, tabs_s, tri_s, pk, pqt, grp, m_s)
        ckv, kr, za, ga, qnT, qrT = mla[:6]
        fk, fv, logf, zb, gb, fqT = fox[:6]
        qn_bd = _block_diag_queries(qnT, nbatch, t_new)
        qf_bd = _block_diag_queries(fqT, nbatch, t_new)
        qr = qrT.reshape(N_HEADS, ROPE_DIM, nbatch, t_new).transpose(2, 0, 3, 1).reshape(
            nbatch, N_HEADS * t_new, ROPE_DIM)
        lf_all = jnp.concatenate([cache_fox_logf[l], logf.reshape(nbatch, t_new, N_HEADS),
                                  jnp.zeros((nbatch, key_pad, N_HEADS), F32)], axis=1)
        lrep = jnp.repeat(lf_all.transpose(0, 2, 1), t_new, axis=1)
        o_a, o_b = _sample_attention(
            l, qn_bd, qr, qf_bd, lrep, cache_mla_ckv, ckv, cache_mla_krope, kr,
            cache_fox_k.reshape(depth, nbatch, past, HEADS_W), fk,
            cache_fox_v.reshape(depth, nbatch, past, HEADS_W), fv, sw['w_ukn'], sw['w_uv'])
        res = _merge(l, xs, o_a, za, o_b, zb, ga, gb, sw['w_oa'], sw['w_ob'], sw['w_out'], fg, m_s, final)
        xs = res[0]
        if final:
            ys = res[1]
        outs['s_ckv'].append(ckv); outs['s_kr'].append(kr[:, :ROPE_DIM]); outs['s_fk'].append(fk)
        outs['s_fv'].append(fv); outs['s_lf'].append(logf)

    st = lambda name, shape: jnp.stack(outs[name]).reshape((depth,) + shape)
    return (yp.reshape(1, seq, D_MODEL), ys.reshape(nbatch, t_new, D_MODEL),
            p_ckv_stack[0].reshape(depth, 1, seq, KV_LORA), st('p_kr', (1, seq, ROPE_DIM)),
            p_kv_stack[0].reshape(depth, 1, seq, N_HEADS, HEAD_DV),
            p_kv_stack[1].reshape(depth, 1, seq, N_HEADS, HEAD_DV),
            st('p_lf', (1, seq, N_HEADS)),
            st('s_ckv', (nbatch, t_new, KV_LORA)), st('s_kr', (nbatch, t_new, ROPE_DIM)),
            st('s_fk', (nbatch, t_new, N_HEADS, HEAD_DV)), st('s_fv', (nbatch, t_new, N_HEADS, HEAD_DV)),
            st('s_lf', (nbatch, t_new, N_HEADS)))
```

```python
import functools
import math

import numpy as np
import jax
import jax.numpy as jnp
from jax import lax
from jax.experimental import pallas as pl
from jax.experimental.pallas import tpu as pltpu

D_MODEL = 1024
N_HEADS = 8
NOPE_DIM = 64
ROPE_DIM = 32
HALF_ROPE = ROPE_DIM // 2
HEAD_DV = 64
Q_LORA = 384
KV_LORA = 256
HEADS_W = N_HEADS * HEAD_DV
CHUNK = 64
ROPE_THETA = 10000.0
LOG2E = math.log2(math.e)
MLA_QSCALE = LOG2E / math.sqrt(NOPE_DIM + ROPE_DIM)
FOX_QSCALE = LOG2E / math.sqrt(HEAD_DV)
EPS = 1e-6
NEG = -1e30
IN_SIZES = (Q_LORA, KV_LORA, ROPE_DIM, HEADS_W, HEADS_W, HEADS_W, HEADS_W, N_HEADS, HEADS_W, D_MODEL, D_MODEL)

LANES = 128
SEQ_BLOCK = 512
KEY_SUB = 256
DEN_ROWS = 16
HEADS_PER_STEP = 8
LOOKAHEAD = 2
SAFE_LOG2 = 100.0
SKIP_LOG2 = 64.0
NORM_SLACK = 1.01
AUG_PER_HEAD = 8
VMEM_LIMIT_BYTES = 56 * 1024 * 1024

F32 = jnp.float32
BF16 = jnp.bfloat16


def _dot(a, b):
    return jnp.dot(a, b, preferred_element_type=F32)


def _dot_nt(a, b):
    return lax.dot_general(a, b, (((1,), (1,)), ((), ())), preferred_element_type=F32)


def _rms(x, g):
    return x * lax.rsqrt(jnp.mean(x * x, axis=-1, keepdims=True) + EPS) * g


def _sigmoid(x):
    return 1.0 / (1.0 + jnp.exp(-x))


def _silu(x):
    return x * _sigmoid(x)


def _split3(x):
    hi = x.astype(BF16)
    r1 = x - hi.astype(F32)
    mid = r1.astype(BF16)
    lo = (r1 - mid.astype(F32)).astype(BF16)
    return hi, mid, lo


def _full_spec(shape):
    nd = len(shape)
    return pl.BlockSpec(shape, lambda *_: (0,) * nd)


def _layer_spec(arr, layer):
    nd = arr.ndim
    return pl.BlockSpec((1,) + arr.shape[1:], lambda *_: (layer,) + (0,) * (nd - 1))


def _put(ref, val):
    ref[...] = val.reshape(ref.shape)


def _skip_refs(body, n_in, n_skip, *refs):
    return body(*refs[:n_in], *refs[n_in + n_skip:])


def _stacked(stack, m, tm, cols):
    if stack is None:
        return jax.ShapeDtypeStruct((m, cols), F32), pl.BlockSpec((tm, cols), lambda i: (i, 0))
    layer, depth, _ = stack
    return (jax.ShapeDtypeStruct((depth, m, cols), F32),
            pl.BlockSpec((1, tm, cols), lambda i: (layer, i, 0)))


def _alias_args(stack, n_in, out_indices):
    if stack is None or stack[2] is None:
        return [], [], {}
    prev = list(stack[2])
    specs = [pl.BlockSpec(memory_space=pl.ANY) for _ in prev]
    return prev, specs, {n_in + k: out for k, out in enumerate(out_indices)}


def _params(n_axes):
    return pltpu.CompilerParams(dimension_semantics=("arbitrary",) * n_axes,
                                vmem_limit_bytes=VMEM_LIMIT_BYTES)


MLA_IN_COLS = Q_LORA + KV_LORA + 2 * LANES + HEADS_W + D_MODEL


def _proj_mla_kernel(x_ref, g_ref, w_ref, gq_ref, gkv_ref, wuq_ref, wukn_ref, wuvt_ref,
                     cosT_ref, sinT_ref, ctok_ref, stok_ref, grp_ref,
                     ckv_ref, kr_ref, za_ref, ga_ref, qnT_ref, qrT_ref, knb_ref, krb_ref, vT_ref,
                     qn2_ref, kn2_ref):
    tm = x_ref.shape[0]
    hb = _rms(x_ref[...], g_ref[0]).astype(BF16)
    o = 0
    zcq = _dot(hb, w_ref[0, :, o:o + Q_LORA]); o += Q_LORA
    zckv = _dot(hb, w_ref[0, :, o:o + KV_LORA]); o += KV_LORA
    zka = _dot(hb, w_ref[0, :, o:o + LANES]); o += LANES
    zkb = _dot(hb, w_ref[0, :, o:o + LANES]); o += LANES
    za_ref[...] = _silu(_dot(hb, w_ref[0, :, o:o + HEADS_W])).astype(BF16); o += HEADS_W
    ga_ref[...] = _sigmoid(_dot(hb, w_ref[0, :, o:o + D_MODEL])).astype(BF16)

    cqb = _rms(zcq, gq_ref[0]).astype(BF16)
    qT = _dot_nt(wuq_ref[0], cqb)
    qn = qT[0:HEADS_W] * MLA_QSCALE
    qnT_ref[...] = qn.astype(BF16)
    x1 = qT[HEADS_W:HEADS_W + LANES]
    x2 = qT[HEADS_W + LANES:HEADS_W + 2 * LANES]
    c = cosT_ref[...]
    s = sinT_ref[...]
    r1 = (x1 * c - x2 * s) * MLA_QSCALE
    r2 = (x1 * s + x2 * c) * MLA_QSCALE
    n1 = r1.astype(BF16)
    n2 = r2.astype(BF16)
    for h in range(N_HEADS):
        qrT_ref[h * ROPE_DIM:h * ROPE_DIM + HALF_ROPE, :] = n1[h * HALF_ROPE:(h + 1) * HALF_ROPE]
        qrT_ref[h * ROPE_DIM + HALF_ROPE:(h + 1) * ROPE_DIM, :] = n2[h * HALF_ROPE:(h + 1) * HALF_ROPE]
    qn2 = (jnp.sum((qn * qn).reshape(N_HEADS, NOPE_DIM, tm), axis=1)
           + jnp.sum((r1 * r1 + r2 * r2).reshape(N_HEADS, HALF_ROPE, tm), axis=1))
    qn2_ref[0] = jnp.broadcast_to(jnp.max(qn2, axis=1, keepdims=True), (N_HEADS, LANES))

    ckv = _rms(zckv, gkv_ref[0])
    _put(ckv_ref, ckv)
    cb = ckv.astype(BF16)
    kn = _dot(cb, wukn_ref[0])
    knb_ref[0] = kn.astype(BF16)
    vT_ref[0] = _dot_nt(wuvt_ref[0], cb).astype(BF16)

    kr = zka * ctok_ref[...] + zkb * stok_ref[...]
    kr_ref[...] = kr
    krb_ref[0] = kr.astype(BF16)
    kn2 = _dot((kn * kn).astype(BF16), grp_ref[...]) + jnp.sum(kr * kr, axis=1, keepdims=True)
    kn2_ref[0] = jnp.max(kn2, axis=0, keepdims=True)


def _proj_mla(layer, x, g, w, gq, gkv, wuq, wukn, wuvt, cosT, sinT, ctok, stok, grp, tm, stack=None):
    m = x.shape[0]
    nb = m // tm
    row = lambda c: pl.BlockSpec((tm, c), lambda i: (i, 0))
    colT = lambda r: pl.BlockSpec((r, tm), lambda i: (0, i))
    blk3 = lambda a, b: pl.BlockSpec((1, a, b), lambda i: (i, 0, 0))
    ckv_shape, ckv_spec = _stacked(stack, m, tm, KV_LORA)
    out_shape = (
        ckv_shape,
        jax.ShapeDtypeStruct((m, LANES), F32),
        jax.ShapeDtypeStruct((m, HEADS_W), BF16),
        jax.ShapeDtypeStruct((m, D_MODEL), BF16),
        jax.ShapeDtypeStruct((HEADS_W, m), BF16),
        jax.ShapeDtypeStruct((N_HEADS * ROPE_DIM, m), BF16),
        jax.ShapeDtypeStruct((nb, tm, HEADS_W), BF16),
        jax.ShapeDtypeStruct((nb, tm, LANES), BF16),
        jax.ShapeDtypeStruct((nb, HEADS_W, tm), BF16),
        jax.ShapeDtypeStruct((nb, N_HEADS, LANES), F32),
        jax.ShapeDtypeStruct((nb, 1, LANES), F32),
    )
    out_specs = (ckv_spec, row(LANES), row(HEADS_W), row(D_MODEL), colT(HEADS_W),
                 colT(N_HEADS * ROPE_DIM), blk3(tm, HEADS_W), blk3(tm, LANES), blk3(HEADS_W, tm),
                 blk3(N_HEADS, LANES), blk3(1, LANES))
    in_specs = [row(D_MODEL)] + [_layer_spec(a, layer) for a in (g, w, gq, gkv, wuq, wukn, wuvt)] + [
        colT(LANES), colT(LANES), row(LANES), row(LANES), _full_spec(grp.shape)]
    prev, prev_specs, aliases = _alias_args(stack, len(in_specs), (0,))
    return pl.pallas_call(
        functools.partial(_skip_refs, _proj_mla_kernel, len(in_specs), len(prev)),
        grid=(nb,), in_specs=in_specs + prev_specs, out_specs=out_specs, out_shape=out_shape,
        input_output_aliases=aliases, compiler_params=_params(1), name="proj_mla",
    )(x, g, w, gq, gkv, wuq, wukn, wuvt, cosT, sinT, ctok, stok, grp, *prev)


FOX_IN_COLS = 2 * HEADS_W + LANES + HEADS_W + D_MODEL


def _proj_fox_kernel(x_ref, g_ref, w_ref, wfqt_ref, bf_ref, tri_ref, pk_ref, pqt_ref, grp_ref,
                     fk_ref, fv_ref, logf_ref, zb_ref, gb_ref, fqT_ref, fkb_ref, fvT_ref,
                     augk_ref, augqT_ref, base_ref, qn2_ref, kn2_ref, carry_ref):
    i = pl.program_id(0)
    tm = x_ref.shape[0]
    hb = _rms(x_ref[...], g_ref[0]).astype(BF16)
    o = 0
    fk = _dot(hb, w_ref[0, :, o:o + HEADS_W]); o += HEADS_W
    _put(fk_ref, fk)
    fkb_ref[0] = fk.astype(BF16)
    kn2_ref[0] = jnp.max(_dot((fk * fk).astype(BF16), grp_ref[...]), axis=0, keepdims=True)
    fv = _dot(hb, w_ref[0, :, o:o + HEADS_W]); o += HEADS_W
    _put(fv_ref, fv)
    fvT_ref[0] = fv.T.astype(BF16)
    zf = _dot(hb, w_ref[0, :, o:o + LANES]) + bf_ref[0]; o += LANES
    zb_ref[...] = _silu(_dot(hb, w_ref[0, :, o:o + HEADS_W])).astype(BF16); o += HEADS_W
    gb_ref[...] = _sigmoid(_dot(hb, w_ref[0, :, o:o + D_MODEL])).astype(BF16)
    fqT = _dot_nt(wfqt_ref[0], hb) * FOX_QSCALE
    fqT_ref[...] = fqT.astype(BF16)
    qn2 = jnp.sum((fqT * fqT).reshape(N_HEADS, HEAD_DV, tm), axis=1)
    qn2_ref[0] = jnp.broadcast_to(jnp.max(qn2, axis=1, keepdims=True), (N_HEADS, LANES))

    lane = lax.broadcasted_iota(jnp.int32, (tm, LANES), 1)
    logf = jnp.minimum(zf, 0.0) - jnp.log(1.0 + jnp.exp(-jnp.abs(zf)))
    logf = jnp.where(lane < N_HEADS, logf, 0.0)
    logf_ref[...] = logf[:, 0:N_HEADS]

    hi, mid, lo = _split3(logf)
    packed = (hi.astype(F32) + pltpu.roll(mid.astype(F32), N_HEADS, 1)
              + pltpu.roll(lo.astype(F32), 2 * N_HEADS, 1)).astype(BF16)
    cum3 = _dot(tri_ref[...], packed)
    r = cum3 + pltpu.roll(cum3, LANES - N_HEADS, 1) + pltpu.roll(cum3, LANES - 2 * N_HEADS, 1)
    r = jnp.where(lane < N_HEADS, r, 0.0)

    @pl.when(i == 0)
    def _():
        carry_ref[...] = jnp.zeros_like(carry_ref)

    base_ref[0] = carry_ref[...] * LOG2E
    carry_ref[...] = carry_ref[...] + r[tm - 1:tm, :]

    rh, rm, rl = _split3(r * LOG2E)
    rcat = jnp.concatenate([rh, rm, rl], axis=1)
    slot = lane % AUG_PER_HEAD
    ones_k = jnp.where((lane < N_HEADS * AUG_PER_HEAD) & (slot < 3), 1.0, 0.0)
    augk_ref[0] = (_dot(rcat, pk_ref[...]) + ones_k).astype(BF16)
    rowq = lax.broadcasted_iota(jnp.int32, (N_HEADS * AUG_PER_HEAD, tm), 0) % AUG_PER_HEAD
    ones_q = jnp.where((rowq >= 3) & (rowq < 6), 1.0, 0.0)
    augqT_ref[...] = (_dot_nt(pqt_ref[...], rcat) + ones_q).astype(BF16)


def _proj_fox(layer, x, g, w, wfqt, bf, tri, pk, pqt, grp, tm, stack=None):
    m = x.shape[0]
    nb = m // tm
    row = lambda c: pl.BlockSpec((tm, c), lambda i: (i, 0))
    colT = lambda r: pl.BlockSpec((r, tm), lambda i: (0, i))
    blk3 = lambda a, b: pl.BlockSpec((1, a, b), lambda i: (i, 0, 0))
    kv_shape, kv_spec = _stacked(stack, m, tm, HEADS_W)
    out_shape = (
        kv_shape,
        kv_shape,
        jax.ShapeDtypeStruct((m, N_HEADS), F32),
        jax.ShapeDtypeStruct((m, HEADS_W), BF16),
        jax.ShapeDtypeStruct((m, D_MODEL), BF16),
        jax.ShapeDtypeStruct((HEADS_W, m), BF16),
        jax.ShapeDtypeStruct((nb, tm, HEADS_W), BF16),
        jax.ShapeDtypeStruct((nb, HEADS_W, tm), BF16),
        jax.ShapeDtypeStruct((nb, tm, LANES), BF16),
        jax.ShapeDtypeStruct((N_HEADS * AUG_PER_HEAD, m), BF16),
        jax.ShapeDtypeStruct((nb, 1, LANES), F32),
        jax.ShapeDtypeStruct((nb, N_HEADS, LANES), F32),
        jax.ShapeDtypeStruct((nb, 1, LANES), F32),
    )
    out_specs = (kv_spec, kv_spec, row(N_HEADS), row(HEADS_W), row(D_MODEL), colT(HEADS_W),
                 blk3(tm, HEADS_W), blk3(HEADS_W, tm), blk3(tm, LANES), colT(N_HEADS * AUG_PER_HEAD),
                 blk3(1, LANES), blk3(N_HEADS, LANES), blk3(1, LANES))
    in_specs = ([row(D_MODEL)] + [_layer_spec(a, layer) for a in (g, w, wfqt, bf)]
                + [_full_spec(a.shape) for a in (tri, pk, pqt, grp)])
    prev, prev_specs, aliases = _alias_args(stack, len(in_specs), (0, 1))
    return pl.pallas_call(
        functools.partial(_skip_refs, _proj_fox_kernel, len(in_specs), len(prev)),
        grid=(nb,), in_specs=in_specs + prev_specs, out_specs=out_specs, out_shape=out_shape,
        scratch_shapes=[pltpu.VMEM((1, LANES), F32)],
        input_output_aliases=aliases, compiler_params=_params(1), name="proj_fox",
    )(x, g, w, wfqt, bf, tri, pk, pqt, grp, *prev)


def _attn_kernel(tab_ref, qp_ref, qe_ref, ka_ref, ke_ref, vT_ref, o_ref, acc_ref, s_ref, *, fox):
    g = pl.program_id(0)
    i = pl.program_id(1)
    tq = qp_ref.shape[1]
    tk = ka_ref.shape[1]
    rows_p = lax.broadcasted_iota(jnp.int32, (LANES, tq), 0)
    rows_e = lax.broadcasted_iota(jnp.int32, qe_ref.shape, 0)
    kpos = lax.broadcasted_iota(jnp.int32, (tk, tq), 0)
    qpos = lax.broadcasted_iota(jnp.int32, (tk, tq), 1)
    if fox:
        visible = kpos <= qpos
    else:
        visible = (kpos // CHUNK) <= (qpos // CHUNK)
    qe = qe_ref[...]

    ws = []
    for hq in range(HEADS_PER_STEP):
        pair, hh = divmod(hq, 2)
        qp = qp_ref[pair * LANES:(pair + 1) * LANES, :]
        keep_p = (rows_p >= hh * HEAD_DV) & (rows_p < (hh + 1) * HEAD_DV)
        if fox:
            h = HEADS_PER_STEP * g + hq
            keep_e = (rows_e >= h * AUG_PER_HEAD) & (rows_e < (h + 1) * AUG_PER_HEAD)
            extra = jnp.where(keep_e, qe, jnp.zeros_like(qe))
        else:
            extra = qe[hq * ROPE_DIM:(hq + 1) * ROPE_DIM]
        zero_rows = jnp.zeros((LANES - extra.shape[0], tq), BF16)
        ws.append(jnp.concatenate([jnp.where(keep_p, qp, jnp.zeros_like(qp)), extra, zero_rows], axis=0))

    steps = [(u, hq) for u in range(tk // KEY_SUB) for hq in range(HEADS_PER_STEP)]
    ones_rows = jnp.ones((DEN_ROWS, KEY_SUB), BF16)

    def scores(j, step):
        u, hq = step
        pair = hq // 2
        rows = slice(u * KEY_SUB, (u + 1) * KEY_SUB)
        lhs = jnp.concatenate([ka_ref[j, rows, pair * LANES:(pair + 1) * LANES], ke_ref[j, rows, :]], axis=1)
        return _dot(lhs, ws[hq])

    def block(j, stats, masked, j_next):
        stats = list(stats)
        n_steps = len(steps)
        tiles = {}
        for n, (u, hq) in enumerate(steps):
            ahead = n + LOOKAHEAD
            if ahead < n_steps:
                tiles[ahead] = scores(j, steps[ahead])
            elif j_next is not None:
                tiles[ahead] = scores(j_next, steps[ahead - n_steps])
            sT = tiles.pop(n) if n in tiles else s_ref[n]
            rows = slice(u * KEY_SUB, (u + 1) * KEY_SUB)
            m_old = stats[hq]
            if masked:
                sT = jnp.where(visible[rows], sT, NEG)
            if fox:
                h = HEADS_PER_STEP * g + hq
                d = tab_ref[0, h, i] - tab_ref[0, h, j]
            else:
                d = 0.0
            m_new = jnp.maximum(m_old, jnp.max(sT, axis=0, keepdims=True) + d)
            alpha = jnp.exp2(m_old - m_new)
            pT = jnp.exp2((sT - (m_new - d)).astype(BF16))
            stats[hq] = m_new
            v = jnp.concatenate([vT_ref[j, hq * HEAD_DV:(hq + 1) * HEAD_DV, rows], ones_rows], axis=0)
            acc_ref[hq] = alpha * acc_ref[hq] + _dot(v, pT)
        for n, tile in tiles.items():
            s_ref[n - n_steps] = tile
        return tuple(stats)

    def more(j_top, stats):
        alive = j_top >= 0
        if fox:
            jt = jnp.maximum(j_top, 0)
            slack = None
            for hq in range(HEADS_PER_STEP):
                h = HEADS_PER_STEP * g + hq
                bound = tab_ref[1, h, i] * tab_ref[2, h, jt] + (tab_ref[0, h, i] - tab_ref[0, h, jt + 1])
                room = stats[hq] - bound
                slack = room if slack is None else jnp.minimum(slack, room)
            alive = alive & (jnp.min(slack) < SKIP_LOG2)
        return alive.astype(jnp.int32)

    def block_unshifted(j, masked, j_next):
        n_steps = len(steps)
        tiles = {}
        for n, (u, hq) in enumerate(steps):
            ahead = n + LOOKAHEAD
            if ahead < n_steps:
                tiles[ahead] = scores(j, steps[ahead])
            else:
                tiles[ahead] = scores(j_next, steps[ahead - n_steps])
            sT = tiles.pop(n) if n in tiles else s_ref[n]
            rows = slice(u * KEY_SUB, (u + 1) * KEY_SUB)
            if masked:
                sT = jnp.where(visible[rows], sT, NEG)
            pT = jnp.exp2(sT).astype(BF16)
            v = jnp.concatenate([vT_ref[j, hq * HEAD_DV:(hq + 1) * HEAD_DV, rows], ones_rows], axis=0)
            acc_ref[hq] = acc_ref[hq] + _dot(v, pT)
        for n, tile in tiles.items():
            s_ref[n - n_steps] = tile

    def walk_online():
        m0 = jnp.full((1, tq), NEG, F32)
        stats = block(i, (m0,) * HEADS_PER_STEP, True, jnp.maximum(i - 1, 0))

        def visit(c):
            j = c[0]
            st = block(j, c[2:], False, jnp.maximum(j - 1, 0))
            return (j - 1, more(j - 1, st)) + tuple(st)

        lax.while_loop(lambda c: c[1] > 0, visit, (i - 1, more(i - 1, stats)) + tuple(stats))

    def walk_unshifted():
        block_unshifted(i, True, jnp.maximum(i - 1, 0))

        def visit(k, c):
            j = i - 1 - k
            block_unshifted(j, False, jnp.maximum(j - 1, 0))
            return c

        lax.fori_loop(0, i, visit, 0)

    acc_ref[...] = jnp.zeros_like(acc_ref)
    for n in range(LOOKAHEAD):
        s_ref[n] = scores(i, steps[n])
    if fox:
        walk_online()
    else:
        bounded = None
        for hq in range(HEADS_PER_STEP):
            h = HEADS_PER_STEP * g + hq
            ok = tab_ref[1, h, i] * tab_ref[2, h, i] <= SAFE_LOG2
            bounded = ok if bounded is None else bounded & ok
        pl.when(bounded)(walk_unshifted)
        pl.when(jnp.logical_not(bounded))(walk_online)
    outs = [acc_ref[hq, 0:HEAD_DV, :] / acc_ref[hq, HEAD_DV:HEAD_DV + 1, :] for hq in range(HEADS_PER_STEP)]
    o_ref[...] = jnp.concatenate(outs, axis=0).T.astype(BF16)


def _attention(tab, qpT, qeT, ka, ke, vT, fox):
    nb, tk, _ = ka.shape
    s = qpT.shape[1]
    tq = tk
    width = HEADS_PER_STEP * HEAD_DV
    e_rows = qeT.shape[0] if fox else HEADS_PER_STEP * ROPE_DIM
    e_map = (lambda g, i, b: (0, i)) if fox else (lambda g, i, b: (g, i))
    once = dict(pipeline_mode=pl.Buffered(1))
    grid_spec = pltpu.PrefetchScalarGridSpec(
        num_scalar_prefetch=1,
        grid=(N_HEADS // HEADS_PER_STEP, nb),
        in_specs=[
            pl.BlockSpec((width, tq), lambda g, i, b: (g, i)),
            pl.BlockSpec((e_rows, tq), e_map),
            pl.BlockSpec((nb, tk, width), lambda g, i, b: (0, 0, g), **once),
            pl.BlockSpec((nb, tk, LANES), lambda g, i, b: (0, 0, 0), **once),
            pl.BlockSpec((nb, width, tk), lambda g, i, b: (0, g, 0), **once),
        ],
        out_specs=pl.BlockSpec((tq, width), lambda g, i, b: (i, g)),
        scratch_shapes=[pltpu.VMEM((HEADS_PER_STEP, HEAD_DV + DEN_ROWS, tq), F32),
                        pltpu.VMEM((LOOKAHEAD, KEY_SUB, tq), F32)],
    )
    return pl.pallas_call(
        functools.partial(_attn_kernel, fox=fox), grid_spec=grid_spec,
        out_shape=jax.ShapeDtypeStruct((s, HEADS_W), BF16),
        compiler_params=_params(2), name="attn_fox" if fox else "attn_mla",
    )(tab, qpT, qeT, ka, ke, vT)


def _softmax_pv(s, v):
    m = jnp.max(s, axis=-1, keepdims=True)
    p = jnp.exp2(s - m)
    l = jnp.sum(p, axis=-1, keepdims=True)
    return _dot(p.astype(BF16), v) / l


def _diag_heads(o_big, t_new):
    rows = lax.broadcasted_iota(jnp.int32, o_big.shape, 0) // t_new
    cols = lax.broadcasted_iota(jnp.int32, o_big.shape, 1) // HEAD_DV
    kept = jnp.where(rows == cols, o_big, 0.0)
    return jnp.sum(kept.reshape(N_HEADS, t_new, o_big.shape[1]), axis=0)


def _sample_attn_kernel(qn_ref, qr_ref, qf_ref, lrep_ref, cckv_ref, nckv_ref, ckr_ref, nkr_ref,
                        cfk_ref, nfk_ref, cfv_ref, nfv_ref, wukn_ref, wuv_ref, oa_ref, ob_ref):
    past = cckv_ref.shape[2]
    t_new = nckv_ref.shape[0]
    rows = N_HEADS * t_new
    pad = lrep_ref.shape[2] - past - t_new
    keys = past + t_new + pad

    def with_new(cache, new):
        return jnp.concatenate([cache, new, jnp.zeros((pad, new.shape[1]), new.dtype)], axis=0)

    kidx = lax.broadcasted_iota(jnp.int32, (rows, keys), 1)
    qpos = past + lax.broadcasted_iota(jnp.int32, (rows, keys), 0) % t_new
    real = kidx < past + t_new

    ckv_all = with_new(cckv_ref[0, 0], nckv_ref[...]).astype(BF16)
    kn = _dot(ckv_all, wukn_ref[0]).astype(BF16)
    va = _dot(ckv_all, wuv_ref[0]).astype(BF16)
    kr = with_new(ckr_ref[0, 0], nkr_ref[:, 0:ROPE_DIM]).astype(BF16)
    s = _dot_nt(qn_ref[0], kn) + _dot_nt(qr_ref[0], kr)
    s = jnp.where(real & ((kidx // CHUNK) <= (qpos // CHUNK)), s, NEG)
    oa_ref[...] = _diag_heads(_softmax_pv(s, va), t_new).astype(BF16)

    fk = with_new(cfk_ref[0, 0], nfk_ref[...]).astype(BF16)
    fv = with_new(cfv_ref[0, 0], nfv_ref[...]).astype(BF16)
    c = lrep_ref[0]
    shift = 1
    while shift < keys:
        c = c + jnp.where(kidx >= shift, pltpu.roll(c, shift, 1), 0.0)
        shift *= 2
    cq = jnp.sum(jnp.where(kidx == qpos, c, 0.0), axis=-1, keepdims=True)
    s = _dot_nt(qf_ref[0], fk) + (cq - c) * LOG2E
    s = jnp.where(real & (kidx <= qpos), s, NEG)
    ob_ref[...] = _diag_heads(_softmax_pv(s, fv), t_new).astype(BF16)


def _sample_attention(layer, qn_bd, qr, qf_bd, lrep, cckv, nckv, ckr, nkr, cfk, nfk, cfv, nfv, wukn, wuv):
    nbatch = cckv.shape[1]
    t_new = nckv.shape[0] // nbatch
    b3 = lambda a: pl.BlockSpec((1,) + a.shape[1:], lambda b: (b, 0, 0))
    cache = lambda a: pl.BlockSpec((1, 1) + a.shape[2:], lambda b: (layer, b, 0, 0))
    new = lambda a: pl.BlockSpec((t_new, a.shape[1]), lambda b: (b, 0))
    in_specs = [b3(qn_bd), b3(qr), b3(qf_bd), b3(lrep), cache(cckv), new(nckv), cache(ckr), new(nkr),
                cache(cfk), new(nfk), cache(cfv), new(nfv), _layer_spec(wukn, layer), _layer_spec(wuv, layer)]
    out = jax.ShapeDtypeStruct((nbatch * t_new, HEADS_W), BF16)
    o_spec = pl.BlockSpec((t_new, HEADS_W), lambda b: (b, 0))
    return pl.pallas_call(
        _sample_attn_kernel, grid=(nbatch,), in_specs=in_specs, out_specs=(o_spec, o_spec),
        out_shape=(out, out), compiler_params=_params(1), name="attn_sample",
    )(qn_bd, qr, qf_bd, lrep, cckv, nckv, ckr, nkr, cfk, nfk, cfv, nfv, wukn, wuv)


def _merge_kernel(x_ref, oa_ref, za_ref, ob_ref, zb_ref, ga_ref, gb_ref, woa_ref, wob_ref, wout_ref,
                  fg_ref, out_ref, *, final):
    a = _dot(oa_ref[...] * za_ref[...], woa_ref[0])
    b = _dot(ob_ref[...] * zb_ref[...], wob_ref[0])
    mix = ga_ref[...].astype(F32) * a + gb_ref[...].astype(F32) * b
    xn = x_ref[...] + _dot(mix.astype(BF16), wout_ref[0])
    out_ref[...] = _rms(xn, fg_ref[...]) if final else xn


def _merge(layer, x, oa, za, ob, zb, ga, gb, woa, wob, wout, fg, tm, final):
    m = x.shape[0]
    row = lambda c: pl.BlockSpec((tm, c), lambda i: (i, 0))
    in_specs = [row(D_MODEL), row(HEADS_W), row(HEADS_W), row(HEADS_W), row(HEADS_W), row(D_MODEL),
                row(D_MODEL), _layer_spec(woa, layer), _layer_spec(wob, layer), _layer_spec(wout, layer),
                _full_spec(fg.shape)]
    return pl.pallas_call(
        functools.partial(_merge_kernel, final=final), grid=(m // tm,), in_specs=in_specs,
        out_specs=row(D_MODEL), out_shape=jax.ShapeDtypeStruct((m, D_MODEL), F32),
        compiler_params=_params(1), name="merge",
    )(x, oa, za, ob, zb, ga, gb, woa, wob, wout, fg)


def _pad_cols(w, width):
    return jnp.pad(w, ((0, 0), (0, width - w.shape[1])))


def _pad_last(w, width):
    return jnp.pad(w, [(0, 0)] * (w.ndim - 1) + [(0, width - w.shape[-1])])


def _stacked_weights(w_in, w_uq, w_ukv, w_oa, w_ob, w_out, b_f, norm_g, g_q, g_kv):
    depth = w_in.shape[0]
    offs = np.cumsum(IN_SIZES)[:-1].tolist()
    w_cq, w_ckv, w_kr, w_za, w_fq, w_fk, w_fv, w_zf, w_zb, w_ga, w_gb = jnp.split(w_in, offs, axis=2)
    w_kr_swapped = jnp.concatenate([w_kr[..., HALF_ROPE:], w_kr[..., :HALF_ROPE]], axis=2)
    w_mla = jnp.concatenate([w_cq, w_ckv, _pad_last(w_kr, LANES), _pad_last(w_kr_swapped, LANES),
                             w_za, w_ga], axis=2).astype(BF16)
    w_fox = jnp.concatenate([w_fk, w_fv, _pad_last(w_zf, LANES), w_zb, w_gb], axis=2).astype(BF16)
    uq = w_uq.reshape(depth, Q_LORA, N_HEADS, NOPE_DIM + ROPE_DIM)
    uq_rows = jnp.concatenate([
        uq[..., :NOPE_DIM].reshape(depth, Q_LORA, -1),
        uq[..., NOPE_DIM:NOPE_DIM + HALF_ROPE].reshape(depth, Q_LORA, -1),
        uq[..., NOPE_DIM + HALF_ROPE:].reshape(depth, Q_LORA, -1)], axis=2)
    ukv = w_ukv.reshape(depth, KV_LORA, N_HEADS, NOPE_DIM + HEAD_DV)
    w_ukn = ukv[..., :NOPE_DIM].reshape(depth, KV_LORA, -1).astype(BF16)
    w_uv = ukv[..., NOPE_DIM:].reshape(depth, KV_LORA, -1).astype(BF16)
    tr = lambda a: jnp.swapaxes(a, 1, 2)
    return dict(
        w_mla=w_mla, w_fox=w_fox, w_uqT=tr(uq_rows).astype(BF16), w_ukn=w_ukn, w_uv=w_uv, w_uvT=tr(w_uv),
        w_fqT=tr(w_fq).astype(BF16),
        b_f=_pad_last(b_f[:, None, :], LANES),
        w_oa=w_oa.astype(BF16), w_ob=w_ob.astype(BF16), w_out=w_out.astype(BF16),
        norm_g=norm_g[:, None, :], g_q=g_q[:, None, :], g_kv=g_kv[:, None, :])


def _rope_tables(pos):
    inv = jnp.exp(-math.log(ROPE_THETA) * jnp.arange(HALF_ROPE, dtype=F32) / HALF_ROPE)
    ang = pos.astype(F32)[:, None] * inv[None, :]
    cos = jnp.cos(ang)
    sin = jnp.sin(ang)
    cosT = jnp.tile(cos.T, (N_HEADS, 1))
    sinT = jnp.tile(sin.T, (N_HEADS, 1))
    ctok = _pad_cols(jnp.concatenate([cos, cos], axis=1), LANES)
    stok = _pad_cols(jnp.concatenate([-sin, sin], axis=1), LANES)
    return cosT, sinT, ctok, stok


def _bias_placement():
    pk = np.zeros((3 * LANES, LANES), np.float32)
    pqt = np.zeros((N_HEADS * AUG_PER_HEAD, 3 * LANES), np.float32)
    for c in range(3):
        for h in range(N_HEADS):
            pk[c * LANES + h, h * AUG_PER_HEAD + 3 + c] = -1.0
            pqt[h * AUG_PER_HEAD + c, c * LANES + h] = 1.0
    return jnp.asarray(pk, BF16), jnp.asarray(pqt, BF16)


def _head_groups():
    grp = np.zeros((HEADS_W, LANES), np.float32)
    for h in range(N_HEADS):
        grp[h * HEAD_DV:(h + 1) * HEAD_DV, h] = 1.0
    return jnp.asarray(grp, BF16)


def _skip_tables(base, qn2, kn2):
    b = base[:, 0, :N_HEADS]
    qmax = jnp.sqrt(qn2[:, :, 0]) * NORM_SLACK
    kmax = lax.cummax(jnp.sqrt(kn2[:, 0, :N_HEADS]) * NORM_SLACK, axis=0)
    return jnp.stack([b, qmax, kmax]).transpose(0, 2, 1)


def _tri(n):
    return jnp.asarray(np.tril(np.ones((n, n), np.float32)), BF16)


def _block_diag_queries(qT, nbatch, t_new):
    q = qT.T.reshape(nbatch, t_new, N_HEADS, HEAD_DV)
    eye = jnp.eye(N_HEADS, dtype=q.dtype)
    return jnp.einsum('bthj,hg->bhtgj', q, eye).reshape(nbatch, N_HEADS * t_new, HEADS_W)


def _project(layer, x, sw, tables, tri, pk, pqt, grp, tm, stack_mla=None, stack_fox=None):
    cosT, sinT, ctok, stok = tables
    mla = _proj_mla(layer, x, sw['norm_g'], sw['w_mla'], sw['g_q'], sw['g_kv'], sw['w_uqT'], sw['w_ukn'],
                    sw['w_uvT'], cosT, sinT, ctok, stok, grp, tm, stack_mla)
    fox = _proj_fox(layer, x, sw['norm_g'], sw['w_fox'], sw['w_fqT'], sw['b_f'], tri, pk, pqt, grp,
                    tm, stack_fox)
    return mla, fox


def kernel(x_prompt, x_sample, cache_mla_ckv, cache_mla_krope, cache_fox_k, cache_fox_v, cache_fox_logf,
           norm_g, w_in, g_q, w_uq, g_kv, w_ukv, b_f, w_oa, w_ob, w_out, final_g):
    depth = w_in.shape[0]
    _, seq, _ = x_prompt.shape
    nbatch, t_new, _ = x_sample.shape
    past = cache_mla_ckv.shape[2]
    m_s = nbatch * t_new
    key_pad = -(past + t_new) % LANES

    tabs_p = _rope_tables(jnp.arange(seq, dtype=jnp.int32))
    tabs_s = _rope_tables(past + jnp.arange(m_s, dtype=jnp.int32) % t_new)
    pk, pqt = _bias_placement()
    grp = _head_groups()
    tri_p = _tri(SEQ_BLOCK)
    tri_s = _tri(m_s)
    fg = final_g[None, :]

    xp = x_prompt.reshape(seq, D_MODEL)
    xs = x_sample.reshape(m_s, D_MODEL)
    outs = {k: [] for k in ('p_kr', 'p_lf', 's_ckv', 's_kr', 's_fk', 's_fv', 's_lf')}
    p_ckv_stack = p_kv_stack = None
    yp = ys = None
    sw = _stacked_weights(w_in, w_uq, w_ukv, w_oa, w_ob, w_out, b_f, norm_g, g_q, g_kv)
    for l in range(depth):
        final = l == depth - 1

        mla, fox = _project(l, xp, sw, tabs_p, tri_p, pk, pqt, grp, SEQ_BLOCK,
                            (l, depth, p_ckv_stack), (l, depth, p_kv_stack))
        ckv, kr, za, ga, qnT, qrT, knb, krb, vT, qn2_a, kn2_a = mla
        fk, fv, logf, zb, gb, fqT, fkb, fvT, augk, augqT, base, qn2, kn2 = fox
        tab = _skip_tables(base, qn2, kn2)
        tab_a = _skip_tables(jnp.zeros_like(base), qn2_a, kn2_a)
        o_a = _attention(tab_a, qnT, qrT, knb, krb, vT, fox=False)
        o_b = _attention(tab, fqT, augqT, fkb, augk, fvT, fox=True)
        res = _merge(l, xp, o_a, za, o_b, zb, ga, gb, sw['w_oa'], sw['w_ob'], sw['w_out'], fg, SEQ_BLOCK, final)
        xp = yp = res
        p_ckv_stack, p_kv_stack = (ckv,), (fk, fv)
        outs['p_kr'].append(kr[:, :ROPE_DIM]); outs['p_lf'].append(logf)

        mla, fox = _project(l, xs, sw, tabs_s, tri_s, pk, pqt, grp, m_s)
        ckv, kr, za, ga, qnT, qrT = mla[:6]
        fk, fv, logf, zb, gb, fqT = fox[:6]
        qn_bd = _block_diag_queries(qnT, nbatch, t_new)
        qf_bd = _block_diag_queries(fqT, nbatch, t_new)
        qr = qrT.reshape(N_HEADS, ROPE_DIM, nbatch, t_new).transpose(2, 0, 3, 1).reshape(
            nbatch, N_HEADS * t_new, ROPE_DIM)
        lf_all = jnp.concatenate([cache_fox_logf[l], logf.reshape(nbatch, t_new, N_HEADS),
                                  jnp.zeros((nbatch, key_pad, N_HEADS), F32)], axis=1)
        lrep = jnp.repeat(lf_all.transpose(0, 2, 1), t_new, axis=1)
        o_a, o_b = _sample_attention(
            l, qn_bd, qr, qf_bd, lrep, cache_mla_ckv, ckv, cache_mla_krope, kr,
            cache_fox_k.reshape(depth, nbatch, past, HEADS_W), fk,
            cache_fox_v.reshape(depth, nbatch, past, HEADS_W), fv, sw['w_ukn'], sw['w_uv'])
        res = _merge(l, xs, o_a, za, o_b, zb, ga, gb, sw['w_oa'], sw['w_ob'], sw['w_out'], fg, m_s, final)
        xs = ys = res
        outs['s_ckv'].append(ckv); outs['s_kr'].append(kr[:, :ROPE_DIM]); outs['s_fk'].append(fk)
        outs['s_fv'].append(fv); outs['s_lf'].append(logf)

    st = lambda name, shape: jnp.stack(outs[name]).reshape((depth,) + shape)
    return (yp.reshape(1, seq, D_MODEL), ys.reshape(nbatch, t_new, D_MODEL),
            p_ckv_stack[0].reshape(depth, 1, seq, KV_LORA), st('p_kr', (1, seq, ROPE_DIM)),
            p_kv_stack[0].reshape(depth, 1, seq, N_HEADS, HEAD_DV),
            p_kv_stack[1].reshape(depth, 1, seq, N_HEADS, HEAD_DV),
            st('p_lf', (1, seq, N_HEADS)),
            st('s_ckv', (nbatch, t_new, KV_LORA)), st('s_kr', (nbatch, t_new, ROPE_DIM)),
            st('s_fk', (nbatch, t_new, N_HEADS, HEAD_DV)), st('s_fv', (nbatch, t_new, N_HEADS, HEAD_DV)),
            st('s_lf', (nbatch, t_new, N_HEADS)))
```
